```python
import jax, jax.numpy as jnp
from jax import lax
import numpy as np

D_MODEL = 1024
BATCH = 32
SEQ = 256
DEPTH = 2
DEC_BATCH = 8
DEC_SEQ = 2048
PAST_LEN = 512

GRID_W = 64
RET_HEADS = 4
RET_DK = 128
RET_DV = 256
RET_QK_W = RET_HEADS * RET_DK
RET_V_W = RET_HEADS * RET_DV
RET_CHUNK = 128
NA_HEADS = 8
NA_DH = 64
NA_W = NA_HEADS * NA_DH
NA_WIN_H = 8
NA_WIN_W = 16
Q_BLOCK = 128
D_FF = -(-8 * D_MODEL // (3 * 256)) * 256
ROPE_BASE = 10000.0
EPS = 1e-6
IN_WIDTHS = (RET_QK_W, RET_QK_W, RET_V_W, RET_V_W, NA_W, NA_W, NA_W)
IN_WIDTH = sum(IN_WIDTHS)

kernel_name = 'hybrid_retention_natten_diffusion_step'


def rmsnorm(x, g):
    xf = x.astype(jnp.float32)
    y = xf * lax.rsqrt(jnp.mean(xf * xf, axis=-1, keepdims=True) + EPS)
    return (y * g.astype(jnp.float32)).astype(x.dtype)


def adaln(cvec, w_ada, b_ada):
    return jnp.split(jax.nn.silu(cvec) @ w_ada + b_ada, 6, axis=-1)


def modulate(h, shift, scale):
    return h * (1 + scale[:, None, :]) + shift[:, None, :]


def project_in(hm, w_in):
    b, l = hm.shape[:2]
    points = [int(p) for p in np.cumsum(IN_WIDTHS)[:-1]]
    rq, rk, rv, rg, nq, nk, nv = jnp.split(hm @ w_in, points, axis=-1)
    return (rq.reshape(b, l, RET_HEADS, RET_DK), rk.reshape(b, l, RET_HEADS, RET_DK),
            rv.reshape(b, l, RET_HEADS, RET_DV), rg,
            nq.reshape(b, l, NA_HEADS, NA_DH), nk.reshape(b, l, NA_HEADS, NA_DH),
            nv.reshape(b, l, NA_HEADS, NA_DH))


def rope_2d(x):
    l = x.shape[1]
    t = jnp.arange(l)
    row = (t // GRID_W).astype(jnp.float32)
    col = (t % GRID_W).astype(jnp.float32)
    n_freq = x.shape[-1] // 4
    inv = ROPE_BASE ** (-jnp.arange(n_freq, dtype=jnp.float32) / n_freq)
    ang = jnp.concatenate([row[:, None] * inv, col[:, None] * inv], axis=-1)
    cos = jnp.cos(ang)[None, :, None, :]
    sin = jnp.sin(ang)[None, :, None, :]
    x1, x2 = jnp.split(x.astype(jnp.float32), 2, axis=-1)
    return jnp.concatenate([x1 * cos - x2 * sin, x1 * sin + x2 * cos], axis=-1).astype(x.dtype)


def retention_chunked(q, k, v, log_gamma, s0):
    b, l, h, dk = q.shape
    dv = v.shape[-1]
    n = l // RET_CHUNK
    pos = jnp.arange(RET_CHUNK, dtype=jnp.float32)
    rel = pos[:, None] - pos[None, :]
    decay_in = jnp.where(rel >= 0, jnp.exp(log_gamma[:, None, None] * jnp.maximum(rel, 0.0)), 0.0)
    xi = jnp.exp(log_gamma[:, None] * (pos + 1.0)).T[None, :, :, None]
    zeta = jnp.exp(log_gamma[:, None] * (RET_CHUNK - 1.0 - pos)).T[None, :, :, None]
    g_chunk = jnp.exp(log_gamma * RET_CHUNK)[None, :, None, None]

    def chunks(a):
        return jnp.moveaxis(a.reshape(b, n, RET_CHUNK, *a.shape[2:]), 1, 0)

    def step(s, inp):
        qc, kc, vc = inp
        scores = jnp.einsum('bnhd,bmhd->bhnm', qc, kc) * decay_in
        inner = jnp.einsum('bhnm,bmhe->bnhe', scores, vc)
        cross = jnp.einsum('bnhd,bhde->bnhe', qc, s) * xi
        s_new = g_chunk * s + jnp.einsum('bmhd,bmhe->bhde', kc * zeta, vc)
        return s_new, inner + cross

    s_fin, out = lax.scan(step, s0, (chunks(q), chunks(k), chunks(v)))
    return jnp.moveaxis(out, 0, 1).reshape(b, l, h, dv), s_fin


def retention_bidir(q, k, v, log_g, s0_f, s0_b):
    y_f, s_f = retention_chunked(q, k, v, log_g[0], s0_f)
    fl = lambda a: jnp.flip(a, axis=1)
    y_b, s_b = retention_chunked(fl(q), fl(k), fl(v), log_g[1], s0_b)
    return y_f + fl(y_b), s_f, s_b


def retention_output(y, gate, gn_gain):
    b, l = y.shape[:2]
    mu = jnp.mean(y, axis=-1, keepdims=True)
    var = jnp.mean(jnp.square(y - mu), axis=-1, keepdims=True)
    yn = ((y - mu) * lax.rsqrt(var + EPS)).reshape(b, l, RET_V_W) * gn_gain.astype(jnp.float32)
    return (jax.nn.silu(gate.astype(jnp.float32)) * yn).astype(gate.dtype)


def context_attention(q, k, v):
    b, lc, h, d = q.shape
    nb = lc // Q_BLOCK
    qb = jnp.moveaxis(q.reshape(b, nb, Q_BLOCK, h, d), 1, 0)

    def blk(qi):
        s = jnp.einsum('bqhd,bkhd->bhqk', qi, k).astype(jnp.float32)
        p = jax.nn.softmax(s, axis=-1).astype(v.dtype)
        return jnp.einsum('bhqk,bkhd->bqhd', p, v)

    o = lax.map(blk, qb)
    return jnp.moveaxis(o, 0, 1).reshape(b, lc, h * d)


def neighbourhood_attention(q, k, v, k_ctx, v_ctx, rpb):
    b, l, h, d = q.shape
    rows = l // GRID_W
    wh = min(NA_WIN_H, rows)
    ww = NA_WIN_W
    n_keys = wh * ww
    cols = jnp.arange(GRID_W)
    col_start = jnp.clip(cols - ww // 2, 0, GRID_W - ww)
    key_cols = col_start[:, None] + jnp.arange(ww)[None, :]
    col_off = key_cols - cols[:, None] + (NA_WIN_W - 1)
    q_rows = jnp.moveaxis(q.reshape(b, rows, GRID_W, h, d), 1, 0)

    def row_block(args):
        r, qr = args
        row_start = jnp.clip(r - wh // 2, 0, rows - wh)
        key_rows = row_start + jnp.arange(wh)
        idx = (key_rows[None, :, None] * GRID_W + key_cols[:, None, :]).reshape(GRID_W * n_keys)
        kw = jnp.take(k, idx, axis=1).reshape(b, GRID_W, n_keys, h, d)
        vw = jnp.take(v, idx, axis=1).reshape(b, GRID_W, n_keys, h, d)
        row_off = key_rows - r + (NA_WIN_H - 1)
        bias = rpb[:, row_off[:, None, None], col_off[None, :, :]]
        bias = jnp.transpose(bias, (0, 2, 1, 3)).reshape(h, GRID_W, n_keys).astype(jnp.float32)
        s_loc = jnp.einsum('bwhd,bwkhd->bhwk', qr, kw).astype(jnp.float32) + bias
        s_ctx = jnp.einsum('bwhd,bshd->bhws', qr, k_ctx).astype(jnp.float32)
        p = jax.nn.softmax(jnp.concatenate([s_loc, s_ctx], axis=-1), axis=-1).astype(v.dtype)
        return (jnp.einsum('bhwk,bwkhd->bwhd', p[..., :n_keys], vw)
                + jnp.einsum('bhws,bshd->bwhd', p[..., n_keys:], v_ctx))

    o = lax.map(row_block, (jnp.arange(rows), q_rows))
    return jnp.moveaxis(o, 0, 1).reshape(b, l, h * d)


def swiglu(h, wg, wu, wd):
    return (jax.nn.silu(h @ wg) * (h @ wu)) @ wd


def block_tail(x, hm, y_ret, y_na, g1, sh2, sc2, g2, w_ret_out, w_na_out, w_gate, w_o,
               n_post_mix, n_pre_ffn, n_post_ffn, w_ffn_gate, w_ffn_up, w_ffn_down):
    g_ret, g_na = jnp.split(jax.nn.sigmoid(hm @ w_gate), 2, axis=-1)
    mixed = (g_ret * (y_ret @ w_ret_out) + g_na * (y_na @ w_na_out)) @ w_o
    x = x + g1[:, None, :] * rmsnorm(mixed, n_post_mix)
    hf = modulate(rmsnorm(x, n_pre_ffn), sh2, sc2)
    return x + g2[:, None, :] * rmsnorm(swiglu(hf, w_ffn_gate, w_ffn_up, w_ffn_down), n_post_ffn)


def setup_inputs(seed: int = 0) -> dict:
    key = jax.random.key(seed)
    ks = jax.random.split(key, 24)
    f32 = jnp.float32
    nrm = lambda k, shape, s: jax.random.normal(k, shape, f32) * s
    gamma0 = 1.0 - 2.0 ** (-5.0 - np.arange(RET_HEADS, dtype=np.float32))
    logit0 = jnp.asarray(np.log(gamma0 / (1.0 - gamma0)), f32)
    return {
        'x_prompt': nrm(ks[0], (BATCH, SEQ, D_MODEL), 1.0),
        'x_sample': nrm(ks[1], (DEC_BATCH, DEC_SEQ, D_MODEL), 1.0),
        'cache_na_k': nrm(ks[2], (DEC_BATCH, DEPTH, PAST_LEN, NA_HEADS, NA_DH), 1.0),
        'cache_na_v': nrm(ks[3], (DEC_BATCH, DEPTH, PAST_LEN, NA_HEADS, NA_DH), 1.0),
        'state_ret': nrm(ks[4], (DEC_BATCH, DEPTH, 2, RET_HEADS, RET_DK, RET_DV), 0.5),
        'c': nrm(ks[5], (DEC_BATCH, D_MODEL), 1.0),
        'c_ctx': nrm(ks[6], (D_MODEL,), 1.0),
        'w_ada': nrm(ks[7], (DEPTH, D_MODEL, 6 * D_MODEL), 0.5 * D_MODEL ** -0.5),
        'b_ada': nrm(ks[8], (DEPTH, 6 * D_MODEL), 0.01),
        'norm_pre_mix': 1.0 + nrm(ks[9], (DEPTH, D_MODEL), 0.05),
        'norm_post_mix': 1.0 + nrm(ks[10], (DEPTH, D_MODEL), 0.05),
        'norm_pre_ffn': 1.0 + nrm(ks[11], (DEPTH, D_MODEL), 0.05),
        'norm_post_ffn': 1.0 + nrm(ks[12], (DEPTH, D_MODEL), 0.05),
        'w_in': nrm(ks[13], (DEPTH, D_MODEL, IN_WIDTH), D_MODEL ** -0.5),
        'ret_decay_logit': logit0[None, None, :] + nrm(ks[14], (DEPTH, 2, RET_HEADS), 0.1),
        'ret_gn_gain': 1.0 + nrm(ks[15], (DEPTH, RET_V_W), 0.05),
        'na_rpb': nrm(ks[16], (DEPTH, NA_HEADS, 2 * NA_WIN_H - 1, 2 * NA_WIN_W - 1), 0.02),
        'w_ret_out': nrm(ks[17], (DEPTH, RET_V_W, D_MODEL), RET_V_W ** -0.5),
        'w_na_out': nrm(ks[18], (DEPTH, NA_W, D_MODEL), NA_W ** -0.5),
        'w_gate': nrm(ks[19], (DEPTH, D_MODEL, 2 * D_MODEL), D_MODEL ** -0.5),
        'w_o': nrm(ks[20], (DEPTH, D_MODEL, D_MODEL), D_MODEL ** -0.5),
        'w_ffn_gate': nrm(ks[21], (DEPTH, D_MODEL, D_FF), D_MODEL ** -0.5),
        'w_ffn_up': nrm(ks[22], (DEPTH, D_MODEL, D_FF), D_MODEL ** -0.5),
        'w_ffn_down': nrm(ks[23], (DEPTH, D_FF, D_MODEL), D_FF ** -0.5),
    }


def reference(x_prompt, x_sample, cache_na_k, cache_na_v, state_ret, c, c_ctx, w_ada, b_ada,
              norm_pre_mix, norm_post_mix, norm_pre_ffn, norm_post_ffn, w_in, ret_decay_logit,
              ret_gn_gain, na_rpb, w_ret_out, w_na_out, w_gate, w_o, w_ffn_gate, w_ffn_up,
              w_ffn_down):
    f32 = jnp.float32
    q_scale = NA_DH ** -0.5
    k_scale = RET_DK ** -0.5

    x = x_prompt
    b_p = x.shape[0]
    k_list, v_list, s_list = [], [], []
    for l in range(DEPTH):
        sh1, sc1, g1, sh2, sc2, g2 = adaln(c_ctx[None, :], w_ada[l], b_ada[l])
        hm = modulate(rmsnorm(x, norm_pre_mix[l]), sh1, sc1)
        rq, rk, rv, rg, nq, nk, nv = project_in(hm, w_in[l])
        log_g = jax.nn.log_sigmoid(ret_decay_logit[l].astype(f32))
        s0 = jnp.zeros((b_p, RET_HEADS, RET_DK, RET_DV), f32)
        y, s_f, s_b = retention_bidir(rq.astype(f32), rk.astype(f32) * k_scale, rv.astype(f32),
                                      log_g, s0, s0)
        y_ret = retention_output(y, rg, ret_gn_gain[l])
        y_na = context_attention(nq * q_scale, nk, nv)
        x = block_tail(x, hm, y_ret, y_na, g1, sh2, sc2, g2, w_ret_out[l], w_na_out[l], w_gate[l],
                       w_o[l], norm_post_mix[l], norm_pre_ffn[l], norm_post_ffn[l],
                       w_ffn_gate[l], w_ffn_up[l], w_ffn_down[l])
        k_list.append(nk)
        v_list.append(nv)
        s_list.append(jnp.stack([s_f, s_b], axis=1).astype(x_prompt.dtype))
    y_prompt = x
    new_cache_na_k = jnp.stack(k_list, axis=1)
    new_cache_na_v = jnp.stack(v_list, axis=1)
    new_state_ret = jnp.stack(s_list, axis=1)

    x = x_sample
    for l in range(DEPTH):
        sh1, sc1, g1, sh2, sc2, g2 = adaln(c, w_ada[l], b_ada[l])
        hm = modulate(rmsnorm(x, norm_pre_mix[l]), sh1, sc1)
        rq, rk, rv, rg, nq, nk, nv = project_in(hm, w_in[l])
        rq = rope_2d(rq)
        rk = rope_2d(rk)
        log_g = jax.nn.log_sigmoid(ret_decay_logit[l].astype(f32))
        y, _, _ = retention_bidir(rq.astype(f32), rk.astype(f32) * k_scale, rv.astype(f32), log_g,
                                  state_ret[:, l, 0].astype(f32), state_ret[:, l, 1].astype(f32))
        y_ret = retention_output(y, rg, ret_gn_gain[l])
        y_na = neighbourhood_attention(nq * q_scale, nk, nv, cache_na_k[:, l], cache_na_v[:, l],
                                       na_rpb[l])
        x = block_tail(x, hm, y_ret, y_na, g1, sh2, sc2, g2, w_ret_out[l], w_na_out[l], w_gate[l],
                       w_o[l], norm_post_mix[l], norm_pre_ffn[l], norm_post_ffn[l],
                       w_ffn_gate[l], w_ffn_up[l], w_ffn_down[l])
    y_sample = x
    return (y_prompt, y_sample, new_cache_na_k, new_cache_na_v, new_state_ret)
```

```python
import functools

import numpy as np
import jax
import jax.numpy as jnp
from jax import lax
from jax.experimental import pallas as pl
from jax.experimental.pallas import tpu as pltpu

F32 = jnp.float32
BF16 = jnp.bfloat16

D_MODEL = 1024
DEPTH = 2
GRID_W = 64
RET_HEADS = 4
RET_DK = 128
RET_DV = 256
RET_QK_W = RET_HEADS * RET_DK
RET_V_W = RET_HEADS * RET_DV
RET_CHUNK = 128
NA_HEADS = 8
NA_DH = 64
NA_W = NA_HEADS * NA_DH
NA_WIN_H = 8
NA_WIN_W = 16
D_FF = 2816
ROPE_BASE = 10000.0
EPS = 1e-6
IN_WIDTH = 2 * RET_QK_W + 2 * RET_V_W + 3 * NA_W

LANES = 128
HEAD_PAIRS = NA_W // LANES
MOD_ROWS = 16
NA_Q_ROWS = 2
NA_Q_BLOCK = NA_Q_ROWS * GRID_W
NA_KEY_BLOCKS = 5
NA_KEY_ROWS = NA_KEY_BLOCKS * LANES // GRID_W
NA_PATTERNS = (0, 1, 2, 14, 15)
MASKED = -1e30
VMEM_LIMIT = 56 * 1024 * 1024
TOKEN_TILE = 256


def _sigmoid(x):
    return 1.0 / (1.0 + jnp.exp(-x))


def _silu(x):
    return x * _sigmoid(x)


def _rms(x, g):
    return x * lax.rsqrt(jnp.mean(x * x, axis=-1, keepdims=True) + EPS) * g


def _dot(a, b):
    return jnp.dot(a, b, preferred_element_type=F32)


def _dot_nt(a, b):
    return lax.dot_general(a, b, (((1,), (1,)), ((), ())), preferred_element_type=F32)


def _resident(shape):
    zeros = (0,) * len(shape)
    return pl.BlockSpec(shape, lambda *_: zeros, pipeline_mode=pl.Buffered(1))


def _params(*sem):
    return pltpu.CompilerParams(dimension_semantics=sem, vmem_limit_bytes=VMEM_LIMIT)


def _adaln_kernel(c_ref, w_ref, b_ref, o_ref):
    s = _silu(c_ref[...]).astype(BF16)
    o_ref[0] = _dot(s, w_ref[0].astype(BF16)) + b_ref[0]


def _adaln(cvec, w_ada, b_ada):
    tn = 1536
    n = 6 * D_MODEL
    return pl.pallas_call(
        _adaln_kernel,
        grid=(DEPTH, n // tn),
        in_specs=[
            pl.BlockSpec((MOD_ROWS, D_MODEL), lambda l, j: (0, 0)),
            pl.BlockSpec((1, D_MODEL, tn), lambda l, j: (l, 0, j)),
            pl.BlockSpec((1, 1, tn), lambda l, j: (l, 0, j)),
        ],
        out_specs=pl.BlockSpec((1, MOD_ROWS, tn), lambda l, j: (l, 0, j)),
        out_shape=jax.ShapeDtypeStruct((DEPTH, MOD_ROWS, n), F32),
        compiler_params=_params("arbitrary", "arbitrary"),
        name="adaln",
    )(cvec, w_ada, b_ada.reshape(DEPTH, 1, n))


def _premix_kernel(*refs, rope):
    if rope:
        (x_ref, mod_ref, g_ref, w_in_ref, w_gate_ref, cos_ref, sin_ref,
         rq_ref, rk_ref, rv_ref, rg_ref, nq_ref, nk_ref, nv_ref, gate_ref) = refs
    else:
        (x_ref, mod_ref, g_ref, w_in_ref, w_gate_ref,
         rq_ref, rk_ref, rv_ref, rg_ref, nq_ref, nk_ref, nv_ref, gate_ref) = refs
    mod = mod_ref[0]
    sh1 = mod[:, 0:D_MODEL]
    sc1 = mod[:, D_MODEL:2 * D_MODEL]
    hm = (_rms(x_ref[...], g_ref[...]) * (1.0 + sc1) + sh1).astype(BF16)

    def proj(lo, width):
        return _dot(hm, w_in_ref[:, lo:lo + width])

    def rotary(t):
        if not rope:
            return t
        cos = cos_ref[...]
        sin = sin_ref[...]
        heads = []
        for h in range(RET_HEADS):
            blk = t[:, h * RET_DK:(h + 1) * RET_DK]
            heads.append(blk * cos + pltpu.roll(blk, RET_DK // 2, axis=1) * sin)
        return jnp.concatenate(heads, axis=1)

    lo = 0
    rq_ref[...] = rotary(proj(lo, RET_QK_W)).astype(rq_ref.dtype)
    lo += RET_QK_W
    rk_ref[...] = (rotary(proj(lo, RET_QK_W)) * (RET_DK ** -0.5)).astype(rk_ref.dtype)
    lo += RET_QK_W
    rv_ref[...] = proj(lo, RET_V_W).astype(rv_ref.dtype)
    lo += RET_V_W
    rg_ref[...] = proj(lo, RET_V_W).astype(rg_ref.dtype)
    lo += RET_V_W
    nq_ref[...] = (proj(lo, NA_W) * (NA_DH ** -0.5)).astype(nq_ref.dtype)
    lo += NA_W
    nk_ref[...] = proj(lo, NA_W).astype(nk_ref.dtype)
    lo += NA_W
    nv_ref[...] = proj(lo, NA_W).astype(nv_ref.dtype)
    gate_ref[...] = _sigmoid(_dot(hm, w_gate_ref[...])).astype(gate_ref.dtype)


def _premix(x, mod, mod_row, g, w_in, w_gate, rope_tabs, kv_dtype):
    b, l, _ = x.shape
    tm = TOKEN_TILE
    tiles = l // tm
    tok = lambda w: pl.BlockSpec((None, tm, w), lambda i, j: (i, j, 0))
    in_specs = [
        tok(D_MODEL),
        pl.BlockSpec((1, 1, 6 * D_MODEL), lambda i, j: (mod_row(i), 0, 0)),
        _resident((1, D_MODEL)),
        _resident((D_MODEL, IN_WIDTH)),
        _resident((D_MODEL, 2 * D_MODEL)),
    ]
    args = [x, mod, g.reshape(1, D_MODEL), w_in, w_gate]
    if rope_tabs is not None:
        in_specs += [pl.BlockSpec((tm, RET_DK), lambda i, j: (j, 0))] * 2
        args += list(rope_tabs)
    widths = (RET_QK_W, RET_QK_W, RET_V_W, RET_V_W, NA_W, NA_W, NA_W, 2 * D_MODEL)
    dtypes = (BF16, BF16, BF16, BF16, BF16, kv_dtype, kv_dtype, BF16)
    return pl.pallas_call(
        functools.partial(_premix_kernel, rope=rope_tabs is not None),
        grid=(b, tiles),
        in_specs=in_specs,
        out_specs=[tok(w) for w in widths],
        out_shape=[jax.ShapeDtypeStruct((b, l, w), d) for w, d in zip(widths, dtypes)],
        compiler_params=_params("arbitrary", "arbitrary"),
        name="premix",
    )(*args)


def _retention_kernel(*refs, n_chunks, has_s0, write_state):
    refs = list(refs)
    lg_ref, q_ref, k_ref, v_ref, rg_ref, gain_ref = refs[:6]
    pos = 6
    s0_ref = None
    if has_s0:
        s0_ref = refs[pos]
        pos += 1
    y_ref = refs[pos]
    pos += 1
    sout_ref = None
    if write_state:
        sout_ref = refs[pos]
        pos += 1
    s_scr = refs[pos]

    c = RET_CHUNK
    h = pl.program_id(1)
    lgf = lg_ref[0, h]
    lgb = lg_ref[1, h]
    row = lax.broadcasted_iota(jnp.int32, (c, c), 0).astype(F32)
    col = lax.broadcasted_iota(jnp.int32, (c, c), 1).astype(F32)
    rel = row - col
    decay = (jnp.where(rel >= 0, jnp.exp(lgf * jnp.maximum(rel, 0.0)), 0.0)
             + jnp.where(rel <= 0, jnp.exp(lgb * jnp.maximum(-rel, 0.0)), 0.0))
    p = lax.broadcasted_iota(jnp.int32, (c, 1), 0).astype(F32)
    xi_f = jnp.exp(lgf * (p + 1.0))
    xi_b = jnp.exp(lgb * (c - p))
    zeta_f = jnp.exp(lgf * (c - 1.0 - p))
    zeta_b = jnp.exp(lgb * p)
    chunk_len = jnp.full((1, RET_DV), float(c), F32)
    g_f = jnp.exp(lgf * chunk_len)
    g_b = jnp.exp(lgb * chunk_len)

    def chunk(ref, i):
        return ref[pl.ds(pl.multiple_of(i * c, c), c), :]

    def outer_kv(i, zeta):
        kz = (chunk(k_ref, i).astype(F32) * zeta).T.astype(BF16)
        return _dot(kz, chunk(v_ref, i))

    if has_s0:
        s_f0 = s0_ref[0]
        s_b0 = s0_ref[1]
    else:
        s_f0 = jnp.zeros((RET_DK, RET_DV), F32)
        s_b0 = s_f0

    def scan(t, carry):
        s_f, s_b = carry
        i_b = n_chunks - 1 - t
        s_scr[t, 0:RET_DK, :] = s_f.astype(BF16)
        s_scr[i_b, RET_DK:2 * RET_DK, :] = s_b.astype(BF16)
        return (g_f * s_f + outer_kv(t, zeta_f), g_b * s_b + outer_kv(i_b, zeta_b))

    s_f, s_b = lax.fori_loop(0, n_chunks, scan, (s_f0, s_b0))
    if write_state:
        sout_ref[0] = s_f
        sout_ref[1] = s_b

    gain = gain_ref[...]

    def emit(i, _):
        qi = chunk(q_ref, i)
        scores = (_dot_nt(qi, chunk(k_ref, i)) * decay).astype(BF16)
        qf = qi.astype(F32)
        qx = jnp.concatenate([(qf * xi_f).astype(BF16), (qf * xi_b).astype(BF16)], axis=1)
        y = _dot(scores, chunk(v_ref, i)) + _dot(qx, s_scr[i])
        mu = jnp.mean(y, axis=-1, keepdims=True)
        d = y - mu
        var = jnp.mean(d * d, axis=-1, keepdims=True)
        yn = d * lax.rsqrt(var + EPS) * gain
        out = _silu(chunk(rg_ref, i).astype(F32)) * yn
        y_ref[pl.ds(pl.multiple_of(i * c, c), c), :] = out.astype(y_ref.dtype)
        return 0

    lax.fori_loop(0, n_chunks, emit, 0)


def _retention(log_g, rq, rk, rv, rg, gain, state, layer, write_state):
    b, l, _ = rq.shape
    n_chunks = l // RET_CHUNK
    in_specs = [
        pl.BlockSpec(memory_space=pltpu.SMEM),
        pl.BlockSpec((None, l, RET_DK), lambda i, h: (i, 0, h)),
        pl.BlockSpec((None, l, RET_DK), lambda i, h: (i, 0, h)),
        pl.BlockSpec((None, l, RET_DV), lambda i, h: (i, 0, h)),
        pl.BlockSpec((None, l, RET_DV), lambda i, h: (i, 0, h)),
        pl.BlockSpec((1, RET_DV), lambda i, h: (0, h)),
    ]
    args = [log_g, rq, rk, rv, rg, gain.reshape(1, RET_V_W)]
    if state is not None:
        in_specs.append(pl.BlockSpec((None, None, 2, None, RET_DK, RET_DV),
                                     lambda i, h: (i, layer, 0, h, 0, 0)))
        args.append(state)
    out_specs = [pl.BlockSpec((None, l, RET_DV), lambda i, h: (i, 0, h))]
    out_shape = [jax.ShapeDtypeStruct((b, l, RET_V_W), BF16)]
    if write_state:
        out_specs.append(pl.BlockSpec((None, 2, None, RET_DK, RET_DV),
                                      lambda i, h: (i, 0, h, 0, 0)))
        out_shape.append(jax.ShapeDtypeStruct((b, 2, RET_HEADS, RET_DK, RET_DV), F32))
    return pl.pallas_call(
        functools.partial(_retention_kernel, n_chunks=n_chunks, has_s0=state is not None,
                          write_state=write_state),
        grid=(b, RET_HEADS),
        in_specs=in_specs,
        out_specs=out_specs,
        out_shape=out_shape,
        scratch_shapes=[pltpu.VMEM((n_chunks, 2 * RET_DK, RET_DV), BF16)],
        compiler_params=_params("arbitrary", "arbitrary"),
        name="retention",
    )(*args)


def _head_masks():
    lane = lax.broadcasted_iota(jnp.int32, (1, LANES), 1)
    return lane < NA_DH, lane >= NA_DH


def _softmax_pv(score_blocks, value_blocks):
    m = score_blocks[0].max(axis=-1, keepdims=True)
    for s in score_blocks[1:]:
        m = jnp.maximum(m, s.max(axis=-1, keepdims=True))
    denom = None
    acc = None
    for s, v in zip(score_blocks, value_blocks):
        e = jnp.exp(s - m)
        part = e.sum(axis=-1, keepdims=True)
        pv = _dot(e.astype(BF16), v)
        denom = part if denom is None else denom + part
        acc = pv if acc is None else acc + pv
    return acc * (1.0 / denom)


def _ctx_attn_kernel(q_ref, k_ref, v_ref, o_ref):
    lo_mask, hi_mask = _head_masks()
    for hp in range(HEAD_PAIRS):
        sl = slice(hp * LANES, (hp + 1) * LANES)
        q2 = q_ref[:, sl]
        k2 = k_ref[:, sl].astype(BF16)
        v2 = v_ref[:, sl].astype(BF16)
        halves = []
        for mask in (lo_mask, hi_mask):
            qa = jnp.where(mask, q2, jnp.zeros_like(q2))
            halves.append(_softmax_pv([_dot_nt(qa, k2)], [v2]))
        o_ref[:, sl] = jnp.where(lo_mask, halves[0], halves[1]).astype(o_ref.dtype)


def _ctx_attention(nq, nk, nv):
    b, l, _ = nq.shape
    spec = pl.BlockSpec((None, l, NA_W), lambda i: (i, 0, 0))
    return pl.pallas_call(
        _ctx_attn_kernel,
        grid=(b,),
        in_specs=[spec, spec, spec],
        out_specs=spec,
        out_shape=jax.ShapeDtypeStruct((b, l, NA_W), BF16),
        compiler_params=_params("arbitrary"),
        name="ctx_attention",
    )(nq, nk, nv)


def _na_window_start(j, rows):
    return int(np.clip(NA_Q_ROWS * j - NA_WIN_H // 2, 0, rows - NA_KEY_ROWS))


def _na_bias_table(rpb, rows):
    a = np.arange(NA_Q_ROWS)[:, None, None, None]
    qc = np.arange(GRID_W)[None, :, None, None]
    kk = np.arange(NA_KEY_ROWS)[None, None, :, None]
    kc = np.arange(GRID_W)[None, None, None, :]
    shape = (NA_Q_ROWS, GRID_W, NA_KEY_ROWS, GRID_W)
    drs, dcs, valids = [], [], []
    for j in NA_PATTERNS:
        qr = NA_Q_ROWS * j + a
        kr = _na_window_start(j, rows) + kk
        rs = np.clip(qr - NA_WIN_H // 2, 0, rows - NA_WIN_H)
        cs = np.clip(qc - NA_WIN_W // 2, 0, GRID_W - NA_WIN_W)
        valid = (kr >= rs) & (kr < rs + NA_WIN_H) & (kc >= cs) & (kc < cs + NA_WIN_W)
        dr = np.clip(kr - qr + NA_WIN_H - 1, 0, 2 * NA_WIN_H - 2)
        dc = np.clip(kc - qc + NA_WIN_W - 1, 0, 2 * NA_WIN_W - 2)
        drs.append(np.broadcast_to(dr, shape).reshape(NA_Q_BLOCK, -1))
        dcs.append(np.broadcast_to(dc, shape).reshape(NA_Q_BLOCK, -1))
        valids.append(np.broadcast_to(valid, shape).reshape(NA_Q_BLOCK, -1))
    dr, dc, valid = np.stack(drs), np.stack(dcs), np.stack(valids)
    bias = rpb.astype(F32)[:, dr, dc]
    return jnp.where(valid[None], bias, MASKED).transpose(1, 0, 2, 3)


def _na_kernel(*refs):
    q_ref = refs[0]
    k_refs = refs[1:1 + NA_KEY_BLOCKS]
    v_refs = refs[1 + NA_KEY_BLOCKS:1 + 2 * NA_KEY_BLOCKS]
    ck_ref, cv_ref, tab_ref, o_ref = refs[1 + 2 * NA_KEY_BLOCKS:]
    j = pl.program_id(1)
    last = pl.num_programs(1) - 1
    pattern = jnp.minimum(j, 2) + jnp.maximum(j - (last - 2), 0)
    lo_mask, hi_mask = _head_masks()
    for hp in range(HEAD_PAIRS):
        sl = slice(hp * LANES, (hp + 1) * LANES)
        q2 = q_ref[:, sl]
        keys = [r[:, sl] for r in k_refs]
        vals = [r[:, sl] for r in v_refs] + [cv_ref[:, sl]]
        ck2 = ck_ref[:, sl]
        halves = []
        for half, mask in enumerate((lo_mask, hi_mask)):
            qa = jnp.where(mask, q2, jnp.zeros_like(q2))
            head = 2 * hp + half
            scores = [_dot_nt(qa, kt) + tab_ref[pattern, head, :, t * LANES:(t + 1) * LANES]
                      for t, kt in enumerate(keys)]
            scores.append(_dot_nt(qa, ck2))
            halves.append(_softmax_pv(scores, vals))
        o_ref[:, sl] = jnp.where(lo_mask, halves[0], halves[1]).astype(o_ref.dtype)


def _na_attention(nq, nk, nv, ck, cv, table):
    b, l, _ = nq.shape
    n_blocks = l // NA_Q_BLOCK
    assert [min(j, 2) + max(j - (n_blocks - 3), 0) for j in NA_PATTERNS] == list(range(5))
    max_start = n_blocks - NA_KEY_BLOCKS

    def key_spec(t):
        return pl.BlockSpec((None, NA_Q_BLOCK, NA_W),
                            lambda i, j: (i, jnp.clip(j - 2, 0, max_start) + t, 0))

    ctx_spec = pl.BlockSpec((None,) + ck.shape[1:], lambda i, j: (i, 0, 0))
    key_specs = [key_spec(t) for t in range(NA_KEY_BLOCKS)]
    return pl.pallas_call(
        _na_kernel,
        grid=(b, n_blocks),
        in_specs=([pl.BlockSpec((None, NA_Q_BLOCK, NA_W), lambda i, j: (i, j, 0))]
                  + key_specs + key_specs + [ctx_spec, ctx_spec, _resident(table.shape)]),
        out_specs=pl.BlockSpec((None, NA_Q_BLOCK, NA_W), lambda i, j: (i, j, 0)),
        out_shape=jax.ShapeDtypeStruct((b, l, NA_W), BF16),
        compiler_params=_params("arbitrary", "arbitrary"),
        name="na_attention",
    )(nq, *([nk] * NA_KEY_BLOCKS), *([nv] * NA_KEY_BLOCKS), ck, cv, table)


def _tail_kernel(x_ref, mod_ref, yret_ref, yna_ref, gate_ref, n1_ref, n2_ref, n3_ref,
                 w_ret_ref, w_na_ref, w_o_ref, w_g_ref, w_u_ref, w_d_ref, o_ref):
    mod = mod_ref[0]
    g1 = mod[:, 2 * D_MODEL:3 * D_MODEL]
    sh2 = mod[:, 3 * D_MODEL:4 * D_MODEL]
    sc2 = mod[:, 4 * D_MODEL:5 * D_MODEL]
    g2 = mod[:, 5 * D_MODEL:6 * D_MODEL]
    g_ret = gate_ref[:, 0:D_MODEL].astype(F32)
    g_na = gate_ref[:, D_MODEL:2 * D_MODEL].astype(F32)
    branches = (g_ret * _dot(yret_ref[...], w_ret_ref[...])
                + g_na * _dot(yna_ref[...], w_na_ref[...]))
    mixed = _dot(branches.astype(BF16), w_o_ref[...])
    x = x_ref[...] + g1 * _rms(mixed, n1_ref[...])
    hf = (_rms(x, n2_ref[...]) * (1.0 + sc2) + sh2).astype(BF16)
    act = (_silu(_dot(hf, w_g_ref[...])) * _dot(hf, w_u_ref[...])).astype(BF16)
    o_ref[...] = x + g2 * _rms(_dot(act, w_d_ref[...]), n3_ref[...])


def _tail(x, mod, mod_row, y_ret, y_na, gates, n1, n2, n3, w_ret, w_na, w_o, w_g, w_u, w_d):
    b, l, _ = x.shape
    tm = TOKEN_TILE
    tok = lambda w: pl.BlockSpec((None, tm, w), lambda i, j: (i, j, 0))
    vec = lambda a: a.reshape(1, D_MODEL)
    return pl.pallas_call(
        _tail_kernel,
        grid=(b, l // tm),
        in_specs=[
            tok(D_MODEL),
            pl.BlockSpec((1, 1, 6 * D_MODEL), lambda i, j: (mod_row(i), 0, 0)),
            tok(RET_V_W), tok(NA_W), tok(2 * D_MODEL),
            _resident((1, D_MODEL)), _resident((1, D_MODEL)), _resident((1, D_MODEL)),
            _resident(w_ret.shape), _resident(w_na.shape), _resident(w_o.shape),
            _resident(w_g.shape), _resident(w_u.shape), _resident(w_d.shape),
        ],
        out_specs=tok(D_MODEL),
        out_shape=jax.ShapeDtypeStruct((b, l, D_MODEL), F32),
        compiler_params=_params("arbitrary", "arbitrary"),
        name="tail",
    )(x, mod, y_ret, y_na, gates, vec(n1), vec(n2), vec(n3), w_ret, w_na, w_o, w_g, w_u, w_d)


def _rope_tables(l):
    t = jnp.arange(l)
    row = (t // GRID_W).astype(F32)
    col = (t % GRID_W).astype(F32)
    n_freq = RET_DK // 4
    inv = ROPE_BASE ** (-jnp.arange(n_freq, dtype=F32) / n_freq)
    ang = jnp.concatenate([row[:, None] * inv, col[:, None] * inv], axis=-1)
    cos = jnp.cos(ang)
    sin = jnp.sin(ang)
    return jnp.concatenate([cos, cos], axis=-1), jnp.concatenate([-sin, sin], axis=-1)


def kernel(x_prompt, x_sample, cache_na_k, cache_na_v, state_ret, c, c_ctx, w_ada, b_ada,
           norm_pre_mix, norm_post_mix, norm_pre_ffn, norm_post_ffn, w_in, ret_decay_logit,
           ret_gn_gain, na_rpb, w_ret_out, w_na_out, w_gate, w_o, w_ffn_gate, w_ffn_up,
           w_ffn_down):
    dec_b, dec_l, _ = x_sample.shape
    past = cache_na_k.shape[2]
    cvec = jnp.zeros((MOD_ROWS, D_MODEL), F32).at[0].set(c_ctx).at[1:1 + dec_b].set(c)
    mods = _adaln(cvec, w_ada, b_ada).reshape(DEPTH, MOD_ROWS, 1, 6 * D_MODEL)
    log_g = jax.nn.log_sigmoid(ret_decay_logit.astype(F32))
    rope_tabs = _rope_tables(dec_l)
    bf = lambda a: a.astype(BF16)
    w_in_b, w_gate_b = bf(w_in), bf(w_gate)
    w_ret_b, w_na_b, w_o_b = bf(w_ret_out), bf(w_na_out), bf(w_o)
    w_g_b, w_u_b, w_d_b = bf(w_ffn_gate), bf(w_ffn_up), bf(w_ffn_down)
    ck_all = bf(cache_na_k).reshape(dec_b, DEPTH, past, NA_W)
    cv_all = bf(cache_na_v).reshape(dec_b, DEPTH, past, NA_W)
    ctx_row = lambda i: 0
    dec_row = lambda i: i + 1

    def tail(x, l, row, y_ret, y_na, gates):
        return _tail(x, mods[l], row, y_ret, y_na, gates, norm_post_mix[l], norm_pre_ffn[l],
                     norm_post_ffn[l], w_ret_b[l], w_na_b[l], w_o_b[l], w_g_b[l], w_u_b[l],
                     w_d_b[l])

    x = x_prompt
    k_list, v_list, s_list = [], [], []
    for l in range(DEPTH):
        rq, rk, rv, rg, nq, nk, nv, gates = _premix(
            x, mods[l], ctx_row, norm_pre_mix[l], w_in_b[l], w_gate_b[l], None, F32)
        y_ret, s_new = _retention(log_g[l], rq, rk, rv, rg, ret_gn_gain[l], None, l, True)
        y_na = _ctx_attention(nq, nk, nv)
        x = tail(x, l, ctx_row, y_ret, y_na, gates)
        k_list.append(nk.reshape(nk.shape[0], nk.shape[1], NA_HEADS, NA_DH))
        v_list.append(nv.reshape(nv.shape[0], nv.shape[1], NA_HEADS, NA_DH))
        s_list.append(s_new)
    y_prompt = x
    new_k = jnp.stack(k_list, axis=1)
    new_v = jnp.stack(v_list, axis=1)
    new_s = jnp.stack(s_list, axis=1)

    x = x_sample
    for l in range(DEPTH):
        rq, rk, rv, rg, nq, nk, nv, gates = _premix(
            x, mods[l], dec_row, norm_pre_mix[l], w_in_b[l], w_gate_b[l], rope_tabs, BF16)
        (y_ret,) = _retention(log_g[l], rq, rk, rv, rg, ret_gn_gain[l], state_ret, l, False)
        table = _na_bias_table(na_rpb[l], dec_l // GRID_W)
        y_na = _na_attention(nq, nk, nv, ck_all[:, l], cv_all[:, l], table)
        x = tail(x, l, dec_row, y_ret, y_na, gates)
    return (y_prompt, x, new_k, new_v, new_s)
```

```python
import functools

import numpy as np
import jax
import jax.numpy as jnp
from jax import lax
from jax.experimental import pallas as pl
from jax.experimental.pallas import tpu as pltpu

F32 = jnp.float32
BF16 = jnp.bfloat16

D_MODEL = 1024
DEPTH = 2
GRID_W = 64
RET_HEADS = 4
RET_DK = 128
RET_DV = 256
RET_QK_W = RET_HEADS * RET_DK
RET_V_W = RET_HEADS * RET_DV
RET_CHUNK = 128
NA_HEADS = 8
NA_DH = 64
NA_W = NA_HEADS * NA_DH
NA_WIN_H = 8
NA_WIN_W = 16
D_FF = 2816
ROPE_BASE = 10000.0
EPS = 1e-6
IN_WIDTH = 2 * RET_QK_W + 2 * RET_V_W + 3 * NA_W

LANES = 128
HEAD_PAIRS = NA_W // LANES
MOD_ROWS = 16
NA_Q_ROWS = 2
NA_Q_BLOCK = NA_Q_ROWS * GRID_W
NA_KEY_BLOCKS = 5
NA_KEY_ROWS = NA_KEY_BLOCKS * LANES // GRID_W
NA_PATTERNS = (0, 1, 2, 14, 15)
MASKED = -1e30
VMEM_LIMIT = 56 * 1024 * 1024
TOKEN_TILE = 256
RET_UNROLL = 4


def _sigmoid(x):
    return 1.0 / (1.0 + jnp.exp(-x))


def _silu(x):
    return x * _sigmoid(x)


def _rms(x, g):
    return x * lax.rsqrt(jnp.mean(x * x, axis=-1, keepdims=True) + EPS) * g


def _dot(a, b):
    return jnp.dot(a, b, preferred_element_type=F32)


def _dot_nt(a, b):
    return lax.dot_general(a, b, (((1,), (1,)), ((), ())), preferred_element_type=F32)


def _resident(shape):
    zeros = (0,) * len(shape)
    return pl.BlockSpec(shape, lambda *_: zeros, pipeline_mode=pl.Buffered(1))


def _params(*sem):
    return pltpu.CompilerParams(dimension_semantics=sem, vmem_limit_bytes=VMEM_LIMIT)


def _adaln_kernel(c_ref, w_ref, b_ref, o_ref):
    s = _silu(c_ref[...]).astype(BF16)
    o_ref[0] = _dot(s, w_ref[0].astype(BF16)) + b_ref[0]


def _adaln(cvec, w_ada, b_ada):
    tn = 1536
    n = 6 * D_MODEL
    return pl.pallas_call(
        _adaln_kernel,
        grid=(DEPTH, n // tn),
        in_specs=[
            pl.BlockSpec((MOD_ROWS, D_MODEL), lambda l, j: (0, 0)),
            pl.BlockSpec((1, D_MODEL, tn), lambda l, j: (l, 0, j)),
            pl.BlockSpec((1, 1, tn), lambda l, j: (l, 0, j)),
        ],
        out_specs=pl.BlockSpec((1, MOD_ROWS, tn), lambda l, j: (l, 0, j)),
        out_shape=jax.ShapeDtypeStruct((DEPTH, MOD_ROWS, n), F32),
        compiler_params=_params("arbitrary", "arbitrary"),
        name="adaln",
    )(cvec, w_ada, b_ada.reshape(DEPTH, 1, n))


def _premix_kernel(*refs, rope):
    if rope:
        (x_ref, mod_ref, g_ref, w_in_ref, w_gate_ref, cos_ref, sin_ref,
         rq_ref, rk_ref, rv_ref, rg_ref, nq_ref, nk_ref, nv_ref, gate_ref) = refs
    else:
        (x_ref, mod_ref, g_ref, w_in_ref, w_gate_ref,
         rq_ref, rk_ref, rv_ref, rg_ref, nq_ref, nk_ref, nv_ref, gate_ref) = refs
    mod = mod_ref[0]
    sh1 = mod[:, 0:D_MODEL]
    sc1 = mod[:, D_MODEL:2 * D_MODEL]
    hm = (_rms(x_ref[...], g_ref[...]) * (1.0 + sc1) + sh1).astype(BF16)

    def proj(lo, width):
        return _dot(hm, w_in_ref[:, lo:lo + width])

    def rotary(t):
        if not rope:
            return t
        cos = cos_ref[...]
        sin = sin_ref[...]
        heads = []
        for h in range(RET_HEADS):
            blk = t[:, h * RET_DK:(h + 1) * RET_DK]
            heads.append(blk * cos + pltpu.roll(blk, RET_DK // 2, axis=1) * sin)
        return jnp.concatenate(heads, axis=1)

    lo = 0
    rq_ref[...] = rotary(proj(lo, RET_QK_W)).astype(rq_ref.dtype)
    lo += RET_QK_W
    rk_ref[...] = (rotary(proj(lo, RET_QK_W)) * (RET_DK ** -0.5)).astype(rk_ref.dtype)
    lo += RET_QK_W
    rv_ref[...] = proj(lo, RET_V_W).astype(rv_ref.dtype)
    lo += RET_V_W
    rg_ref[...] = proj(lo, RET_V_W).astype(rg_ref.dtype)
    lo += RET_V_W
    nq_ref[...] = (proj(lo, NA_W) * (NA_DH ** -0.5)).astype(nq_ref.dtype)
    lo += NA_W
    nk_ref[...] = proj(lo, NA_W).astype(nk_ref.dtype)
    lo += NA_W
    nv_ref[...] = proj(lo, NA_W).astype(nv_ref.dtype)
    gate_ref[...] = _sigmoid(_dot(hm, w_gate_ref[...])).astype(gate_ref.dtype)


def _premix(x, mod, mod_row, g, w_in, w_gate, rope_tabs, kv_dtype):
    b, l, _ = x.shape
    tm = TOKEN_TILE
    tiles = l // tm
    tok = lambda w: pl.BlockSpec((None, tm, w), lambda i, j: (i, j, 0))
    in_specs = [
        tok(D_MODEL),
        pl.BlockSpec((1, 1, 6 * D_MODEL), lambda i, j: (mod_row(i), 0, 0)),
        _resident((1, D_MODEL)),
        _resident((D_MODEL, IN_WIDTH)),
        _resident((D_MODEL, 2 * D_MODEL)),
    ]
    args = [x, mod, g.reshape(1, D_MODEL), w_in, w_gate]
    if rope_tabs is not None:
        in_specs += [pl.BlockSpec((tm, RET_DK), lambda i, j: (j, 0))] * 2
        args += list(rope_tabs)
    widths = (RET_QK_W, RET_QK_W, RET_V_W, RET_V_W, NA_W, NA_W, NA_W, 2 * D_MODEL)
    dtypes = (BF16, BF16, BF16, BF16, BF16, kv_dtype, kv_dtype, BF16)
    return pl.pallas_call(
        functools.partial(_premix_kernel, rope=rope_tabs is not None),
        grid=(b, tiles),
        in_specs=in_specs,
        out_specs=[tok(w) for w in widths],
        out_shape=[jax.ShapeDtypeStruct((b, l, w), d) for w, d in zip(widths, dtypes)],
        compiler_params=_params("arbitrary", "arbitrary"),
        name="premix",
    )(*args)


def _retention_kernel(*refs, n_chunks, has_s0, write_state):
    refs = list(refs)
    lg_ref, q_ref, k_ref, v_ref, rg_ref, gain_ref = refs[:6]
    pos = 6
    s0_ref = None
    if has_s0:
        s0_ref = refs[pos]
        pos += 1
    y_ref = refs[pos]
    pos += 1
    sout_ref = None
    if write_state:
        sout_ref = refs[pos]
        pos += 1
    s_scr = refs[pos]

    c = RET_CHUNK
    h = pl.program_id(1)
    lgf = lg_ref[0, h]
    lgb = lg_ref[1, h]
    row = lax.broadcasted_iota(jnp.int32, (c, c), 0).astype(F32)
    col = lax.broadcasted_iota(jnp.int32, (c, c), 1).astype(F32)
    rel = row - col
    decay = (jnp.where(rel >= 0, jnp.exp(lgf * jnp.maximum(rel, 0.0)), 0.0)
             + jnp.where(rel <= 0, jnp.exp(lgb * jnp.maximum(-rel, 0.0)), 0.0))
    p = lax.broadcasted_iota(jnp.int32, (c, 1), 0).astype(F32)
    xi_f = jnp.exp(lgf * (p + 1.0))
    xi_b = jnp.exp(lgb * (c - p))
    zeta_f = jnp.exp(lgf * (c - 1.0 - p))
    zeta_b = jnp.exp(lgb * p)
    chunk_len = jnp.full((1, RET_DV), float(c), F32)
    g_f = jnp.exp(lgf * chunk_len)
    g_b = jnp.exp(lgb * chunk_len)

    def chunk(ref, i):
        return ref[pl.ds(pl.multiple_of(i * c, c), c), :]

    def outer_kv(i, zeta):
        kz = (chunk(k_ref, i).astype(F32) * zeta).T.astype(BF16)
        return _dot(kz, chunk(v_ref, i))

    if has_s0:
        s_f0 = s0_ref[0]
        s_b0 = s0_ref[1]
    else:
        s_f0 = jnp.zeros((RET_DK, RET_DV), F32)
        s_b0 = s_f0

    def scan(t, carry):
        s_f, s_b = carry
        i_b = n_chunks - 1 - t
        s_scr[t, 0:RET_DK, :] = s_f.astype(BF16)
        s_scr[i_b, RET_DK:2 * RET_DK, :] = s_b.astype(BF16)
        return (g_f * s_f + outer_kv(t, zeta_f), g_b * s_b + outer_kv(i_b, zeta_b))

    unroll = min(n_chunks, RET_UNROLL)
    s_f, s_b = lax.fori_loop(0, n_chunks, scan, (s_f0, s_b0), unroll=unroll)
    if write_state:
        sout_ref[0] = s_f
        sout_ref[1] = s_b

    gain = gain_ref[...]

    def emit(i, _):
        qi = chunk(q_ref, i)
        scores = (_dot_nt(qi, chunk(k_ref, i)) * decay).astype(BF16)
        qf = qi.astype(F32)
        qx = jnp.concatenate([(qf * xi_f).astype(BF16), (qf * xi_b).astype(BF16)], axis=1)
        y = _dot(scores, chunk(v_ref, i)) + _dot(qx, s_scr[i])
        mu = jnp.mean(y, axis=-1, keepdims=True)
        d = y - mu
        var = jnp.mean(d * d, axis=-1, keepdims=True)
        yn = d * lax.rsqrt(var + EPS) * gain
        out = _silu(chunk(rg_ref, i).astype(F32)) * yn
        y_ref[pl.ds(pl.multiple_of(i * c, c), c), :] = out.astype(y_ref.dtype)
        return 0

    lax.fori_loop(0, n_chunks, emit, 0, unroll=unroll)


def _retention(log_g, rq, rk, rv, rg, gain, state, layer, write_state):
    b, l, _ = rq.shape
    n_chunks = l // RET_CHUNK
    in_specs = [
        pl.BlockSpec(memory_space=pltpu.SMEM),
        pl.BlockSpec((None, l, RET_DK), lambda i, h: (i, 0, h)),
        pl.BlockSpec((None, l, RET_DK), lambda i, h: (i, 0, h)),
        pl.BlockSpec((None, l, RET_DV), lambda i, h: (i, 0, h)),
        pl.BlockSpec((None, l, RET_DV), lambda i, h: (i, 0, h)),
        pl.BlockSpec((1, RET_DV), lambda i, h: (0, h)),
    ]
    args = [log_g, rq, rk, rv, rg, gain.reshape(1, RET_V_W)]
    if state is not None:
        in_specs.append(pl.BlockSpec((None, None, 2, None, RET_DK, RET_DV),
                                     lambda i, h: (i, layer, 0, h, 0, 0)))
        args.append(state)
    out_specs = [pl.BlockSpec((None, l, RET_DV), lambda i, h: (i, 0, h))]
    out_shape = [jax.ShapeDtypeStruct((b, l, RET_V_W), BF16)]
    if write_state:
        out_specs.append(pl.BlockSpec((None, 2, None, RET_DK, RET_DV),
                                      lambda i, h: (i, 0, h, 0, 0)))
        out_shape.append(jax.ShapeDtypeStruct((b, 2, RET_HEADS, RET_DK, RET_DV), F32))
    return pl.pallas_call(
        functools.partial(_retention_kernel, n_chunks=n_chunks, has_s0=state is not None,
                          write_state=write_state),
        grid=(b, RET_HEADS),
        in_specs=in_specs,
        out_specs=out_specs,
        out_shape=out_shape,
        scratch_shapes=[pltpu.VMEM((n_chunks, 2 * RET_DK, RET_DV), BF16)],
        compiler_params=_params("arbitrary", "arbitrary"),
        name="retention",
    )(*args)


def _fold_lanes(x, op):
    return functools.reduce(op, [x[:, c:c + LANES] for c in range(0, x.shape[1], LANES)])


def _pair_attention(q2, keys, values, biases):
    n_q = q2.shape[0]
    lane = lax.broadcasted_iota(jnp.int32, (1, LANES), 1)
    lo_mask = lane < NA_DH
    zero = jnp.zeros_like(q2)
    qq = jnp.concatenate([jnp.where(lo_mask, q2, zero), jnp.where(lo_mask, zero, q2)], axis=0)
    scores = []
    for kt, bt in zip(keys, biases):
        s = _dot_nt(qq, kt)
        scores.append(s if bt is None else s + bt)
    m = functools.reduce(jnp.maximum, [_fold_lanes(s, jnp.maximum) for s in scores])
    m = m.max(axis=-1, keepdims=True)
    denom = None
    acc = None
    for s, v in zip(scores, values):
        e = jnp.exp(s - m)
        part = _fold_lanes(e, jnp.add)
        pv = _dot(e.astype(BF16), v)
        denom = part if denom is None else denom + part
        acc = pv if acc is None else acc + pv
    out = acc * (1.0 / denom.sum(axis=-1, keepdims=True))
    return jnp.where(lo_mask, out[:n_q], out[n_q:])


def _ctx_attn_kernel(q_ref, k_ref, v_ref, o_ref):
    for hp in range(HEAD_PAIRS):
        sl = slice(hp * LANES, (hp + 1) * LANES)
        out = _pair_attention(q_ref[:, sl], [k_ref[:, sl].astype(BF16)],
                              [v_ref[:, sl].astype(BF16)], [None])
        o_ref[:, sl] = out.astype(o_ref.dtype)


def _ctx_attention(nq, nk, nv):
    b, l, _ = nq.shape
    spec = pl.BlockSpec((None, l, NA_W), lambda i: (i, 0, 0))
    return pl.pallas_call(
        _ctx_attn_kernel,
        grid=(b,),
        in_specs=[spec, spec, spec],
        out_specs=spec,
        out_shape=jax.ShapeDtypeStruct((b, l, NA_W), BF16),
        compiler_params=_params("arbitrary"),
        name="ctx_attention",
    )(nq, nk, nv)


def _na_window_start(j, rows):
    return int(np.clip(NA_Q_ROWS * j - NA_WIN_H // 2, 0, rows - NA_KEY_ROWS))


def _na_bias_table(rpb, rows):
    heads, n_dr, n_dc = rpb.shape
    lead = GRID_W - NA_WIN_W
    padded = jnp.pad(rpb.astype(F32), ((0, 0), (0, 0), (lead, 2 * GRID_W - lead - n_dc)))
    skew = jnp.tile(padded, (1, 1, GRID_W))[..., :GRID_W * (2 * GRID_W - 1)]
    toeplitz = skew.reshape(heads, n_dr, GRID_W, 2 * GRID_W - 1)[..., GRID_W - 1:]
    qc = np.arange(GRID_W)[:, None]
    kc = np.arange(GRID_W)[None, :]
    cs = np.clip(qc - NA_WIN_W // 2, 0, GRID_W - NA_WIN_W)
    col_valid = (kc >= cs) & (kc < cs + NA_WIN_W)
    blocks = jnp.where(col_valid, toeplitz, MASKED)
    blocks = jnp.concatenate([blocks, jnp.full((heads, 1, GRID_W, GRID_W), MASKED, F32)], axis=1)
    a = np.arange(NA_Q_ROWS)[:, None]
    kk = np.arange(NA_KEY_ROWS)[None, :]
    index = []
    for j in NA_PATTERNS:
        qr = NA_Q_ROWS * j + a
        kr = _na_window_start(j, rows) + kk
        rs = np.clip(qr - NA_WIN_H // 2, 0, rows - NA_WIN_H)
        row_valid = (kr >= rs) & (kr < rs + NA_WIN_H)
        index.append(np.where(row_valid, kr - qr + NA_WIN_H - 1, n_dr))
    table = blocks[:, np.stack(index)]
    table = table.transpose(1, 0, 2, 4, 3, 5)
    return table.reshape(len(NA_PATTERNS), heads, NA_Q_BLOCK, NA_KEY_ROWS * GRID_W)


def _na_kernel(q_ref, k_ref, v_ref, ck_ref, cv_ref, tab_ref, o_ref):
    j = pl.program_id(1)
    n_blocks = pl.num_programs(1)
    pattern = jnp.minimum(j, 2) + jnp.maximum(j - (n_blocks - 3), 0)
    start = jnp.clip(j - 2, 0, n_blocks - NA_KEY_BLOCKS) * NA_Q_BLOCK
    window = pl.ds(pl.multiple_of(start, NA_Q_BLOCK), NA_KEY_BLOCKS * LANES)
    for hp in range(HEAD_PAIRS):
        sl = slice(hp * LANES, (hp + 1) * LANES)
        bias = tab_ref[pattern, 2 * hp:2 * hp + 2].reshape(2 * NA_Q_BLOCK, -1)
        out = _pair_attention(q_ref[:, sl], [k_ref[window, sl], ck_ref[:, sl]],
                              [v_ref[window, sl], cv_ref[:, sl]], [bias, None])
        o_ref[:, sl] = out.astype(o_ref.dtype)


def _na_attention(nq, nk, nv, ck, cv, table):
    b, l, _ = nq.shape
    n_blocks = l // NA_Q_BLOCK
    assert [min(j, 2) + max(j - (n_blocks - 3), 0) for j in NA_PATTERNS] == list(range(5))
    tile = pl.BlockSpec((None, NA_Q_BLOCK, NA_W), lambda i, j: (i, j, 0))
    whole = lambda a: pl.BlockSpec((None,) + a.shape[1:], lambda i, j: (i, 0, 0))
    return pl.pallas_call(
        _na_kernel,
        grid=(b, n_blocks),
        in_specs=[tile, whole(nk), whole(nv), whole(ck), whole(cv), _resident(table.shape)],
        out_specs=tile,
        out_shape=jax.ShapeDtypeStruct((b, l, NA_W), BF16),
        compiler_params=_params("arbitrary", "arbitrary"),
        name="na_attention",
    )(nq, nk, nv, ck, cv, table)


def _tail_kernel(x_ref, mod_ref, yret_ref, yna_ref, gate_ref, n1_ref, n2_ref, n3_ref,
                 w_ret_ref, w_na_ref, w_o_ref, w_g_ref, w_u_ref, w_d_ref, o_ref):
    mod = mod_ref[0]
    g1 = mod[:, 2 * D_MODEL:3 * D_MODEL]
    sh2 = mod[:, 3 * D_MODEL:4 * D_MODEL]
    sc2 = mod[:, 4 * D_MODEL:5 * D_MODEL]
    g2 = mod[:, 5 * D_MODEL:6 * D_MODEL]
    g_ret = gate_ref[:, 0:D_MODEL].astype(F32)
    g_na = gate_ref[:, D_MODEL:2 * D_MODEL].astype(F32)
    branches = (g_ret * _dot(yret_ref[...], w_ret_ref[...])
                + g_na * _dot(yna_ref[...], w_na_ref[...]))
    mixed = _dot(branches.astype(BF16), w_o_ref[...])
    x = x_ref[...] + g1 * _rms(mixed, n1_ref[...])
    hf = (_rms(x, n2_ref[...]) * (1.0 + sc2) + sh2).astype(BF16)
    act = (_silu(_dot(hf, w_g_ref[...])) * _dot(hf, w_u_ref[...])).astype(BF16)
    o_ref[...] = x + g2 * _rms(_dot(act, w_d_ref[...]), n3_ref[...])


def _tail(x, mod, mod_row, y_ret, y_na, gates, n1, n2, n3, w_ret, w_na, w_o, w_g, w_u, w_d):
    b, l, _ = x.shape
    tm = TOKEN_TILE
    tok = lambda w: pl.BlockSpec((None, tm, w), lambda i, j: (i, j, 0))
    vec = lambda a: a.reshape(1, D_MODEL)
    return pl.pallas_call(
        _tail_kernel,
        grid=(b, l // tm),
        in_specs=[
            tok(D_MODEL),
            pl.BlockSpec((1, 1, 6 * D_MODEL), lambda i, j: (mod_row(i), 0, 0)),
            tok(RET_V_W), tok(NA_W), tok(2 * D_MODEL),
            _resident((1, D_MODEL)), _resident((1, D_MODEL)), _resident((1, D_MODEL)),
            _resident(w_ret.shape), _resident(w_na.shape), _resident(w_o.shape),
            _resident(w_g.shape), _resident(w_u.shape), _resident(w_d.shape),
        ],
        out_specs=tok(D_MODEL),
        out_shape=jax.ShapeDtypeStruct((b, l, D_MODEL), F32),
        compiler_params=_params("arbitrary", "arbitrary"),
        name="tail",
    )(x, mod, y_ret, y_na, gates, vec(n1), vec(n2), vec(n3), w_ret, w_na, w_o, w_g, w_u, w_d)


def _rope_tables(l):
    t = jnp.arange(l)
    row = (t // GRID_W).astype(F32)
    col = (t % GRID_W).astype(F32)
    n_freq = RET_DK // 4
    inv = ROPE_BASE ** (-jnp.arange(n_freq, dtype=F32) / n_freq)
    ang = jnp.concatenate([row[:, None] * inv, col[:, None] * inv], axis=-1)
    cos = jnp.cos(ang)
    sin = jnp.sin(ang)
    return jnp.concatenate([cos, cos], axis=-1), jnp.concatenate([-sin, sin], axis=-1)


def kernel(x_prompt, x_sample, cache_na_k, cache_na_v, state_ret, c, c_ctx, w_ada, b_ada,
           norm_pre_mix, norm_post_mix, norm_pre_ffn, norm_post_ffn, w_in, ret_decay_logit,
           ret_gn_gain, na_rpb, w_ret_out, w_na_out, w_gate, w_o, w_ffn_gate, w_ffn_up,
           w_ffn_down):
    dec_b, dec_l, _ = x_sample.shape
    past = cache_na_k.shape[2]
    cvec = jnp.zeros((MOD_ROWS, D_MODEL), F32).at[0].set(c_ctx).at[1:1 + dec_b].set(c)
    mods = _adaln(cvec, w_ada, b_ada).reshape(DEPTH, MOD_ROWS, 1, 6 * D_MODEL)
    log_g = jax.nn.log_sigmoid(ret_decay_logit.astype(F32))
    rope_tabs = _rope_tables(dec_l)
    bf = lambda a: a.astype(BF16)
    w_in_b, w_gate_b = bf(w_in), bf(w_gate)
    w_ret_b, w_na_b, w_o_b = bf(w_ret_out), bf(w_na_out), bf(w_o)
    w_g_b, w_u_b, w_d_b = bf(w_ffn_gate), bf(w_ffn_up), bf(w_ffn_down)
    ck_all = bf(cache_na_k).reshape(dec_b, DEPTH, past, NA_W)
    cv_all = bf(cache_na_v).reshape(dec_b, DEPTH, past, NA_W)
    ctx_row = lambda i: 0
    dec_row = lambda i: i + 1

    def tail(x, l, row, y_ret, y_na, gates):
        return _tail(x, mods[l], row, y_ret, y_na, gates, norm_post_mix[l], norm_pre_ffn[l],
                     norm_post_ffn[l], w_ret_b[l], w_na_b[l], w_o_b[l], w_g_b[l], w_u_b[l],
                     w_d_b[l])

    x = x_prompt
    k_list, v_list, s_list = [], [], []
    for l in range(DEPTH):
        rq, rk, rv, rg, nq, nk, nv, gates = _premix(
            x, mods[l], ctx_row, norm_pre_mix[l], w_in_b[l], w_gate_b[l], None, F32)
        y_ret, s_new = _retention(log_g[l], rq, rk, rv, rg, ret_gn_gain[l], None, l, True)
        y_na = _ctx_attention(nq, nk, nv)
        x = tail(x, l, ctx_row, y_ret, y_na, gates)
        k_list.append(nk.reshape(nk.shape[0], nk.shape[1], NA_HEADS, NA_DH))
        v_list.append(nv.reshape(nv.shape[0], nv.shape[1], NA_HEADS, NA_DH))
        s_list.append(s_new)
    y_prompt = x
    new_k = jnp.stack(k_list, axis=1)
    new_v = jnp.stack(v_list, axis=1)
    new_s = jnp.stack(s_list, axis=1)

    x = x_sample
    for l in range(DEPTH):
        rq, rk, rv, rg, nq, nk, nv, gates = _premix(
            x, mods[l], dec_row, norm_pre_mix[l], w_in_b[l], w_gate_b[l], rope_tabs, BF16)
        (y_ret,) = _retention(log_g[l], rq, rk, rv, rg, ret_gn_gain[l], state_ret, l, False)
        table = _na_bias_table(na_rpb[l], dec_l // GRID_W)
        y_na = _na_attention(nq, nk, nv, ck_all[:, l], cv_all[:, l], table)
        x = tail(x, l, dec_row, y_ret, y_na, gates)
    return (y_prompt, x, new_k, new_v, new_s)
```

```python
import functools

import numpy as np
import jax
import jax.numpy as jnp
from jax import lax
from jax.experimental import pallas as pl
from jax.experimental.pallas import tpu as pltpu

F32 = jnp.float32
BF16 = jnp.bfloat16

D_MODEL = 1024
DEPTH = 2
GRID_W = 64
RET_HEADS = 4
RET_DK = 128
RET_DV = 256
RET_QK_W = RET_HEADS * RET_DK
RET_V_W = RET_HEADS * RET_DV
RET_CHUNK = 128
NA_HEADS = 8
NA_DH = 64
NA_W = NA_HEADS * NA_DH
NA_WIN_H = 8
NA_WIN_W = 16
D_FF = 2816
ROPE_BASE = 10000.0
EPS = 1e-6
IN_WIDTH = 2 * RET_QK_W + 2 * RET_V_W + 3 * NA_W

LANES = 128
HEAD_PAIRS = NA_W // LANES
MOD_ROWS = 16
NA_Q_ROWS = 2
NA_Q_BLOCK = NA_Q_ROWS * GRID_W
NA_KEY_BLOCKS = 5
NA_KEY_ROWS = NA_KEY_BLOCKS * LANES // GRID_W
NA_PATTERNS = (0, 1, 2, 14, 15)
MASKED = -1e30
VMEM_LIMIT = 56 * 1024 * 1024
TOKEN_TILE = 512
RET_UNROLL = 8


def _sigmoid(x):
    return 1.0 / (1.0 + jnp.exp(-x))


def _silu(x):
    return x * _sigmoid(x)


def _rms(x, g):
    return x * lax.rsqrt(jnp.mean(x * x, axis=-1, keepdims=True) + EPS) * g


def _dot(a, b):
    return jnp.dot(a, b, preferred_element_type=F32)


def _dot_nt(a, b):
    return lax.dot_general(a, b, (((1,), (1,)), ((), ())), preferred_element_type=F32)


def _resident(shape):
    zeros = (0,) * len(shape)
    return pl.BlockSpec(shape, lambda *_: zeros, pipeline_mode=pl.Buffered(1))


def _params(*sem):
    return pltpu.CompilerParams(dimension_semantics=sem, vmem_limit_bytes=VMEM_LIMIT)


def _adaln_kernel(c_ref, w_ref, b_ref, o_ref):
    s = _silu(c_ref[...]).astype(BF16)
    o_ref[0] = _dot(s, w_ref[0].astype(BF16)) + b_ref[0]


def _adaln(cvec, w_ada, b_ada):
    tn = 1536
    n = 6 * D_MODEL
    return pl.pallas_call(
        _adaln_kernel,
        grid=(DEPTH, n // tn),
        in_specs=[
            pl.BlockSpec((MOD_ROWS, D_MODEL), lambda l, j: (0, 0)),
            pl.BlockSpec((1, D_MODEL, tn), lambda l, j: (l, 0, j)),
            pl.BlockSpec((1, 1, tn), lambda l, j: (l, 0, j)),
        ],
        out_specs=pl.BlockSpec((1, MOD_ROWS, tn), lambda l, j: (l, 0, j)),
        out_shape=jax.ShapeDtypeStruct((DEPTH, MOD_ROWS, n), F32),
        compiler_params=_params("arbitrary", "arbitrary"),
        name="adaln",
    )(cvec, w_ada, b_ada.reshape(DEPTH, 1, n))


def _premix_kernel(*refs, rope, cache_layer):
    refs = list(refs)
    x_ref, mod_ref, g_ref, w_in_ref, w_gate_ref = refs[:5]
    pos = 5
    if rope:
        cos_ref, sin_ref = refs[pos:pos + 2]
        pos += 2
    if cache_layer:
        k_prev_ref, v_prev_ref = refs[pos:pos + 2]
        pos += 2
    rq_ref, rk_ref, rv_ref, rg_ref, nq_ref, nk_ref, nv_ref, gate_ref = refs[pos:]
    mod = mod_ref[0]
    sh1 = mod[:, 0:D_MODEL]
    sc1 = mod[:, D_MODEL:2 * D_MODEL]
    hm = (_rms(x_ref[...], g_ref[...]) * (1.0 + sc1) + sh1).astype(BF16)

    def proj(lo, width):
        return _dot(hm, w_in_ref[:, lo:lo + width])

    def rotary(t):
        if not rope:
            return t
        cos = cos_ref[...]
        sin = sin_ref[...]
        heads = []
        for h in range(RET_HEADS):
            blk = t[:, h * RET_DK:(h + 1) * RET_DK]
            heads.append(blk * cos + pltpu.roll(blk, RET_DK // 2, axis=1) * sin)
        return jnp.concatenate(heads, axis=1)

    def store_kv(ref, prev_ref, t):
        if cache_layer is None:
            ref[...] = t.astype(ref.dtype)
            return
        if cache_layer:
            ref[:, 0:cache_layer] = prev_ref[...]
        ref[:, cache_layer] = t.reshape(ref.shape[0], ref.shape[2], ref.shape[3])

    lo = 0
    rq_ref[...] = rotary(proj(lo, RET_QK_W)).astype(rq_ref.dtype)
    lo += RET_QK_W
    rk_ref[...] = (rotary(proj(lo, RET_QK_W)) * (RET_DK ** -0.5)).astype(rk_ref.dtype)
    lo += RET_QK_W
    rv_ref[...] = proj(lo, RET_V_W).astype(rv_ref.dtype)
    lo += RET_V_W
    rg_ref[...] = proj(lo, RET_V_W).astype(rg_ref.dtype)
    lo += RET_V_W
    nq_ref[...] = (proj(lo, NA_W) * (NA_DH ** -0.5)).astype(nq_ref.dtype)
    lo += NA_W
    store_kv(nk_ref, k_prev_ref if cache_layer else None, proj(lo, NA_W))
    lo += NA_W
    store_kv(nv_ref, v_prev_ref if cache_layer else None, proj(lo, NA_W))
    gate_ref[...] = _sigmoid(_dot(hm, w_gate_ref[...])).astype(gate_ref.dtype)


def _premix(x, mod, mod_row, g, w_in, w_gate, rope_tabs=None, cache=None):
    b, l, _ = x.shape
    tm = min(TOKEN_TILE, l)
    tok = lambda w: pl.BlockSpec((None, tm, w), lambda i, j: (i, j, 0))
    in_specs = [
        tok(D_MODEL),
        pl.BlockSpec((1, 1, 6 * D_MODEL), lambda i, j: (mod_row(i), 0, 0)),
        _resident((1, D_MODEL)),
        _resident((D_MODEL, IN_WIDTH)),
        _resident((D_MODEL, 2 * D_MODEL)),
    ]
    args = [x, mod, g.reshape(1, D_MODEL), w_in, w_gate]
    if rope_tabs is not None:
        in_specs += [pl.BlockSpec((tm, RET_DK), lambda i, j: (j, 0))] * 2
        args += list(rope_tabs)
    widths = (RET_QK_W, RET_QK_W, RET_V_W, RET_V_W, NA_W, NA_W, NA_W, 2 * D_MODEL)
    out_specs = [tok(w) for w in widths]
    out_shape = [jax.ShapeDtypeStruct((b, l, w), BF16) for w in widths]
    cache_layer = None
    if cache is not None:
        cache_layer, seq, k_prev, v_prev = cache
        assert b == 1 and tm % seq == 0
        per_tile = tm // seq
        kv_spec = lambda n: pl.BlockSpec((per_tile, n, seq, NA_W), lambda i, j: (j, 0, 0, 0))
        if cache_layer:
            in_specs += [kv_spec(cache_layer)] * 2
            args += [k_prev, v_prev]
        kv_shape = jax.ShapeDtypeStruct((l // seq, cache_layer + 1, seq, NA_W), F32)
        out_specs[5:7] = [kv_spec(cache_layer + 1)] * 2
        out_shape[5:7] = [kv_shape] * 2
    return pl.pallas_call(
        functools.partial(_premix_kernel, rope=rope_tabs is not None, cache_layer=cache_layer),
        grid=(b, l // tm),
        in_specs=in_specs,
        out_specs=out_specs,
        out_shape=out_shape,
        compiler_params=_params("arbitrary", "arbitrary"),
        name="premix",
    )(*args)


def _retention_kernel(*refs, n_chunks, has_s0, state_layer):
    refs = list(refs)
    lg_ref, q_ref, k_ref, v_ref, rg_ref, gain_ref = refs[:6]
    pos = 6
    if has_s0:
        s0_ref = refs[pos]
        pos += 1
    if state_layer:
        sprev_ref = refs[pos]
        pos += 1
    y_ref = refs[pos]
    pos += 1
    if state_layer is not None:
        sout_ref = refs[pos]
        pos += 1
    s_scr = refs[pos]

    c = RET_CHUNK
    h = pl.program_id(1)
    lgf = lg_ref[0, h]
    lgb = lg_ref[1, h]
    row = lax.broadcasted_iota(jnp.int32, (c, c), 0).astype(F32)
    col = lax.broadcasted_iota(jnp.int32, (c, c), 1).astype(F32)
    rel = row - col
    decay = (jnp.where(rel >= 0, jnp.exp(lgf * jnp.maximum(rel, 0.0)), 0.0)
             + jnp.where(rel <= 0, jnp.exp(lgb * jnp.maximum(-rel, 0.0)), 0.0))
    p = lax.broadcasted_iota(jnp.int32, (c, 1), 0).astype(F32)
    xi_f = jnp.exp(lgf * (p + 1.0))
    xi_b = jnp.exp(lgb * (c - p))
    zeta_f = jnp.exp(lgf * (c - 1.0 - p))
    zeta_b = jnp.exp(lgb * p)
    chunk_len = jnp.full((1, RET_DV), float(c), F32)
    g_f = jnp.exp(lgf * chunk_len)
    g_b = jnp.exp(lgb * chunk_len)

    def chunk(ref, i):
        return ref[pl.ds(pl.multiple_of(i * c, c), c), :]

    def outer_kv(i, zeta):
        kz = (chunk(k_ref, i).astype(F32) * zeta).T.astype(BF16)
        return _dot(kz, chunk(v_ref, i))

    if has_s0:
        s_f0 = s0_ref[0]
        s_b0 = s0_ref[1]
    else:
        s_f0 = jnp.zeros((RET_DK, RET_DV), F32)
        s_b0 = s_f0

    def scan(t, carry):
        s_f, s_b = carry
        i_b = n_chunks - 1 - t
        s_scr[t, 0:RET_DK, :] = s_f.astype(BF16)
        s_scr[i_b, RET_DK:2 * RET_DK, :] = s_b.astype(BF16)
        return (g_f * s_f + outer_kv(t, zeta_f), g_b * s_b + outer_kv(i_b, zeta_b))

    unroll = min(n_chunks, RET_UNROLL)
    s_f, s_b = lax.fori_loop(0, n_chunks, scan, (s_f0, s_b0), unroll=unroll)
    if state_layer is not None:
        if state_layer:
            sout_ref[0:state_layer] = sprev_ref[...]
        sout_ref[state_layer, 0] = s_f
        sout_ref[state_layer, 1] = s_b

    gain = gain_ref[...]

    def emit(i, _):
        qi = chunk(q_ref, i)
        scores = (_dot_nt(qi, chunk(k_ref, i)) * decay).astype(BF16)
        qf = qi.astype(F32)
        qx = jnp.concatenate([(qf * xi_f).astype(BF16), (qf * xi_b).astype(BF16)], axis=1)
        y = _dot(scores, chunk(v_ref, i)) + _dot(qx, s_scr[i])
        mu = jnp.mean(y, axis=-1, keepdims=True)
        d = y - mu
        var = jnp.mean(d * d, axis=-1, keepdims=True)
        yn = d * lax.rsqrt(var + EPS) * gain
        out = _silu(chunk(rg_ref, i).astype(F32)) * yn
        y_ref[pl.ds(pl.multiple_of(i * c, c), c), :] = out.astype(y_ref.dtype)
        return 0

    lax.fori_loop(0, n_chunks, emit, 0, unroll=unroll)


def _retention(log_g, rq, rk, rv, rg, gain, state=None, layer=0, new_state=None):
    b, l, _ = rq.shape
    n_chunks = l // RET_CHUNK
    in_specs = [
        pl.BlockSpec(memory_space=pltpu.SMEM),
        pl.BlockSpec((None, l, RET_DK), lambda i, h: (i, 0, h)),
        pl.BlockSpec((None, l, RET_DK), lambda i, h: (i, 0, h)),
        pl.BlockSpec((None, l, RET_DV), lambda i, h: (i, 0, h)),
        pl.BlockSpec((None, l, RET_DV), lambda i, h: (i, 0, h)),
        pl.BlockSpec((1, RET_DV), lambda i, h: (0, h)),
    ]
    args = [log_g, rq, rk, rv, rg, gain.reshape(1, RET_V_W)]
    if state is not None:
        in_specs.append(pl.BlockSpec((None, None, 2, None, RET_DK, RET_DV),
                                     lambda i, h: (i, layer, 0, h, 0, 0)))
        args.append(state)
    out_specs = [pl.BlockSpec((None, l, RET_DV), lambda i, h: (i, 0, h))]
    out_shape = [jax.ShapeDtypeStruct((b, l, RET_V_W), BF16)]
    state_layer = None
    if new_state is not None:
        state_layer, earlier = new_state
        state_spec = lambda n: pl.BlockSpec((None, n, 2, None, RET_DK, RET_DV),
                                            lambda i, h: (i, 0, 0, h, 0, 0))
        if state_layer:
            in_specs.append(state_spec(state_layer))
            args.append(earlier)
        out_specs.append(state_spec(state_layer + 1))
        out_shape.append(jax.ShapeDtypeStruct(
            (b, state_layer + 1, 2, RET_HEADS, RET_DK, RET_DV), F32))
    return pl.pallas_call(
        functools.partial(_retention_kernel, n_chunks=n_chunks, has_s0=state is not None,
                          state_layer=state_layer),
        grid=(b, RET_HEADS),
        in_specs=in_specs,
        out_specs=out_specs,
        out_shape=out_shape,
        scratch_shapes=[pltpu.VMEM((n_chunks, 2 * RET_DK, RET_DV), BF16)],
        compiler_params=_params("arbitrary", "arbitrary"),
        name="retention",
    )(*args)


def _fold_lanes(x, op):
    return functools.reduce(op, [x[:, c:c + LANES] for c in range(0, x.shape[1], LANES)])


def _pair_attention(q2, keys, values, biases):
    n_q = q2.shape[0]
    lane = lax.broadcasted_iota(jnp.int32, (1, LANES), 1)
    lo_mask = lane < NA_DH
    zero = jnp.zeros_like(q2)
    qq = jnp.concatenate([jnp.where(lo_mask, q2, zero), jnp.where(lo_mask, zero, q2)], axis=0)
    scores = []
    for kt, bt in zip(keys, biases):
        s = _dot_nt(qq, kt)
        scores.append(s if bt is None else s + bt)
    m = functools.reduce(jnp.maximum, [_fold_lanes(s, jnp.maximum) for s in scores])
    m = m.max(axis=-1, keepdims=True)
    denom = None
    acc = None
    for s, v in zip(scores, values):
        e = jnp.exp(s - m)
        part = _fold_lanes(e, jnp.add)
        pv = _dot(e.astype(BF16), v)
        denom = part if denom is None else denom + part
        acc = pv if acc is None else acc + pv
    out = acc * (1.0 / denom.sum(axis=-1, keepdims=True))
    return jnp.where(lo_mask, out[:n_q], out[n_q:])


def _ctx_attn_kernel(q_ref, k_ref, v_ref, o_ref):
    for hp in range(HEAD_PAIRS):
        sl = slice(hp * LANES, (hp + 1) * LANES)
        out = _pair_attention(q_ref[:, sl], [k_ref[:, sl].astype(BF16)],
                              [v_ref[:, sl].astype(BF16)], [None])
        o_ref[:, sl] = out.astype(o_ref.dtype)


def _ctx_attention(nq, k_cache, v_cache, layer):
    b, l, _ = nq.shape
    spec = pl.BlockSpec((None, l, NA_W), lambda i: (i, 0, 0))
    kv_spec = pl.BlockSpec((None, None, l, NA_W), lambda i: (i, layer, 0, 0))
    return pl.pallas_call(
        _ctx_attn_kernel,
        grid=(b,),
        in_specs=[spec, kv_spec, kv_spec],
        out_specs=spec,
        out_shape=jax.ShapeDtypeStruct((b, l, NA_W), BF16),
        compiler_params=_params("arbitrary"),
        name="ctx_attention",
    )(nq, k_cache, v_cache)


def _na_window_start(j, rows):
    return int(np.clip(NA_Q_ROWS * j - NA_WIN_H // 2, 0, rows - NA_KEY_ROWS))


def _na_bias_table(rpb, rows):
    heads, n_dr, n_dc = rpb.shape
    lead = GRID_W - NA_WIN_W
    padded = jnp.pad(rpb.astype(F32), ((0, 0), (0, 0), (lead, 2 * GRID_W - lead - n_dc)))
    skew = jnp.tile(padded, (1, 1, GRID_W))[..., :GRID_W * (2 * GRID_W - 1)]
    toeplitz = skew.reshape(heads, n_dr, GRID_W, 2 * GRID_W - 1)[..., GRID_W - 1:]
    qc = np.arange(GRID_W)[:, None]
    kc = np.arange(GRID_W)[None, :]
    cs = np.clip(qc - NA_WIN_W // 2, 0, GRID_W - NA_WIN_W)
    col_valid = (kc >= cs) & (kc < cs + NA_WIN_W)
    blocks = jnp.where(col_valid, toeplitz, MASKED)
    blocks = jnp.concatenate([blocks, jnp.full((heads, 1, GRID_W, GRID_W), MASKED, F32)], axis=1)
    a = np.arange(NA_Q_ROWS)[:, None]
    kk = np.arange(NA_KEY_ROWS)[None, :]
    index = []
    for j in NA_PATTERNS:
        qr = NA_Q_ROWS * j + a
        kr = _na_window_start(j, rows) + kk
        rs = np.clip(qr - NA_WIN_H // 2, 0, rows - NA_WIN_H)
        row_valid = (kr >= rs) & (kr < rs + NA_WIN_H)
        index.append(np.where(row_valid, kr - qr + NA_WIN_H - 1, n_dr))
    table = blocks[:, np.stack(index)]
    table = table.transpose(1, 0, 2, 4, 3, 5)
    return table.reshape(len(NA_PATTERNS), heads, NA_Q_BLOCK, NA_KEY_ROWS * GRID_W)


def _na_kernel(q_ref, k_ref, v_ref, ck_ref, cv_ref, tab_ref, o_ref):
    j = pl.program_id(1)
    n_blocks = pl.num_programs(1)
    pattern = jnp.minimum(j, 2) + jnp.maximum(j - (n_blocks - 3), 0)
    start = jnp.clip(j - 2, 0, n_blocks - NA_KEY_BLOCKS) * NA_Q_BLOCK
    window = pl.ds(pl.multiple_of(start, NA_Q_BLOCK), NA_KEY_BLOCKS * LANES)
    for hp in range(HEAD_PAIRS):
        sl = slice(hp * LANES, (hp + 1) * LANES)
        bias = tab_ref[pattern, 2 * hp:2 * hp + 2].reshape(2 * NA_Q_BLOCK, -1)
        out = _pair_attention(q_ref[:, sl], [k_ref[window, sl], ck_ref[:, sl]],
                              [v_ref[window, sl], cv_ref[:, sl]], [bias, None])
        o_ref[:, sl] = out.astype(o_ref.dtype)


def _na_attention(nq, nk, nv, ck, cv, table):
    b, l, _ = nq.shape
    n_blocks = l // NA_Q_BLOCK
    assert [min(j, 2) + max(j - (n_blocks - 3), 0) for j in NA_PATTERNS] == list(range(5))
    tile = pl.BlockSpec((None, NA_Q_BLOCK, NA_W), lambda i, j: (i, j, 0))
    whole = lambda a: pl.BlockSpec((None,) + a.shape[1:], lambda i, j: (i, 0, 0))
    return pl.pallas_call(
        _na_kernel,
        grid=(b, n_blocks),
        in_specs=[tile, whole(nk), whole(nv), whole(ck), whole(cv), _resident(table.shape)],
        out_specs=tile,
        out_shape=jax.ShapeDtypeStruct((b, l, NA_W), BF16),
        compiler_params=_params("arbitrary", "arbitrary"),
        name="na_attention",
    )(nq, nk, nv, ck, cv, table)


def _tail_kernel(x_ref, mod_ref, yret_ref, yna_ref, gate_ref, n1_ref, n2_ref, n3_ref,
                 w_ret_ref, w_na_ref, w_o_ref, w_g_ref, w_u_ref, w_d_ref, o_ref):
    mod = mod_ref[0]
    g1 = mod[:, 2 * D_MODEL:3 * D_MODEL]
    sh2 = mod[:, 3 * D_MODEL:4 * D_MODEL]
    sc2 = mod[:, 4 * D_MODEL:5 * D_MODEL]
    g2 = mod[:, 5 * D_MODEL:6 * D_MODEL]
    g_ret = gate_ref[:, 0:D_MODEL].astype(F32)
    g_na = gate_ref[:, D_MODEL:2 * D_MODEL].astype(F32)
    branches = (g_ret * _dot(yret_ref[...], w_ret_ref[...])
                + g_na * _dot(yna_ref[...], w_na_ref[...]))
    mixed = _dot(branches.astype(BF16), w_o_ref[...])
    x = x_ref[...] + g1 * _rms(mixed, n1_ref[...])
    hf = (_rms(x, n2_ref[...]) * (1.0 + sc2) + sh2).astype(BF16)
    act = (_silu(_dot(hf, w_g_ref[...])) * _dot(hf, w_u_ref[...])).astype(BF16)
    o_ref[...] = x + g2 * _rms(_dot(act, w_d_ref[...]), n3_ref[...])


def _tail(x, mod, mod_row, y_ret, y_na, gates, n1, n2, n3, w_ret, w_na, w_o, w_g, w_u, w_d):
    b, l, _ = x.shape
    tm = min(TOKEN_TILE, l)
    tok = lambda w: pl.BlockSpec((None, tm, w), lambda i, j: (i, j, 0))
    vec = lambda a: a.reshape(1, D_MODEL)
    return pl.pallas_call(
        _tail_kernel,
        grid=(b, l // tm),
        in_specs=[
            tok(D_MODEL),
            pl.BlockSpec((1, 1, 6 * D_MODEL), lambda i, j: (mod_row(i), 0, 0)),
            tok(RET_V_W), tok(NA_W), tok(2 * D_MODEL),
            _resident((1, D_MODEL)), _resident((1, D_MODEL)), _resident((1, D_MODEL)),
            _resident(w_ret.shape), _resident(w_na.shape), _resident(w_o.shape),
            _resident(w_g.shape), _resident(w_u.shape), _resident(w_d.shape),
        ],
        out_specs=tok(D_MODEL),
        out_shape=jax.ShapeDtypeStruct((b, l, D_MODEL), F32),
        compiler_params=_params("arbitrary", "arbitrary"),
        name="tail",
    )(x, mod, y_ret, y_na, gates, vec(n1), vec(n2), vec(n3), w_ret, w_na, w_o, w_g, w_u, w_d)


def _rope_tables(l):
    t = jnp.arange(l)
    row = (t // GRID_W).astype(F32)
    col = (t % GRID_W).astype(F32)
    n_freq = RET_DK // 4
    inv = ROPE_BASE ** (-jnp.arange(n_freq, dtype=F32) / n_freq)
    ang = jnp.concatenate([row[:, None] * inv, col[:, None] * inv], axis=-1)
    cos = jnp.cos(ang)
    sin = jnp.sin(ang)
    return jnp.concatenate([cos, cos], axis=-1), jnp.concatenate([-sin, sin], axis=-1)


def kernel(x_prompt, x_sample, cache_na_k, cache_na_v, state_ret, c, c_ctx, w_ada, b_ada,
           norm_pre_mix, norm_post_mix, norm_pre_ffn, norm_post_ffn, w_in, ret_decay_logit,
           ret_gn_gain, na_rpb, w_ret_out, w_na_out, w_gate, w_o, w_ffn_gate, w_ffn_up,
           w_ffn_down):
    dec_b, dec_l, _ = x_sample.shape
    past = cache_na_k.shape[2]
    cvec = jnp.zeros((MOD_ROWS, D_MODEL), F32).at[0].set(c_ctx).at[1:1 + dec_b].set(c)
    mods = _adaln(cvec, w_ada, b_ada).reshape(DEPTH, MOD_ROWS, 1, 6 * D_MODEL)
    log_g = jax.nn.log_sigmoid(ret_decay_logit.astype(F32))
    rope_tabs = _rope_tables(dec_l)
    bf = lambda a: a.astype(BF16)
    w_in_b, w_gate_b = bf(w_in), bf(w_gate)
    w_ret_b, w_na_b, w_o_b = bf(w_ret_out), bf(w_na_out), bf(w_o)
    w_g_b, w_u_b, w_d_b = bf(w_ffn_gate), bf(w_ffn_up), bf(w_ffn_down)
    ck_all = bf(cache_na_k).reshape(dec_b, DEPTH, past, NA_W)
    cv_all = bf(cache_na_v).reshape(dec_b, DEPTH, past, NA_W)
    ctx_row = lambda i: 0
    dec_row = lambda i: i + 1

    def tail(x, l, row, y_ret, y_na, gates):
        return _tail(x, mods[l], row, y_ret, y_na, gates, norm_post_mix[l], norm_pre_ffn[l],
                     norm_post_ffn[l], w_ret_b[l], w_na_b[l], w_o_b[l], w_g_b[l], w_u_b[l],
                     w_d_b[l])

    ctx_b, ctx_l, _ = x_prompt.shape
    flat = lambda a: a.reshape(1, ctx_b * ctx_l, a.shape[-1])
    per_seq = lambda a: a.reshape(ctx_b, ctx_l, a.shape[-1])
    x = flat(x_prompt)
    new_k = new_v = new_s = None
    for l in range(DEPTH):
        rq, rk, rv, rg, nq, new_k, new_v, gates = _premix(
            x, mods[l], ctx_row, norm_pre_mix[l], w_in_b[l], w_gate_b[l],
            cache=(l, ctx_l, new_k, new_v))
        y_ret, new_s = _retention(log_g[l], per_seq(rq), per_seq(rk), per_seq(rv), per_seq(rg),
                                  ret_gn_gain[l], new_state=(l, new_s))
        y_na = _ctx_attention(per_seq(nq), new_k, new_v, l)
        x = tail(x, l, ctx_row, flat(y_ret), flat(y_na), gates)
    y_prompt = per_seq(x)
    new_k = new_k.reshape(ctx_b, DEPTH, ctx_l, NA_HEADS, NA_DH)
    new_v = new_v.reshape(ctx_b, DEPTH, ctx_l, NA_HEADS, NA_DH)

    x = x_sample
    for l in range(DEPTH):
        rq, rk, rv, rg, nq, nk, nv, gates = _premix(
            x, mods[l], dec_row, norm_pre_mix[l], w_in_b[l], w_gate_b[l], rope_tabs=rope_tabs)
        (y_ret,) = _retention(log_g[l], rq, rk, rv, rg, ret_gn_gain[l], state=state_ret, layer=l)
        table = _na_bias_table(na_rpb[l], dec_l // GRID_W)
        y_na = _na_attention(nq, nk, nv, ck_all[:, l], cv_all[:, l], table)
        x = tail(x, l, dec_row, y_ret, y_na, gates)
    return (y_prompt, x, new_k, new_v, new_s)
```

```python
import functools

import numpy as np
import jax
import jax.numpy as jnp
from jax import lax
from jax.experimental import pallas as pl
from jax.experimental.pallas import tpu as pltpu

F32 = jnp.float32
BF16 = jnp.bfloat16

D_MODEL = 1024
DEPTH = 2
GRID_W = 64
RET_HEADS = 4
RET_DK = 128
RET_DV = 256
RET_QK_W = RET_HEADS * RET_DK
RET_V_W = RET_HEADS * RET_DV
RET_CHUNK = 128
NA_HEADS = 8
NA_DH = 64
NA_W = NA_HEADS * NA_DH
NA_WIN_H = 8
NA_WIN_W = 16
D_FF = 2816
ROPE_BASE = 10000.0
EPS = 1e-6
IN_WIDTH = 2 * RET_QK_W + 2 * RET_V_W + 3 * NA_W

LANES = 128
HEAD_PAIRS = NA_W // LANES
MOD_ROWS = 16
NA_Q_ROWS = 2
NA_Q_BLOCK = NA_Q_ROWS * GRID_W
NA_KEY_BLOCKS = 5
NA_KEY_ROWS = NA_KEY_BLOCKS * LANES // GRID_W
NA_PATTERNS = (0, 1, 2, 14, 15)
MASKED = -1e30
VMEM_LIMIT = 56 * 1024 * 1024
TOKEN_TILE = 512
RET_UNROLL = 8
SHORT_SEQ_TOKENS = 1024


def _sigmoid(x):
    return 1.0 / (1.0 + jnp.exp(-x))


def _silu(x):
    return x * _sigmoid(x)


def _rms(x, g):
    return x * lax.rsqrt(jnp.mean(x * x, axis=-1, keepdims=True) + EPS) * g


def _dot(a, b):
    return jnp.dot(a, b, preferred_element_type=F32)


def _dot_nt(a, b):
    return lax.dot_general(a, b, (((1,), (1,)), ((), ())), preferred_element_type=F32)


def _resident(shape):
    zeros = (0,) * len(shape)
    return pl.BlockSpec(shape, lambda *_: zeros, pipeline_mode=pl.Buffered(1))


def _params(*sem):
    return pltpu.CompilerParams(dimension_semantics=sem, vmem_limit_bytes=VMEM_LIMIT)


def _adaln_kernel(c_ref, w_ref, b_ref, o_ref):
    s = _silu(c_ref[...]).astype(BF16)
    o_ref[0] = _dot(s, w_ref[0].astype(BF16)) + b_ref[0]


def _adaln(cvec, w_ada, b_ada):
    tn = 1536
    n = 6 * D_MODEL
    return pl.pallas_call(
        _adaln_kernel,
        grid=(DEPTH, n // tn),
        in_specs=[
            pl.BlockSpec((MOD_ROWS, D_MODEL), lambda l, j: (0, 0)),
            pl.BlockSpec((1, D_MODEL, tn), lambda l, j: (l, 0, j)),
            pl.BlockSpec((1, 1, tn), lambda l, j: (l, 0, j)),
        ],
        out_specs=pl.BlockSpec((1, MOD_ROWS, tn), lambda l, j: (l, 0, j)),
        out_shape=jax.ShapeDtypeStruct((DEPTH, MOD_ROWS, n), F32),
        compiler_params=_params("arbitrary", "arbitrary"),
        name="adaln",
    )(cvec, w_ada, b_ada.reshape(DEPTH, 1, n))


def _premix_kernel(*refs, rope, cache_layer):
    refs = list(refs)
    x_ref, mod_ref, g_ref, w_in_ref, w_gate_ref = refs[:5]
    pos = 5
    if rope:
        cos_ref, sin_ref = refs[pos:pos + 2]
        pos += 2
    if cache_layer:
        k_prev_ref, v_prev_ref = refs[pos:pos + 2]
        pos += 2
    rq_ref, rk_ref, rv_ref, rg_ref, nq_ref, nk_ref, nv_ref, gate_ref = refs[pos:]
    mod = mod_ref[0]
    sh1 = mod[:, 0:D_MODEL]
    sc1 = mod[:, D_MODEL:2 * D_MODEL]
    hm = (_rms(x_ref[...], g_ref[...]) * (1.0 + sc1) + sh1).astype(BF16)

    def proj(lo, width):
        return _dot(hm, w_in_ref[:, lo:lo + width])

    def rotary(t):
        if not rope:
            return t
        cos = cos_ref[...]
        sin = sin_ref[...]
        heads = []
        for h in range(RET_HEADS):
            blk = t[:, h * RET_DK:(h + 1) * RET_DK]
            heads.append(blk * cos + pltpu.roll(blk, RET_DK // 2, axis=1) * sin)
        return jnp.concatenate(heads, axis=1)

    def store_kv(ref, prev_ref, t):
        if cache_layer is None:
            ref[...] = t.astype(ref.dtype)
            return
        if cache_layer:
            ref[:, 0:cache_layer] = prev_ref[...]
        ref[:, cache_layer] = t.reshape(ref.shape[0], ref.shape[2], ref.shape[3])

    lo = 0
    rq_ref[...] = rotary(proj(lo, RET_QK_W)).astype(rq_ref.dtype)
    lo += RET_QK_W
    rk_ref[...] = (rotary(proj(lo, RET_QK_W)) * (RET_DK ** -0.5)).astype(rk_ref.dtype)
    lo += RET_QK_W
    rv_ref[...] = proj(lo, RET_V_W).astype(rv_ref.dtype)
    lo += RET_V_W
    rg_ref[...] = proj(lo, RET_V_W).astype(rg_ref.dtype)
    lo += RET_V_W
    nq_ref[...] = (proj(lo, NA_W) * (NA_DH ** -0.5)).astype(nq_ref.dtype)
    lo += NA_W
    store_kv(nk_ref, k_prev_ref if cache_layer else None, proj(lo, NA_W))
    lo += NA_W
    store_kv(nv_ref, v_prev_ref if cache_layer else None, proj(lo, NA_W))
    gate_ref[...] = _sigmoid(_dot(hm, w_gate_ref[...])).astype(gate_ref.dtype)


def _premix(x, mod, mod_row, g, w_in, w_gate, rope_tabs=None, cache=None):
    b, l, _ = x.shape
    tm = min(TOKEN_TILE, l)
    tok = lambda w: pl.BlockSpec((None, tm, w), lambda i, j: (i, j, 0))
    in_specs = [
        tok(D_MODEL),
        pl.BlockSpec((1, 1, 6 * D_MODEL), lambda i, j: (mod_row(i), 0, 0)),
        _resident((1, D_MODEL)),
        _resident((D_MODEL, IN_WIDTH)),
        _resident((D_MODEL, 2 * D_MODEL)),
    ]
    args = [x, mod, g.reshape(1, D_MODEL), w_in, w_gate]
    if rope_tabs is not None:
        in_specs += [pl.BlockSpec((tm, RET_DK), lambda i, j: (j, 0))] * 2
        args += list(rope_tabs)
    widths = (RET_QK_W, RET_QK_W, RET_V_W, RET_V_W, NA_W, NA_W, NA_W, 2 * D_MODEL)
    out_specs = [tok(w) for w in widths]
    out_shape = [jax.ShapeDtypeStruct((b, l, w), BF16) for w in widths]
    cache_layer = None
    if cache is not None:
        cache_layer, seq, k_prev, v_prev = cache
        assert b == 1 and tm % seq == 0
        per_tile = tm // seq
        kv_spec = lambda n: pl.BlockSpec((per_tile, n, seq, NA_W), lambda i, j: (j, 0, 0, 0))
        if cache_layer:
            in_specs += [kv_spec(cache_layer)] * 2
            args += [k_prev, v_prev]
        kv_shape = jax.ShapeDtypeStruct((l // seq, cache_layer + 1, seq, NA_W), F32)
        out_specs[5:7] = [kv_spec(cache_layer + 1)] * 2
        out_shape[5:7] = [kv_shape] * 2
    return pl.pallas_call(
        functools.partial(_premix_kernel, rope=rope_tabs is not None, cache_layer=cache_layer),
        grid=(b, l // tm),
        in_specs=in_specs,
        out_specs=out_specs,
        out_shape=out_shape,
        compiler_params=_params("arbitrary", "arbitrary"),
        name="premix",
    )(*args)


def _retention_kernel(*refs, n_chunks, has_s0, state_layer):
    refs = list(refs)
    lg_ref, q_ref, k_ref, v_ref, rg_ref, gain_ref = refs[:6]
    pos = 6
    if has_s0:
        s0_ref = refs[pos]
        pos += 1
    if state_layer:
        sprev_ref = refs[pos]
        pos += 1
    y_ref = refs[pos]
    pos += 1
    if state_layer is not None:
        sout_ref = refs[pos]
        pos += 1
    s_scr = refs[pos]

    c = RET_CHUNK
    h = pl.program_id(1)
    lgf = lg_ref[0, h]
    lgb = lg_ref[1, h]
    row = lax.broadcasted_iota(jnp.int32, (c, c), 0).astype(F32)
    col = lax.broadcasted_iota(jnp.int32, (c, c), 1).astype(F32)
    rel = row - col
    decay = (jnp.where(rel >= 0, jnp.exp(lgf * jnp.maximum(rel, 0.0)), 0.0)
             + jnp.where(rel <= 0, jnp.exp(lgb * jnp.maximum(-rel, 0.0)), 0.0))
    p = lax.broadcasted_iota(jnp.int32, (c, 1), 0).astype(F32)
    xi_f = jnp.exp(lgf * (p + 1.0))
    xi_b = jnp.exp(lgb * (c - p))
    zeta_f = jnp.exp(lgf * (c - 1.0 - p))
    zeta_b = jnp.exp(lgb * p)
    chunk_len = jnp.full((1, RET_DV), float(c), F32)
    g_f = jnp.exp(lgf * chunk_len)
    g_b = jnp.exp(lgb * chunk_len)

    gain = gain_ref[...]
    unroll = min(n_chunks, RET_UNROLL)

    def chunk(ref, seq, i):
        return ref[seq, pl.ds(pl.multiple_of(i * c, c), c), :]

    for seq in range(q_ref.shape[0]):
        def outer_kv(i, zeta):
            kz = (chunk(k_ref, seq, i).astype(F32) * zeta).T.astype(BF16)
            return _dot(kz, chunk(v_ref, seq, i))

        if has_s0:
            s_f0 = s0_ref[seq, 0]
            s_b0 = s0_ref[seq, 1]
        else:
            s_f0 = jnp.zeros((RET_DK, RET_DV), F32)
            s_b0 = s_f0

        def scan(t, carry):
            s_f, s_b = carry
            i_b = n_chunks - 1 - t
            s_scr[seq, t, 0:RET_DK, :] = s_f.astype(BF16)
            s_scr[seq, i_b, RET_DK:2 * RET_DK, :] = s_b.astype(BF16)
            return (g_f * s_f + outer_kv(t, zeta_f), g_b * s_b + outer_kv(i_b, zeta_b))

        s_f, s_b = lax.fori_loop(0, n_chunks, scan, (s_f0, s_b0), unroll=unroll)
        if state_layer is not None:
            if state_layer:
                sout_ref[seq, 0:state_layer] = sprev_ref[seq]
            sout_ref[seq, state_layer, 0] = s_f
            sout_ref[seq, state_layer, 1] = s_b

        def emit(i, _):
            qi = chunk(q_ref, seq, i)
            scores = (_dot_nt(qi, chunk(k_ref, seq, i)) * decay).astype(BF16)
            qf = qi.astype(F32)
            qx = jnp.concatenate([(qf * xi_f).astype(BF16), (qf * xi_b).astype(BF16)], axis=1)
            y = _dot(scores, chunk(v_ref, seq, i)) + _dot(qx, s_scr[seq, i])
            mu = jnp.mean(y, axis=-1, keepdims=True)
            d = y - mu
            var = jnp.mean(d * d, axis=-1, keepdims=True)
            yn = d * lax.rsqrt(var + EPS) * gain
            out = _silu(chunk(rg_ref, seq, i).astype(F32)) * yn
            y_ref[seq, pl.ds(pl.multiple_of(i * c, c), c), :] = out.astype(y_ref.dtype)
            return 0

        lax.fori_loop(0, n_chunks, emit, 0, unroll=unroll)


def _retention(log_g, rq, rk, rv, rg, gain, state=None, layer=0, new_state=None):
    b, l, _ = rq.shape
    n_chunks = l // RET_CHUNK
    seqs = max(1, SHORT_SEQ_TOKENS // l)
    assert b % seqs == 0
    head = lambda w: pl.BlockSpec((seqs, l, w), lambda i, h: (i, 0, h))
    in_specs = [pl.BlockSpec(memory_space=pltpu.SMEM), head(RET_DK), head(RET_DK),
                head(RET_DV), head(RET_DV), pl.BlockSpec((1, RET_DV), lambda i, h: (0, h))]
    args = [log_g, rq, rk, rv, rg, gain.reshape(1, RET_V_W)]
    if state is not None:
        in_specs.append(pl.BlockSpec((seqs, None, 2, None, RET_DK, RET_DV),
                                     lambda i, h: (i, layer, 0, h, 0, 0)))
        args.append(state)
    out_specs = [head(RET_DV)]
    out_shape = [jax.ShapeDtypeStruct((b, l, RET_V_W), BF16)]
    state_layer = None
    if new_state is not None:
        state_layer, earlier = new_state
        state_spec = lambda n: pl.BlockSpec((seqs, n, 2, None, RET_DK, RET_DV),
                                            lambda i, h: (i, 0, 0, h, 0, 0))
        if state_layer:
            in_specs.append(state_spec(state_layer))
            args.append(earlier)
        out_specs.append(state_spec(state_layer + 1))
        out_shape.append(jax.ShapeDtypeStruct(
            (b, state_layer + 1, 2, RET_HEADS, RET_DK, RET_DV), F32))
    return pl.pallas_call(
        functools.partial(_retention_kernel, n_chunks=n_chunks, has_s0=state is not None,
                          state_layer=state_layer),
        grid=(b // seqs, RET_HEADS),
        in_specs=in_specs,
        out_specs=out_specs,
        out_shape=out_shape,
        scratch_shapes=[pltpu.VMEM((seqs, n_chunks, 2 * RET_DK, RET_DV), BF16)],
        compiler_params=_params("arbitrary", "arbitrary"),
        name="retention",
    )(*args)


def _fold_lanes(x, op):
    return functools.reduce(op, [x[:, c:c + LANES] for c in range(0, x.shape[1], LANES)])


def _pair_attention(q2, keys, values, biases):
    n_q = q2.shape[0]
    lane = lax.broadcasted_iota(jnp.int32, (1, LANES), 1)
    lo_mask = lane < NA_DH
    zero = jnp.zeros_like(q2)
    qq = jnp.concatenate([jnp.where(lo_mask, q2, zero), jnp.where(lo_mask, zero, q2)], axis=0)
    scores = []
    for kt, bt in zip(keys, biases):
        s = _dot_nt(qq, kt)
        scores.append(s if bt is None else s + bt)
    m = functools.reduce(jnp.maximum, [_fold_lanes(s, jnp.maximum) for s in scores])
    m = m.max(axis=-1, keepdims=True)
    denom = None
    acc = None
    for s, v in zip(scores, values):
        e = jnp.exp(s - m)
        part = _fold_lanes(e, jnp.add)
        pv = _dot(e.astype(BF16), v)
        denom = part if denom is None else denom + part
        acc = pv if acc is None else acc + pv
    out = acc * (1.0 / denom.sum(axis=-1, keepdims=True))
    return jnp.where(lo_mask, out[:n_q], out[n_q:])


def _ctx_attn_kernel(q_ref, k_ref, v_ref, o_ref):
    for seq in range(q_ref.shape[0]):
        for hp in range(HEAD_PAIRS):
            sl = slice(hp * LANES, (hp + 1) * LANES)
            out = _pair_attention(q_ref[seq, :, sl], [k_ref[seq, :, sl].astype(BF16)],
                                  [v_ref[seq, :, sl].astype(BF16)], [None])
            o_ref[seq, :, sl] = out.astype(o_ref.dtype)


def _ctx_attention(nq, k_cache, v_cache, layer):
    b, l, _ = nq.shape
    seqs = max(1, SHORT_SEQ_TOKENS // l)
    assert b % seqs == 0
    spec = pl.BlockSpec((seqs, l, NA_W), lambda i: (i, 0, 0))
    kv_spec = pl.BlockSpec((seqs, None, l, NA_W), lambda i: (i, layer, 0, 0))
    return pl.pallas_call(
        _ctx_attn_kernel,
        grid=(b // seqs,),
        in_specs=[spec, kv_spec, kv_spec],
        out_specs=spec,
        out_shape=jax.ShapeDtypeStruct((b, l, NA_W), BF16),
        compiler_params=_params("arbitrary"),
        name="ctx_attention",
    )(nq, k_cache, v_cache)


def _na_window_start(j, rows):
    return int(np.clip(NA_Q_ROWS * j - NA_WIN_H // 2, 0, rows - NA_KEY_ROWS))


def _na_bias_blocks(rpb):
    heads, n_dr, n_dc = rpb.shape
    lead = GRID_W - NA_WIN_W
    padded = jnp.pad(rpb.astype(F32), ((0, 0), (0, 0), (lead, 2 * GRID_W - lead - n_dc)))
    skew = jnp.tile(padded, (1, 1, GRID_W))[..., :GRID_W * (2 * GRID_W - 1)]
    toeplitz = skew.reshape(heads, n_dr, GRID_W, 2 * GRID_W - 1)[..., GRID_W - 1:]
    qc = np.arange(GRID_W)[:, None]
    kc = np.arange(GRID_W)[None, :]
    cs = np.clip(qc - NA_WIN_W // 2, 0, GRID_W - NA_WIN_W)
    col_valid = (kc >= cs) & (kc < cs + NA_WIN_W)
    blocks = jnp.where(col_valid, toeplitz, MASKED)
    blocks = jnp.concatenate([blocks, jnp.full((heads, 1, GRID_W, GRID_W), MASKED, F32)], axis=1)
    return blocks


def _na_block_index(rows):
    a = np.arange(NA_Q_ROWS)[:, None]
    kk = np.arange(NA_KEY_ROWS)[None, :]
    index = []
    for j in NA_PATTERNS:
        qr = NA_Q_ROWS * j + a
        kr = _na_window_start(j, rows) + kk
        rs = np.clip(qr - NA_WIN_H // 2, 0, rows - NA_WIN_H)
        row_valid = (kr >= rs) & (kr < rs + NA_WIN_H)
        index.append(np.where(row_valid, kr - qr + NA_WIN_H - 1, 2 * NA_WIN_H - 1))
    return np.stack(index)


def _na_kernel(q_ref, k_ref, v_ref, ck_ref, cv_ref, blocks_ref, o_ref, tab_ref, *, index):
    j = pl.program_id(1)
    n_blocks = pl.num_programs(1)

    @pl.when((pl.program_id(0) == 0) & (j == 0))
    def _():
        for p, rows_p in enumerate(index):
            for a, row_a in enumerate(rows_p):
                for kk, blk in enumerate(row_a):
                    tab_ref[p, :, a * GRID_W:(a + 1) * GRID_W,
                            kk * GRID_W:(kk + 1) * GRID_W] = blocks_ref[:, int(blk)]

    pattern = jnp.minimum(j, 2) + jnp.maximum(j - (n_blocks - 3), 0)
    start = jnp.clip(j - 2, 0, n_blocks - NA_KEY_BLOCKS) * NA_Q_BLOCK
    window = pl.ds(pl.multiple_of(start, NA_Q_BLOCK), NA_KEY_BLOCKS * LANES)
    for hp in range(HEAD_PAIRS):
        sl = slice(hp * LANES, (hp + 1) * LANES)
        bias = tab_ref[pattern, 2 * hp:2 * hp + 2].reshape(2 * NA_Q_BLOCK, -1)
        out = _pair_attention(q_ref[:, sl], [k_ref[window, sl], ck_ref[:, sl]],
                              [v_ref[window, sl], cv_ref[:, sl]], [bias, None])
        o_ref[:, sl] = out.astype(o_ref.dtype)


def _na_attention(nq, nk, nv, ck, cv, blocks):
    b, l, _ = nq.shape
    n_blocks = l // NA_Q_BLOCK
    assert [min(j, 2) + max(j - (n_blocks - 3), 0) for j in NA_PATTERNS] == list(range(5))
    tile = pl.BlockSpec((None, NA_Q_BLOCK, NA_W), lambda i, j: (i, j, 0))
    whole = lambda a: pl.BlockSpec((None,) + a.shape[1:], lambda i, j: (i, 0, 0))
    table = (len(NA_PATTERNS), NA_HEADS, NA_Q_BLOCK, NA_KEY_ROWS * GRID_W)
    return pl.pallas_call(
        functools.partial(_na_kernel, index=_na_block_index(l // GRID_W)),
        grid=(b, n_blocks),
        in_specs=[tile, whole(nk), whole(nv), whole(ck), whole(cv), _resident(blocks.shape)],
        out_specs=tile,
        out_shape=jax.ShapeDtypeStruct((b, l, NA_W), BF16),
        scratch_shapes=[pltpu.VMEM(table, F32)],
        compiler_params=_params("arbitrary", "arbitrary"),
        name="na_attention",
    )(nq, nk, nv, ck, cv, blocks)


def _tail_kernel(x_ref, mod_ref, yret_ref, yna_ref, gate_ref, n1_ref, n2_ref, n3_ref,
                 w_ret_ref, w_na_ref, w_o_ref, w_g_ref, w_u_ref, w_d_ref, o_ref):
    mod = mod_ref[0]
    g1 = mod[:, 2 * D_MODEL:3 * D_MODEL]
    sh2 = mod[:, 3 * D_MODEL:4 * D_MODEL]
    sc2 = mod[:, 4 * D_MODEL:5 * D_MODEL]
    g2 = mod[:, 5 * D_MODEL:6 * D_MODEL]
    g_ret = gate_ref[:, 0:D_MODEL].astype(F32)
    g_na = gate_ref[:, D_MODEL:2 * D_MODEL].astype(F32)
    branches = (g_ret * _dot(yret_ref[...], w_ret_ref[...])
                + g_na * _dot(yna_ref[...], w_na_ref[...]))
    mixed = _dot(branches.astype(BF16), w_o_ref[...])
    x = x_ref[...] + g1 * _rms(mixed, n1_ref[...])
    hf = (_rms(x, n2_ref[...]) * (1.0 + sc2) + sh2).astype(BF16)
    act = (_silu(_dot(hf, w_g_ref[...])) * _dot(hf, w_u_ref[...])).astype(BF16)
    o_ref[...] = x + g2 * _rms(_dot(act, w_d_ref[...]), n3_ref[...])


def _tail(x, mod, mod_row, y_ret, y_na, gates, n1, n2, n3, w_ret, w_na, w_o, w_g, w_u, w_d):
    b, l, _ = x.shape
    tm = min(TOKEN_TILE, l)
    tok = lambda w: pl.BlockSpec((None, tm, w), lambda i, j: (i, j, 0))
    vec = lambda a: a.reshape(1, D_MODEL)
    return pl.pallas_call(
        _tail_kernel,
        grid=(b, l // tm),
        in_specs=[
            tok(D_MODEL),
            pl.BlockSpec((1, 1, 6 * D_MODEL), lambda i, j: (mod_row(i), 0, 0)),
            tok(RET_V_W), tok(NA_W), tok(2 * D_MODEL),
            _resident((1, D_MODEL)), _resident((1, D_MODEL)), _resident((1, D_MODEL)),
            _resident(w_ret.shape), _resident(w_na.shape), _resident(w_o.shape),
            _resident(w_g.shape), _resident(w_u.shape), _resident(w_d.shape),
        ],
        out_specs=tok(D_MODEL),
        out_shape=jax.ShapeDtypeStruct((b, l, D_MODEL), F32),
        compiler_params=_params("arbitrary", "arbitrary"),
        name="tail",
    )(x, mod, y_ret, y_na, gates, vec(n1), vec(n2), vec(n3), w_ret, w_na, w_o, w_g, w_u, w_d)


def _rope_tables(l):
    t = jnp.arange(l)
    row = (t // GRID_W).astype(F32)
    col = (t % GRID_W).astype(F32)
    n_freq = RET_DK // 4
    inv = ROPE_BASE ** (-jnp.arange(n_freq, dtype=F32) / n_freq)
    ang = jnp.concatenate([row[:, None] * inv, col[:, None] * inv], axis=-1)
    cos = jnp.cos(ang)
    sin = jnp.sin(ang)
    return jnp.concatenate([cos, cos], axis=-1), jnp.concatenate([-sin, sin], axis=-1)


def kernel(x_prompt, x_sample, cache_na_k, cache_na_v, state_ret, c, c_ctx, w_ada, b_ada,
           norm_pre_mix, norm_post_mix, norm_pre_ffn, norm_post_ffn, w_in, ret_decay_logit,
           ret_gn_gain, na_rpb, w_ret_out, w_na_out, w_gate, w_o, w_ffn_gate, w_ffn_up,
           w_ffn_down):
    dec_b, dec_l, _ = x_sample.shape
    past = cache_na_k.shape[2]
    cvec = jnp.zeros((MOD_ROWS, D_MODEL), F32).at[0].set(c_ctx).at[1:1 + dec_b].set(c)
    mods = _adaln(cvec, w_ada, b_ada).reshape(DEPTH, MOD_ROWS, 1, 6 * D_MODEL)
    log_g = jax.nn.log_sigmoid(ret_decay_logit.astype(F32))
    rope_tabs = _rope_tables(dec_l)
    bf = lambda a: a.astype(BF16)
    w_in_b, w_gate_b = bf(w_in), bf(w_gate)
    w_ret_b, w_na_b, w_o_b = bf(w_ret_out), bf(w_na_out), bf(w_o)
    w_g_b, w_u_b, w_d_b = bf(w_ffn_gate), bf(w_ffn_up), bf(w_ffn_down)
    ck_all = bf(cache_na_k).reshape(dec_b, DEPTH, past, NA_W)
    cv_all = bf(cache_na_v).reshape(dec_b, DEPTH, past, NA_W)
    ctx_row = lambda i: 0
    dec_row = lambda i: i + 1

    def tail(x, l, row, y_ret, y_na, gates):
        return _tail(x, mods[l], row, y_ret, y_na, gates, norm_post_mix[l], norm_pre_ffn[l],
                     norm_post_ffn[l], w_ret_b[l], w_na_b[l], w_o_b[l], w_g_b[l], w_u_b[l],
                     w_d_b[l])

    ctx_b, ctx_l, _ = x_prompt.shape
    flat = lambda a: a.reshape(1, ctx_b * ctx_l, a.shape[-1])
    per_seq = lambda a: a.reshape(ctx_b, ctx_l, a.shape[-1])
    x = flat(x_prompt)
    new_k = new_v = new_s = None
    for l in range(DEPTH):
        rq, rk, rv, rg, nq, new_k, new_v, gates = _premix(
            x, mods[l], ctx_row, norm_pre_mix[l], w_in_b[l], w_gate_b[l],
            cache=(l, ctx_l, new_k, new_v))
        y_ret, new_s = _retention(log_g[l], per_seq(rq), per_seq(rk), per_seq(rv), per_seq(rg),
                                  ret_gn_gain[l], new_state=(l, new_s))
        y_na = _ctx_attention(per_seq(nq), new_k, new_v, l)
        x = tail(x, l, ctx_row, flat(y_ret), flat(y_na), gates)
    y_prompt = per_seq(x)
    new_k = new_k.reshape(ctx_b, DEPTH, ctx_l, NA_HEADS, NA_DH)
    new_v = new_v.reshape(ctx_b, DEPTH, ctx_l, NA_HEADS, NA_DH)

    x = x_sample
    for l in range(DEPTH):
        rq, rk, rv, rg, nq, nk, nv, gates = _premix(
            x, mods[l], dec_row, norm_pre_mix[l], w_in_b[l], w_gate_b[l], rope_tabs=rope_tabs)
        (y_ret,) = _retention(log_g[l], rq, rk, rv, rg, ret_gn_gain[l], state=state_ret, layer=l)
        y_na = _na_attention(nq, nk, nv, ck_all[:, l], cv_all[:, l], _na_bias_blocks(na_rpb[l]))
        x = tail(x, l, dec_row, y_ret, y_na, gates)
    return (y_prompt, x, new_k, new_v, new_s)
```

```python
import functools
import math

import numpy as np
import jax
import jax.numpy as jnp
from jax import lax
from jax.experimental import pallas as pl
from jax.experimental.pallas import tpu as pltpu

F32 = jnp.float32
BF16 = jnp.bfloat16

D_MODEL = 1024
DEPTH = 2
GRID_W = 64
RET_HEADS = 4
RET_DK = 128
RET_DV = 256
RET_QK_W = RET_HEADS * RET_DK
RET_V_W = RET_HEADS * RET_DV
RET_CHUNK = 128
NA_HEADS = 8
NA_DH = 64
NA_W = NA_HEADS * NA_DH
NA_WIN_H = 8
NA_WIN_W = 16
D_FF = 2816
ROPE_BASE = 10000.0
EPS = 1e-6
IN_WIDTH = 2 * RET_QK_W + 2 * RET_V_W + 3 * NA_W

LANES = 128
HEAD_PAIRS = NA_W // LANES
MOD_ROWS = 16
NA_Q_ROWS = 2
NA_Q_BLOCK = NA_Q_ROWS * GRID_W
NA_KEY_BLOCKS = 5
NA_KEY_ROWS = NA_KEY_BLOCKS * LANES // GRID_W
NA_PATTERNS = (0, 1, 2, 14, 15)
MASKED = -1e30
LOG2_E = math.log2(math.e)
NA_Q_SCALE = NA_DH ** -0.5 * LOG2_E
VMEM_LIMIT = 56 * 1024 * 1024
TOKEN_TILE = 512
RET_UNROLL = 8
SHORT_SEQ_TOKENS = 1024


def _sigmoid(x):
    return 1.0 / (1.0 + jnp.exp(-x))


def _silu(x):
    return x * _sigmoid(x)


def _rms(x, g):
    return x * lax.rsqrt(jnp.mean(x * x, axis=-1, keepdims=True) + EPS) * g


def _dot(a, b):
    return jnp.dot(a, b, preferred_element_type=F32)


def _dot_nt(a, b):
    return lax.dot_general(a, b, (((1,), (1,)), ((), ())), preferred_element_type=F32)


def _resident(shape):
    zeros = (0,) * len(shape)
    return pl.BlockSpec(shape, lambda *_: zeros, pipeline_mode=pl.Buffered(1))


def _params(*sem):
    return pltpu.CompilerParams(dimension_semantics=sem, vmem_limit_bytes=VMEM_LIMIT)


def _adaln_kernel(c_ref, w_ref, b_ref, o_ref):
    s = _silu(c_ref[...]).astype(BF16)
    o_ref[0] = _dot(s, w_ref[0].astype(BF16)) + b_ref[0]


def _adaln(cvec, w_ada, b_ada):
    tn = 1536
    n = 6 * D_MODEL
    return pl.pallas_call(
        _adaln_kernel,
        grid=(DEPTH, n // tn),
        in_specs=[
            pl.BlockSpec((MOD_ROWS, D_MODEL), lambda l, j: (0, 0)),
            pl.BlockSpec((1, D_MODEL, tn), lambda l, j: (l, 0, j)),
            pl.BlockSpec((1, 1, tn), lambda l, j: (l, 0, j)),
        ],
        out_specs=pl.BlockSpec((1, MOD_ROWS, tn), lambda l, j: (l, 0, j)),
        out_shape=jax.ShapeDtypeStruct((DEPTH, MOD_ROWS, n), F32),
        compiler_params=_params("arbitrary", "arbitrary"),
        name="adaln",
    )(cvec, w_ada, b_ada.reshape(DEPTH, 1, n))


def _premix_kernel(*refs, rope, cache_layer):
    refs = list(refs)
    x_ref, mod_ref, g_ref, w_in_ref, w_gate_ref = refs[:5]
    pos = 5
    if rope:
        cos_ref, sin_ref = refs[pos:pos + 2]
        pos += 2
    if cache_layer:
        k_prev_ref, v_prev_ref = refs[pos:pos + 2]
        pos += 2
    rq_ref, rk_ref, rv_ref, rg_ref, nq_ref, nk_ref, nv_ref, gate_ref = refs[pos:]
    mod = mod_ref[0]
    sh1 = mod[:, 0:D_MODEL]
    sc1 = mod[:, D_MODEL:2 * D_MODEL]
    hm = (_rms(x_ref[...], g_ref[...]) * (1.0 + sc1) + sh1).astype(BF16)

    def proj(lo, width):
        return _dot(hm, w_in_ref[:, lo:lo + width])

    def rotary(t):
        if not rope:
            return t
        cos = cos_ref[...]
        sin = sin_ref[...]
        heads = []
        for h in range(RET_HEADS):
            blk = t[:, h * RET_DK:(h + 1) * RET_DK]
            heads.append(blk * cos + pltpu.roll(blk, RET_DK // 2, axis=1) * sin)
        return jnp.concatenate(heads, axis=1)

    def store_kv(ref, prev_ref, t):
        if cache_layer is None:
            ref[...] = t.astype(ref.dtype)
            return
        if cache_layer:
            ref[:, 0:cache_layer] = prev_ref[...]
        ref[:, cache_layer] = t.reshape(ref.shape[0], ref.shape[2], ref.shape[3])

    lo = 0
    rq_ref[...] = rotary(proj(lo, RET_QK_W)).astype(rq_ref.dtype)
    lo += RET_QK_W
    rk_ref[...] = (rotary(proj(lo, RET_QK_W)) * (RET_DK ** -0.5)).astype(rk_ref.dtype)
    lo += RET_QK_W
    rv_ref[...] = proj(lo, RET_V_W).astype(rv_ref.dtype)
    lo += RET_V_W
    rg_ref[...] = proj(lo, RET_V_W).astype(rg_ref.dtype)
    lo += RET_V_W
    nq_ref[...] = (proj(lo, NA_W) * NA_Q_SCALE).astype(nq_ref.dtype)
    lo += NA_W
    store_kv(nk_ref, k_prev_ref if cache_layer else None, proj(lo, NA_W))
    lo += NA_W
    store_kv(nv_ref, v_prev_ref if cache_layer else None, proj(lo, NA_W))
    gate_ref[...] = _sigmoid(_dot(hm, w_gate_ref[...])).astype(gate_ref.dtype)


def _premix(x, mod, mod_row, g, w_in, w_gate, rope_tabs=None, cache=None):
    b, l, _ = x.shape
    tm = min(TOKEN_TILE, l)
    tok = lambda w: pl.BlockSpec((None, tm, w), lambda i, j: (i, j, 0))
    in_specs = [
        tok(D_MODEL),
        pl.BlockSpec((1, 1, 6 * D_MODEL), lambda i, j: (mod_row(i), 0, 0)),
        _resident((1, D_MODEL)),
        _resident((D_MODEL, IN_WIDTH)),
        _resident((D_MODEL, 2 * D_MODEL)),
    ]
    args = [x, mod, g.reshape(1, D_MODEL), w_in, w_gate]
    if rope_tabs is not None:
        in_specs += [pl.BlockSpec((tm, RET_DK), lambda i, j: (j, 0))] * 2
        args += list(rope_tabs)
    widths = (RET_QK_W, RET_QK_W, RET_V_W, RET_V_W, NA_W, NA_W, NA_W, 2 * D_MODEL)
    out_specs = [tok(w) for w in widths]
    out_shape = [jax.ShapeDtypeStruct((b, l, w), BF16) for w in widths]
    cache_layer = None
    if cache is not None:
        cache_layer, seq, k_prev, v_prev = cache
        assert b == 1 and tm % seq == 0
        per_tile = tm // seq
        kv_spec = lambda n: pl.BlockSpec((per_tile, n, seq, NA_W), lambda i, j: (j, 0, 0, 0))
        if cache_layer:
            in_specs += [kv_spec(cache_layer)] * 2
            args += [k_prev, v_prev]
        kv_shape = jax.ShapeDtypeStruct((l // seq, cache_layer + 1, seq, NA_W), F32)
        out_specs[5:7] = [kv_spec(cache_layer + 1)] * 2
        out_shape[5:7] = [kv_shape] * 2
    return pl.pallas_call(
        functools.partial(_premix_kernel, rope=rope_tabs is not None, cache_layer=cache_layer),
        grid=(b, l // tm),
        in_specs=in_specs,
        out_specs=out_specs,
        out_shape=out_shape,
        compiler_params=_params("arbitrary", "arbitrary"),
        name="premix",
    )(*args)


def _retention_kernel(*refs, n_chunks, has_s0, state_layer):
    refs = list(refs)
    lg_ref, q_ref, k_ref, v_ref, rg_ref, gain_ref = refs[:6]
    pos = 6
    if has_s0:
        s0_ref = refs[pos]
        pos += 1
    if state_layer:
        sprev_ref = refs[pos]
        pos += 1
    y_ref = refs[pos]
    pos += 1
    if state_layer is not None:
        sout_ref = refs[pos]
        pos += 1
    s_scr = refs[pos]

    c = RET_CHUNK
    h = pl.program_id(1)
    lgf = lg_ref[0, h]
    lgb = lg_ref[1, h]
    row = lax.broadcasted_iota(jnp.int32, (c, c), 0).astype(F32)
    col = lax.broadcasted_iota(jnp.int32, (c, c), 1).astype(F32)
    rel = row - col
    decay = (jnp.where(rel >= 0, jnp.exp(lgf * jnp.maximum(rel, 0.0)), 0.0)
             + jnp.where(rel <= 0, jnp.exp(lgb * jnp.maximum(-rel, 0.0)), 0.0))
    p = lax.broadcasted_iota(jnp.int32, (c, 1), 0).astype(F32)
    xi_f = jnp.exp(lgf * (p + 1.0))
    xi_b = jnp.exp(lgb * (c - p))
    zeta_f = jnp.exp(lgf * (c - 1.0 - p))
    zeta_b = jnp.exp(lgb * p)
    chunk_len = jnp.full((1, RET_DV), float(c), F32)
    g_f = jnp.exp(lgf * chunk_len)
    g_b = jnp.exp(lgb * chunk_len)

    gain = gain_ref[...]
    unroll = min(n_chunks, RET_UNROLL)

    def chunk(ref, seq, i):
        return ref[seq, pl.ds(pl.multiple_of(i * c, c), c), :]

    for seq in range(q_ref.shape[0]):
        def outer_kv(i, zeta):
            kz = (chunk(k_ref, seq, i).astype(F32) * zeta).T.astype(BF16)
            return _dot(kz, chunk(v_ref, seq, i))

        if has_s0:
            s_f0 = s0_ref[seq, 0]
            s_b0 = s0_ref[seq, 1]
        else:
            s_f0 = jnp.zeros((RET_DK, RET_DV), F32)
            s_b0 = s_f0

        def scan(t, carry):
            s_f, s_b = carry
            i_b = n_chunks - 1 - t
            s_scr[seq, t, 0:RET_DK, :] = s_f.astype(BF16)
            s_scr[seq, i_b, RET_DK:2 * RET_DK, :] = s_b.astype(BF16)
            return (g_f * s_f + outer_kv(t, zeta_f), g_b * s_b + outer_kv(i_b, zeta_b))

        s_f, s_b = lax.fori_loop(0, n_chunks, scan, (s_f0, s_b0), unroll=unroll)
        if state_layer is not None:
            if state_layer:
                sout_ref[seq, 0:state_layer] = sprev_ref[seq]
            sout_ref[seq, state_layer, 0] = s_f
            sout_ref[seq, state_layer, 1] = s_b

        def emit(i, _):
            qi = chunk(q_ref, seq, i)
            scores = (_dot_nt(qi, chunk(k_ref, seq, i)) * decay).astype(BF16)
            qf = qi.astype(F32)
            qx = jnp.concatenate([(qf * xi_f).astype(BF16), (qf * xi_b).astype(BF16)], axis=1)
            y = _dot(scores, chunk(v_ref, seq, i)) + _dot(qx, s_scr[seq, i])
            mu = jnp.mean(y, axis=-1, keepdims=True)
            d = y - mu
            var = jnp.mean(d * d, axis=-1, keepdims=True)
            yn = d * lax.rsqrt(var + EPS) * gain
            out = _silu(chunk(rg_ref, seq, i).astype(F32)) * yn
            y_ref[seq, pl.ds(pl.multiple_of(i * c, c), c), :] = out.astype(y_ref.dtype)
            return 0

        lax.fori_loop(0, n_chunks, emit, 0, unroll=unroll)


def _retention(log_g, rq, rk, rv, rg, gain, state=None, layer=0, new_state=None):
    b, l, _ = rq.shape
    n_chunks = l // RET_CHUNK
    seqs = max(1, SHORT_SEQ_TOKENS // l)
    assert b % seqs == 0
    head = lambda w: pl.BlockSpec((seqs, l, w), lambda i, h: (i, 0, h))
    in_specs = [pl.BlockSpec(memory_space=pltpu.SMEM), head(RET_DK), head(RET_DK),
                head(RET_DV), head(RET_DV), pl.BlockSpec((1, RET_DV), lambda i, h: (0, h))]
    args = [log_g, rq, rk, rv, rg, gain.reshape(1, RET_V_W)]
    if state is not None:
        in_specs.append(pl.BlockSpec((seqs, None, 2, None, RET_DK, RET_DV),
                                     lambda i, h: (i, layer, 0, h, 0, 0)))
        args.append(state)
    out_specs = [head(RET_DV)]
    out_shape = [jax.ShapeDtypeStruct((b, l, RET_V_W), BF16)]
    state_layer = None
    if new_state is not None:
        state_layer, earlier = new_state
        state_spec = lambda n: pl.BlockSpec((seqs, n, 2, None, RET_DK, RET_DV),
                                            lambda i, h: (i, 0, 0, h, 0, 0))
        if state_layer:
            in_specs.append(state_spec(state_layer))
            args.append(earlier)
        out_specs.append(state_spec(state_layer + 1))
        out_shape.append(jax.ShapeDtypeStruct(
            (b, state_layer + 1, 2, RET_HEADS, RET_DK, RET_DV), F32))
    return pl.pallas_call(
        functools.partial(_retention_kernel, n_chunks=n_chunks, has_s0=state is not None,
                          state_layer=state_layer),
        grid=(b // seqs, RET_HEADS),
        in_specs=in_specs,
        out_specs=out_specs,
        out_shape=out_shape,
        scratch_shapes=[pltpu.VMEM((seqs, n_chunks, 2 * RET_DK, RET_DV), BF16)],
        compiler_params=_params("arbitrary", "arbitrary"),
        name="retention",
    )(*args)


def _fold_lanes(x, op):
    return functools.reduce(op, [x[:, c:c + LANES] for c in range(0, x.shape[1], LANES)])


def _pair_attention(q2, keys, values, biases):
    n_q = q2.shape[0]
    lane = lax.broadcasted_iota(jnp.int32, (1, LANES), 1)
    lo_mask = lane < NA_DH
    zero = jnp.zeros_like(q2)
    qq = jnp.concatenate([jnp.where(lo_mask, q2, zero), jnp.where(lo_mask, zero, q2)], axis=0)
    scores = []
    for kt, bt in zip(keys, biases):
        s = _dot_nt(qq, kt)
        scores.append(s if bt is None else s + bt)
    m = functools.reduce(jnp.maximum, [_fold_lanes(s, jnp.maximum) for s in scores])
    m = m.max(axis=-1, keepdims=True)
    denom = None
    acc = None
    for s, v in zip(scores, values):
        e = jnp.exp2(s - m)
        part = _fold_lanes(e, jnp.add)
        pv = _dot(e.astype(BF16), v)
        denom = part if denom is None else denom + part
        acc = pv if acc is None else acc + pv
    out = acc * (1.0 / denom.sum(axis=-1, keepdims=True))
    return jnp.where(lo_mask, out[:n_q], out[n_q:])


def _ctx_attn_kernel(q_ref, k_ref, v_ref, o_ref):
    for seq in range(q_ref.shape[0]):
        for hp in range(HEAD_PAIRS):
            sl = slice(hp * LANES, (hp + 1) * LANES)
            out = _pair_attention(q_ref[seq, :, sl], [k_ref[seq, :, sl].astype(BF16)],
                                  [v_ref[seq, :, sl].astype(BF16)], [None])
            o_ref[seq, :, sl] = out.astype(o_ref.dtype)


def _ctx_attention(nq, k_cache, v_cache, layer):
    b, l, _ = nq.shape
    seqs = max(1, SHORT_SEQ_TOKENS // l)
    assert b % seqs == 0
    spec = pl.BlockSpec((seqs, l, NA_W), lambda i: (i, 0, 0))
    kv_spec = pl.BlockSpec((seqs, None, l, NA_W), lambda i: (i, layer, 0, 0))
    return pl.pallas_call(
        _ctx_attn_kernel,
        grid=(b // seqs,),
        in_specs=[spec, kv_spec, kv_spec],
        out_specs=spec,
        out_shape=jax.ShapeDtypeStruct((b, l, NA_W), BF16),
        compiler_params=_params("arbitrary"),
        name="ctx_attention",
    )(nq, k_cache, v_cache)


def _na_window_start(j, rows):
    return int(np.clip(NA_Q_ROWS * j - NA_WIN_H // 2, 0, rows - NA_KEY_ROWS))


def _na_bias_blocks(rpb):
    heads, n_dr, n_dc = rpb.shape
    lead = GRID_W - NA_WIN_W
    padded = jnp.pad(rpb.astype(F32), ((0, 0), (0, 0), (lead, 2 * GRID_W - lead - n_dc)))
    skew = jnp.tile(padded, (1, 1, GRID_W))[..., :GRID_W * (2 * GRID_W - 1)]
    toeplitz = skew.reshape(heads, n_dr, GRID_W, 2 * GRID_W - 1)[..., GRID_W - 1:]
    qc = np.arange(GRID_W)[:, None]
    kc = np.arange(GRID_W)[None, :]
    cs = np.clip(qc - NA_WIN_W // 2, 0, GRID_W - NA_WIN_W)
    col_valid = (kc >= cs) & (kc < cs + NA_WIN_W)
    blocks = jnp.where(col_valid, toeplitz * LOG2_E, MASKED)
    blocks = jnp.concatenate([blocks, jnp.full((heads, 1, GRID_W, GRID_W), MASKED, F32)], axis=1)
    return blocks


def _na_block_index(rows):
    a = np.arange(NA_Q_ROWS)[:, None]
    kk = np.arange(NA_KEY_ROWS)[None, :]
    index = []
    for j in NA_PATTERNS:
        qr = NA_Q_ROWS * j + a
        kr = _na_window_start(j, rows) + kk
        rs = np.clip(qr - NA_WIN_H // 2, 0, rows - NA_WIN_H)
        row_valid = (kr >= rs) & (kr < rs + NA_WIN_H)
        index.append(np.where(row_valid, kr - qr + NA_WIN_H - 1, 2 * NA_WIN_H - 1))
    return np.stack(index)


def _na_kernel(q_ref, k_ref, v_ref, ck_ref, cv_ref, blocks_ref, o_ref, tab_ref, *, index):
    j = pl.program_id(1)
    n_blocks = pl.num_programs(1)

    @pl.when((pl.program_id(0) == 0) & (j == 0))
    def _():
        for p, rows_p in enumerate(index):
            for a, row_a in enumerate(rows_p):
                for kk, blk in enumerate(row_a):
                    tab_ref[p, :, a * GRID_W:(a + 1) * GRID_W,
                            kk * GRID_W:(kk + 1) * GRID_W] = blocks_ref[:, int(blk)]

    pattern = jnp.minimum(j, 2) + jnp.maximum(j - (n_blocks - 3), 0)
    start = jnp.clip(j - 2, 0, n_blocks - NA_KEY_BLOCKS) * NA_Q_BLOCK
    window = pl.ds(pl.multiple_of(start, NA_Q_BLOCK), NA_KEY_BLOCKS * LANES)
    for hp in range(HEAD_PAIRS):
        sl = slice(hp * LANES, (hp + 1) * LANES)
        bias = tab_ref[pattern, 2 * hp:2 * hp + 2].reshape(2 * NA_Q_BLOCK, -1)
        out = _pair_attention(q_ref[:, sl], [k_ref[window, sl], ck_ref[:, sl]],
                              [v_ref[window, sl], cv_ref[:, sl]], [bias, None])
        o_ref[:, sl] = out.astype(o_ref.dtype)


def _na_attention(nq, nk, nv, ck, cv, blocks, layer):
    b, l, _ = nq.shape
    n_blocks = l // NA_Q_BLOCK
    assert [min(j, 2) + max(j - (n_blocks - 3), 0) for j in NA_PATTERNS] == list(range(5))
    tile = pl.BlockSpec((None, NA_Q_BLOCK, NA_W), lambda i, j: (i, j, 0))
    whole = lambda a: pl.BlockSpec((None,) + a.shape[1:], lambda i, j: (i, 0, 0))
    table = (len(NA_PATTERNS), NA_HEADS, NA_Q_BLOCK, NA_KEY_ROWS * GRID_W)
    return pl.pallas_call(
        functools.partial(_na_kernel, index=_na_block_index(l // GRID_W)),
        grid=(b, n_blocks),
        in_specs=[tile, whole(nk), whole(nv), whole(ck), whole(cv),
                  pl.BlockSpec((None,) + blocks.shape[1:], lambda i, j: (layer, 0, 0, 0, 0),
                               pipeline_mode=pl.Buffered(1))],
        out_specs=tile,
        out_shape=jax.ShapeDtypeStruct((b, l, NA_W), BF16),
        scratch_shapes=[pltpu.VMEM(table, F32)],
        compiler_params=_params("arbitrary", "arbitrary"),
        name="na_attention",
    )(nq, nk, nv, ck, cv, blocks)


def _tail_kernel(x_ref, mod_ref, yret_ref, yna_ref, gate_ref, n1_ref, n2_ref, n3_ref,
                 w_ret_ref, w_na_ref, w_o_ref, w_g_ref, w_u_ref, w_d_ref, o_ref):
    mod = mod_ref[0]
    g1 = mod[:, 2 * D_MODEL:3 * D_MODEL]
    sh2 = mod[:, 3 * D_MODEL:4 * D_MODEL]
    sc2 = mod[:, 4 * D_MODEL:5 * D_MODEL]
    g2 = mod[:, 5 * D_MODEL:6 * D_MODEL]
    g_ret = gate_ref[:, 0:D_MODEL].astype(F32)
    g_na = gate_ref[:, D_MODEL:2 * D_MODEL].astype(F32)
    branches = (g_ret * _dot(yret_ref[...], w_ret_ref[...])
                + g_na * _dot(yna_ref[...], w_na_ref[...]))
    mixed = _dot(branches.astype(BF16), w_o_ref[...])
    x = x_ref[...] + g1 * _rms(mixed, n1_ref[...])
    hf = (_rms(x, n2_ref[...]) * (1.0 + sc2) + sh2).astype(BF16)
    act = (_silu(_dot(hf, w_g_ref[...])) * _dot(hf, w_u_ref[...])).astype(BF16)
    o_ref[...] = x + g2 * _rms(_dot(act, w_d_ref[...]), n3_ref[...])


def _tail(x, mod, mod_row, y_ret, y_na, gates, n1, n2, n3, w_ret, w_na, w_o, w_g, w_u, w_d):
    b, l, _ = x.shape
    tm = min(TOKEN_TILE, l)
    tok = lambda w: pl.BlockSpec((None, tm, w), lambda i, j: (i, j, 0))
    vec = lambda a: a.reshape(1, D_MODEL)
    return pl.pallas_call(
        _tail_kernel,
        grid=(b, l // tm),
        in_specs=[
            tok(D_MODEL),
            pl.BlockSpec((1, 1, 6 * D_MODEL), lambda i, j: (mod_row(i), 0, 0)),
            tok(RET_V_W), tok(NA_W), tok(2 * D_MODEL),
            _resident((1, D_MODEL)), _resident((1, D_MODEL)), _resident((1, D_MODEL)),
            _resident(w_ret.shape), _resident(w_na.shape), _resident(w_o.shape),
            _resident(w_g.shape), _resident(w_u.shape), _resident(w_d.shape),
        ],
        out_specs=tok(D_MODEL),
        out_shape=jax.ShapeDtypeStruct((b, l, D_MODEL), F32),
        compiler_params=_params("arbitrary", "arbitrary"),
        name="tail",
    )(x, mod, y_ret, y_na, gates, vec(n1), vec(n2), vec(n3), w_ret, w_na, w_o, w_g, w_u, w_d)


def _rope_tables(l):
    t = jnp.arange(l)
    row = (t // GRID_W).astype(F32)
    col = (t % GRID_W).astype(F32)
    n_freq = RET_DK // 4
    inv = ROPE_BASE ** (-jnp.arange(n_freq, dtype=F32) / n_freq)
    ang = jnp.concatenate([row[:, None] * inv, col[:, None] * inv], axis=-1)
    cos = jnp.cos(ang)
    sin = jnp.sin(ang)
    return jnp.concatenate([cos, cos], axis=-1), jnp.concatenate([-sin, sin], axis=-1)


def kernel(x_prompt, x_sample, cache_na_k, cache_na_v, state_ret, c, c_ctx, w_ada, b_ada,
           norm_pre_mix, norm_post_mix, norm_pre_ffn, norm_post_ffn, w_in, ret_decay_logit,
           ret_gn_gain, na_rpb, w_ret_out, w_na_out, w_gate, w_o, w_ffn_gate, w_ffn_up,
           w_ffn_down):
    dec_b, dec_l, _ = x_sample.shape
    past = cache_na_k.shape[2]
    cvec = jnp.zeros((MOD_ROWS, D_MODEL), F32).at[0].set(c_ctx).at[1:1 + dec_b].set(c)
    mods = _adaln(cvec, w_ada, b_ada).reshape(DEPTH, MOD_ROWS, 1, 6 * D_MODEL)
    log_g = jax.nn.log_sigmoid(ret_decay_logit.astype(F32))
    rope_tabs = _rope_tables(dec_l)
    bf = lambda a: a.astype(BF16)
    layers = lambda w: [bf(w[l]) for l in range(DEPTH)]
    w_in_b, w_gate_b = layers(w_in), layers(w_gate)
    w_ret_b, w_na_b, w_o_b = layers(w_ret_out), layers(w_na_out), layers(w_o)
    w_g_b, w_u_b, w_d_b = layers(w_ffn_gate), layers(w_ffn_up), layers(w_ffn_down)
    ck = [bf(cache_na_k[:, l]).reshape(dec_b, past, NA_W) for l in range(DEPTH)]
    cv = [bf(cache_na_v[:, l]).reshape(dec_b, past, NA_W) for l in range(DEPTH)]
    bias_blocks = _na_bias_blocks(na_rpb.reshape((DEPTH * NA_HEADS,) + na_rpb.shape[2:]))
    bias_blocks = bias_blocks.reshape((DEPTH, NA_HEADS) + bias_blocks.shape[1:])
    ctx_row = lambda i: 0
    dec_row = lambda i: i + 1

    def tail(x, l, row, y_ret, y_na, gates):
        return _tail(x, mods[l], row, y_ret, y_na, gates, norm_post_mix[l], norm_pre_ffn[l],
                     norm_post_ffn[l], w_ret_b[l], w_na_b[l], w_o_b[l], w_g_b[l], w_u_b[l],
                     w_d_b[l])

    ctx_b, ctx_l, _ = x_prompt.shape
    flat = lambda a: a.reshape(1, ctx_b * ctx_l, a.shape[-1])
    per_seq = lambda a: a.reshape(ctx_b, ctx_l, a.shape[-1])
    x = flat(x_prompt)
    new_k = new_v = new_s = None
    for l in range(DEPTH):
        rq, rk, rv, rg, nq, new_k, new_v, gates = _premix(
            x, mods[l], ctx_row, norm_pre_mix[l], w_in_b[l], w_gate_b[l],
            cache=(l, ctx_l, new_k, new_v))
        y_ret, new_s = _retention(log_g[l], per_seq(rq), per_seq(rk), per_seq(rv), per_seq(rg),
                                  ret_gn_gain[l], new_state=(l, new_s))
        y_na = _ctx_attention(per_seq(nq), new_k, new_v, l)
        x = tail(x, l, ctx_row, flat(y_ret), flat(y_na), gates)
    y_prompt = per_seq(x)
    new_k = new_k.reshape(ctx_b, DEPTH, ctx_l, NA_HEADS, NA_DH)
    new_v = new_v.reshape(ctx_b, DEPTH, ctx_l, NA_HEADS, NA_DH)

    x = x_sample
    for l in range(DEPTH):
        rq, rk, rv, rg, nq, nk, nv, gates = _premix(
            x, mods[l], dec_row, norm_pre_mix[l], w_in_b[l], w_gate_b[l], rope_tabs=rope_tabs)
        (y_ret,) = _retention(log_g[l], rq, rk, rv, rg, ret_gn_gain[l], state=state_ret, layer=l)
        y_na = _na_attention(nq, nk, nv, ck[l], cv[l], bias_blocks, l)
        x = tail(x, l, dec_row, y_ret, y_na, gates)
    return (y_prompt, x, new_k, new_v, new_s)
```

```python
import functools
import math

import numpy as np
import jax
import jax.numpy as jnp
from jax import lax
from jax.experimental import pallas as pl
from jax.experimental.pallas import tpu as pltpu

F32 = jnp.float32
BF16 = jnp.bfloat16

D_MODEL = 1024
DEPTH = 2
GRID_W = 64
RET_HEADS = 4
RET_DK = 128
RET_DV = 256
RET_QK_W = RET_HEADS * RET_DK
RET_V_W = RET_HEADS * RET_DV
RET_CHUNK = 128
NA_HEADS = 8
NA_DH = 64
NA_W = NA_HEADS * NA_DH
NA_WIN_H = 8
NA_WIN_W = 16
D_FF = 2816
ROPE_BASE = 10000.0
EPS = 1e-6
IN_WIDTH = 2 * RET_QK_W + 2 * RET_V_W + 3 * NA_W

LANES = 128
HEAD_PAIRS = NA_W // LANES
MOD_ROWS = 16
NA_Q_ROWS = 2
NA_Q_BLOCK = NA_Q_ROWS * GRID_W
NA_KEY_BLOCKS = 5
NA_KEY_ROWS = NA_KEY_BLOCKS * LANES // GRID_W
NA_PATTERNS = (0, 1, 2, 14, 15)
NA_BLOCKS_PER_STEP = 2
MASKED = -1e30
LOG2_E = math.log2(math.e)
NA_Q_SCALE = NA_DH ** -0.5 * LOG2_E
VMEM_LIMIT = 56 * 1024 * 1024
TOKEN_TILE = 512
RET_UNROLL = 16
SHORT_SEQ_TOKENS = 1024


def _sigmoid(x):
    return 1.0 / (1.0 + jnp.exp(-x))


def _silu(x):
    return x * _sigmoid(x)


def _rms(x, g):
    return x * lax.rsqrt(jnp.mean(x * x, axis=-1, keepdims=True) + EPS) * g


def _dot(a, b):
    return jnp.dot(a, b, preferred_element_type=F32)


def _dot_nt(a, b):
    return lax.dot_general(a, b, (((1,), (1,)), ((), ())), preferred_element_type=F32)


def _resident(shape):
    zeros = (0,) * len(shape)
    return pl.BlockSpec(shape, lambda *_: zeros, pipeline_mode=pl.Buffered(1))


def _params(*sem):
    return pltpu.CompilerParams(dimension_semantics=sem, vmem_limit_bytes=VMEM_LIMIT)


def _adaln_kernel(c_ref, w_ref, b_ref, o_ref):
    s = _silu(c_ref[...]).astype(BF16)
    o_ref[0] = _dot(s, w_ref[0].astype(BF16)) + b_ref[0]


def _adaln(cvec, w_ada, b_ada):
    tn = 1536
    n = 6 * D_MODEL
    return pl.pallas_call(
        _adaln_kernel,
        grid=(DEPTH, n // tn),
        in_specs=[
            pl.BlockSpec((MOD_ROWS, D_MODEL), lambda l, j: (0, 0)),
            pl.BlockSpec((1, D_MODEL, tn), lambda l, j: (l, 0, j)),
            pl.BlockSpec((1, 1, tn), lambda l, j: (l, 0, j)),
        ],
        out_specs=pl.BlockSpec((1, MOD_ROWS, tn), lambda l, j: (l, 0, j)),
        out_shape=jax.ShapeDtypeStruct((DEPTH, MOD_ROWS, n), F32),
        compiler_params=_params("arbitrary", "arbitrary"),
        name="adaln",
    )(cvec, w_ada, b_ada.reshape(DEPTH, 1, n))


def _premix_kernel(*refs, rope, cache_layer):
    refs = list(refs)
    x_ref, mod_ref, g_ref, w_in_ref, w_gate_ref = refs[:5]
    pos = 5
    if rope:
        cos_ref, sin_ref = refs[pos:pos + 2]
        pos += 2
    if cache_layer:
        k_prev_ref, v_prev_ref = refs[pos:pos + 2]
        pos += 2
    rq_ref, rk_ref, rv_ref, rg_ref, nq_ref, nk_ref, nv_ref, gate_ref = refs[pos:]
    mod = mod_ref[0]
    sh1 = mod[:, 0:D_MODEL]
    sc1 = mod[:, D_MODEL:2 * D_MODEL]
    hm = (_rms(x_ref[...], g_ref[...]) * (1.0 + sc1) + sh1).astype(BF16)

    def proj(lo, width):
        return _dot(hm, w_in_ref[:, lo:lo + width])

    def rotary(t):
        if not rope:
            return t
        cos = cos_ref[...]
        sin = sin_ref[...]
        heads = []
        for h in range(RET_HEADS):
            blk = t[:, h * RET_DK:(h + 1) * RET_DK]
            heads.append(blk * cos + pltpu.roll(blk, RET_DK // 2, axis=1) * sin)
        return jnp.concatenate(heads, axis=1)

    def store_kv(ref, prev_ref, t):
        if cache_layer is None:
            ref[...] = t.astype(ref.dtype)
            return
        if cache_layer:
            ref[:, 0:cache_layer] = prev_ref[...]
        ref[:, cache_layer] = t.reshape(ref.shape[0], ref.shape[2], ref.shape[3])

    lo = 0
    rq_ref[...] = rotary(proj(lo, RET_QK_W)).astype(rq_ref.dtype)
    lo += RET_QK_W
    rk_ref[...] = (rotary(proj(lo, RET_QK_W)) * (RET_DK ** -0.5)).astype(rk_ref.dtype)
    lo += RET_QK_W
    rv_ref[...] = proj(lo, RET_V_W).astype(rv_ref.dtype)
    lo += RET_V_W
    rg_ref[...] = proj(lo, RET_V_W).astype(rg_ref.dtype)
    lo += RET_V_W
    nq_ref[...] = (proj(lo, NA_W) * NA_Q_SCALE).astype(nq_ref.dtype)
    lo += NA_W
    store_kv(nk_ref, k_prev_ref if cache_layer else None, proj(lo, NA_W))
    lo += NA_W
    store_kv(nv_ref, v_prev_ref if cache_layer else None, proj(lo, NA_W))
    gate_ref[...] = _sigmoid(_dot(hm, w_gate_ref[...])).astype(gate_ref.dtype)


def _premix(x, mod, mod_row, g, w_in, w_gate, rope_tabs=None, cache=None):
    b, l, _ = x.shape
    tm = min(TOKEN_TILE, l)
    tok = lambda w: pl.BlockSpec((None, tm, w), lambda i, j: (i, j, 0))
    in_specs = [
        tok(D_MODEL),
        pl.BlockSpec((1, 1, 6 * D_MODEL), lambda i, j: (mod_row(i), 0, 0)),
        _resident((1, D_MODEL)),
        _resident((D_MODEL, IN_WIDTH)),
        _resident((D_MODEL, 2 * D_MODEL)),
    ]
    args = [x, mod, g.reshape(1, D_MODEL), w_in, w_gate]
    if rope_tabs is not None:
        in_specs += [pl.BlockSpec((tm, RET_DK), lambda i, j: (j, 0))] * 2
        args += list(rope_tabs)
    widths = (RET_QK_W, RET_QK_W, RET_V_W, RET_V_W, NA_W, NA_W, NA_W, 2 * D_MODEL)
    out_specs = [tok(w) for w in widths]
    out_shape = [jax.ShapeDtypeStruct((b, l, w), BF16) for w in widths]
    cache_layer = None
    if cache is not None:
        cache_layer, seq, k_prev, v_prev = cache
        assert b == 1 and tm % seq == 0
        per_tile = tm // seq
        kv_spec = lambda n: pl.BlockSpec((per_tile, n, seq, NA_W), lambda i, j: (j, 0, 0, 0))
        if cache_layer:
            in_specs += [kv_spec(cache_layer)] * 2
            args += [k_prev, v_prev]
        kv_shape = jax.ShapeDtypeStruct((l // seq, cache_layer + 1, seq, NA_W), F32)
        out_specs[5:7] = [kv_spec(cache_layer + 1)] * 2
        out_shape[5:7] = [kv_shape] * 2
    return pl.pallas_call(
        functools.partial(_premix_kernel, rope=rope_tabs is not None, cache_layer=cache_layer),
        grid=(b, l // tm),
        in_specs=in_specs,
        out_specs=out_specs,
        out_shape=out_shape,
        compiler_params=_params("arbitrary", "arbitrary"),
        name="premix",
    )(*args)


def _retention_kernel(*refs, n_chunks, has_s0, state_layer):
    refs = list(refs)
    lg_ref, q_ref, k_ref, v_ref, rg_ref, gain_ref = refs[:6]
    pos = 6
    if has_s0:
        s0_ref = refs[pos]
        pos += 1
    if state_layer:
        sprev_ref = refs[pos]
        pos += 1
    y_ref = refs[pos]
    pos += 1
    if state_layer is not None:
        sout_ref = refs[pos]
        pos += 1
    s_scr = refs[pos]

    c = RET_CHUNK
    h = pl.program_id(1)
    lgf = lg_ref[0, h]
    lgb = lg_ref[1, h]
    row = lax.broadcasted_iota(jnp.int32, (c, c), 0).astype(F32)
    col = lax.broadcasted_iota(jnp.int32, (c, c), 1).astype(F32)
    rel = row - col
    decay = (jnp.where(rel >= 0, jnp.exp(lgf * jnp.maximum(rel, 0.0)), 0.0)
             + jnp.where(rel <= 0, jnp.exp(lgb * jnp.maximum(-rel, 0.0)), 0.0))
    p = lax.broadcasted_iota(jnp.int32, (c, 1), 0).astype(F32)
    xi_f = jnp.exp(lgf * (p + 1.0))
    xi_b = jnp.exp(lgb * (c - p))
    zeta_f = jnp.exp(lgf * (c - 1.0 - p))
    zeta_b = jnp.exp(lgb * p)
    chunk_len = jnp.full((1, RET_DV), float(c), F32)
    g_f = jnp.exp(lgf * chunk_len)
    g_b = jnp.exp(lgb * chunk_len)

    gain = gain_ref[...]
    unroll = min(n_chunks, RET_UNROLL)

    def chunk(ref, seq, i):
        return ref[seq, pl.ds(pl.multiple_of(i * c, c), c), :]

    for seq in range(q_ref.shape[0]):
        def outer_kv(i, zeta):
            kz = (chunk(k_ref, seq, i).astype(F32) * zeta).T.astype(BF16)
            return _dot(kz, chunk(v_ref, seq, i))

        if has_s0:
            s_f0 = s0_ref[seq, 0]
            s_b0 = s0_ref[seq, 1]
        else:
            s_f0 = jnp.zeros((RET_DK, RET_DV), F32)
            s_b0 = s_f0

        def scan(t, carry):
            s_f, s_b = carry
            i_b = n_chunks - 1 - t
            s_scr[seq, t, 0:RET_DK, :] = s_f.astype(BF16)
            s_scr[seq, i_b, RET_DK:2 * RET_DK, :] = s_b.astype(BF16)
            return (g_f * s_f + outer_kv(t, zeta_f), g_b * s_b + outer_kv(i_b, zeta_b))

        s_f, s_b = lax.fori_loop(0, n_chunks, scan, (s_f0, s_b0), unroll=unroll)
        if state_layer is not None:
            if state_layer:
                sout_ref[seq, 0:state_layer] = sprev_ref[seq]
            sout_ref[seq, state_layer, 0] = s_f
            sout_ref[seq, state_layer, 1] = s_b

        def emit(i, _):
            qi = chunk(q_ref, seq, i)
            scores = (_dot_nt(qi, chunk(k_ref, seq, i)) * decay).astype(BF16)
            qf = qi.astype(F32)
            qx = jnp.concatenate([(qf * xi_f).astype(BF16), (qf * xi_b).astype(BF16)], axis=1)
            y = _dot(scores, chunk(v_ref, seq, i)) + _dot(qx, s_scr[seq, i])
            mu = jnp.mean(y, axis=-1, keepdims=True)
            d = y - mu
            var = jnp.mean(d * d, axis=-1, keepdims=True)
            yn = d * lax.rsqrt(var + EPS) * gain
            out = _silu(chunk(rg_ref, seq, i).astype(F32)) * yn
            y_ref[seq, pl.ds(pl.multiple_of(i * c, c), c), :] = out.astype(y_ref.dtype)
            return 0

        lax.fori_loop(0, n_chunks, emit, 0, unroll=unroll)


def _retention(log_g, rq, rk, rv, rg, gain, state=None, layer=0, new_state=None):
    b, l, _ = rq.shape
    n_chunks = l // RET_CHUNK
    seqs = max(1, SHORT_SEQ_TOKENS // l)
    assert b % seqs == 0
    head = lambda w: pl.BlockSpec((seqs, l, w), lambda i, h: (i, 0, h))
    in_specs = [pl.BlockSpec(memory_space=pltpu.SMEM), head(RET_DK), head(RET_DK),
                head(RET_DV), head(RET_DV), pl.BlockSpec((1, RET_DV), lambda i, h: (0, h))]
    args = [log_g, rq, rk, rv, rg, gain.reshape(1, RET_V_W)]
    if state is not None:
        in_specs.append(pl.BlockSpec((seqs, None, 2, None, RET_DK, RET_DV),
                                     lambda i, h: (i, layer, 0, h, 0, 0)))
        args.append(state)
    out_specs = [head(RET_DV)]
    out_shape = [jax.ShapeDtypeStruct((b, l, RET_V_W), BF16)]
    state_layer = None
    if new_state is not None:
        state_layer, earlier = new_state
        state_spec = lambda n: pl.BlockSpec((seqs, n, 2, None, RET_DK, RET_DV),
                                            lambda i, h: (i, 0, 0, h, 0, 0))
        if state_layer:
            in_specs.append(state_spec(state_layer))
            args.append(earlier)
        out_specs.append(state_spec(state_layer + 1))
        out_shape.append(jax.ShapeDtypeStruct(
            (b, state_layer + 1, 2, RET_HEADS, RET_DK, RET_DV), F32))
    return pl.pallas_call(
        functools.partial(_retention_kernel, n_chunks=n_chunks, has_s0=state is not None,
                          state_layer=state_layer),
        grid=(b // seqs, RET_HEADS),
        in_specs=in_specs,
        out_specs=out_specs,
        out_shape=out_shape,
        scratch_shapes=[pltpu.VMEM((seqs, n_chunks, 2 * RET_DK, RET_DV), BF16)],
        compiler_params=_params("arbitrary", "arbitrary"),
        name="retention",
    )(*args)


def _fold_lanes(x, op):
    return functools.reduce(op, [x[:, c:c + LANES] for c in range(0, x.shape[1], LANES)])


def _pair_attention(q2, keys, values, biases):
    n_q = q2.shape[0]
    lane = lax.broadcasted_iota(jnp.int32, (1, LANES), 1)
    lo_mask = lane < NA_DH
    zero = jnp.zeros_like(q2)
    qq = jnp.concatenate([jnp.where(lo_mask, q2, zero), jnp.where(lo_mask, zero, q2)], axis=0)
    scores = []
    for kt, bt in zip(keys, biases):
        s = _dot_nt(qq, kt)
        scores.append(s if bt is None else s + bt)
    m = functools.reduce(jnp.maximum, [_fold_lanes(s, jnp.maximum) for s in scores])
    m = m.max(axis=-1, keepdims=True)
    denom = None
    acc = None
    for s, v in zip(scores, values):
        e = jnp.exp2(s - m)
        part = _fold_lanes(e, jnp.add)
        pv = _dot(e.astype(BF16), v)
        denom = part if denom is None else denom + part
        acc = pv if acc is None else acc + pv
    out = acc * (1.0 / denom.sum(axis=-1, keepdims=True))
    return jnp.where(lo_mask, out[:n_q], out[n_q:])


def _ctx_attn_kernel(q_ref, k_ref, v_ref, o_ref):
    for seq in range(q_ref.shape[0]):
        for hp in range(HEAD_PAIRS):
            sl = slice(hp * LANES, (hp + 1) * LANES)
            out = _pair_attention(q_ref[seq, :, sl], [k_ref[seq, :, sl].astype(BF16)],
                                  [v_ref[seq, :, sl].astype(BF16)], [None])
            o_ref[seq, :, sl] = out.astype(o_ref.dtype)


def _ctx_attention(nq, k_cache, v_cache, layer):
    b, l, _ = nq.shape
    seqs = max(1, SHORT_SEQ_TOKENS // l)
    assert b % seqs == 0
    spec = pl.BlockSpec((seqs, l, NA_W), lambda i: (i, 0, 0))
    kv_spec = pl.BlockSpec((seqs, None, l, NA_W), lambda i: (i, layer, 0, 0))
    return pl.pallas_call(
        _ctx_attn_kernel,
        grid=(b // seqs,),
        in_specs=[spec, kv_spec, kv_spec],
        out_specs=spec,
        out_shape=jax.ShapeDtypeStruct((b, l, NA_W), BF16),
        compiler_params=_params("arbitrary"),
        name="ctx_attention",
    )(nq, k_cache, v_cache)


def _na_window_start(j, rows):
    return int(np.clip(NA_Q_ROWS * j - NA_WIN_H // 2, 0, rows - NA_KEY_ROWS))


def _na_bias_blocks(rpb):
    heads, n_dr, n_dc = rpb.shape
    lead = GRID_W - NA_WIN_W
    padded = jnp.pad(rpb.astype(F32), ((0, 0), (0, 0), (lead, 2 * GRID_W - lead - n_dc)))
    skew = jnp.tile(padded, (1, 1, GRID_W))[..., :GRID_W * (2 * GRID_W - 1)]
    toeplitz = skew.reshape(heads, n_dr, GRID_W, 2 * GRID_W - 1)[..., GRID_W - 1:]
    qc = np.arange(GRID_W)[:, None]
    kc = np.arange(GRID_W)[None, :]
    cs = np.clip(qc - NA_WIN_W // 2, 0, GRID_W - NA_WIN_W)
    col_valid = (kc >= cs) & (kc < cs + NA_WIN_W)
    blocks = jnp.where(col_valid, toeplitz * LOG2_E, MASKED)
    blocks = jnp.concatenate([blocks, jnp.full((heads, 1, GRID_W, GRID_W), MASKED, F32)], axis=1)
    return blocks


def _na_block_index(rows):
    a = np.arange(NA_Q_ROWS)[:, None]
    kk = np.arange(NA_KEY_ROWS)[None, :]
    index = []
    for j in NA_PATTERNS:
        qr = NA_Q_ROWS * j + a
        kr = _na_window_start(j, rows) + kk
        rs = np.clip(qr - NA_WIN_H // 2, 0, rows - NA_WIN_H)
        row_valid = (kr >= rs) & (kr < rs + NA_WIN_H)
        index.append(np.where(row_valid, kr - qr + NA_WIN_H - 1, 2 * NA_WIN_H - 1))
    return np.stack(index)


def _na_kernel(q_ref, k_ref, v_ref, ck_ref, cv_ref, blocks_ref, o_ref, tab_ref, *, index):
    step = pl.program_id(1)
    n_blocks = pl.num_programs(1) * NA_BLOCKS_PER_STEP

    @pl.when((pl.program_id(0) == 0) & (step == 0))
    def _():
        for p, rows_p in enumerate(index):
            for a, row_a in enumerate(rows_p):
                for kk, blk in enumerate(row_a):
                    tab_ref[p, :, a * GRID_W:(a + 1) * GRID_W,
                            kk * GRID_W:(kk + 1) * GRID_W] = blocks_ref[:, int(blk)]

    for sub in range(NA_BLOCKS_PER_STEP):
        j = step * NA_BLOCKS_PER_STEP + sub
        rows = slice(sub * NA_Q_BLOCK, (sub + 1) * NA_Q_BLOCK)
        pattern = jnp.minimum(j, 2) + jnp.maximum(j - (n_blocks - 3), 0)
        start = jnp.clip(j - 2, 0, n_blocks - NA_KEY_BLOCKS) * NA_Q_BLOCK
        window = pl.ds(pl.multiple_of(start, NA_Q_BLOCK), NA_KEY_BLOCKS * LANES)
        for hp in range(HEAD_PAIRS):
            sl = slice(hp * LANES, (hp + 1) * LANES)
            bias = tab_ref[pattern, 2 * hp:2 * hp + 2].reshape(2 * NA_Q_BLOCK, -1)
            out = _pair_attention(q_ref[rows, sl], [k_ref[window, sl], ck_ref[:, sl]],
                                  [v_ref[window, sl], cv_ref[:, sl]], [bias, None])
            o_ref[rows, sl] = out.astype(o_ref.dtype)


def _na_attention(nq, nk, nv, ck, cv, blocks, layer):
    b, l, _ = nq.shape
    n_blocks = l // NA_Q_BLOCK
    assert [min(j, 2) + max(j - (n_blocks - 3), 0) for j in NA_PATTERNS] == list(range(5))
    assert n_blocks % NA_BLOCKS_PER_STEP == 0
    tile = pl.BlockSpec((None, NA_BLOCKS_PER_STEP * NA_Q_BLOCK, NA_W), lambda i, j: (i, j, 0))
    whole = lambda a: pl.BlockSpec((None,) + a.shape[1:], lambda i, j: (i, 0, 0))
    table = (len(NA_PATTERNS), NA_HEADS, NA_Q_BLOCK, NA_KEY_ROWS * GRID_W)
    return pl.pallas_call(
        functools.partial(_na_kernel, index=_na_block_index(l // GRID_W)),
        grid=(b, n_blocks // NA_BLOCKS_PER_STEP),
        in_specs=[tile, whole(nk), whole(nv), whole(ck), whole(cv),
                  pl.BlockSpec((None,) + blocks.shape[1:], lambda i, j: (layer, 0, 0, 0, 0),
                               pipeline_mode=pl.Buffered(1))],
        out_specs=tile,
        out_shape=jax.ShapeDtypeStruct((b, l, NA_W), BF16),
        scratch_shapes=[pltpu.VMEM(table, F32)],
        compiler_params=_params("arbitrary", "arbitrary"),
        name="na_attention",
    )(nq, nk, nv, ck, cv, blocks)


def _tail_kernel(x_ref, mod_ref, yret_ref, yna_ref, gate_ref, n1_ref, n2_ref, n3_ref,
                 w_ret_ref, w_na_ref, w_o_ref, w_g_ref, w_u_ref, w_d_ref, o_ref):
    mod = mod_ref[0]
    g1 = mod[:, 2 * D_MODEL:3 * D_MODEL]
    sh2 = mod[:, 3 * D_MODEL:4 * D_MODEL]
    sc2 = mod[:, 4 * D_MODEL:5 * D_MODEL]
    g2 = mod[:, 5 * D_MODEL:6 * D_MODEL]
    g_ret = gate_ref[:, 0:D_MODEL].astype(F32)
    g_na = gate_ref[:, D_MODEL:2 * D_MODEL].astype(F32)
    branches = (g_ret * _dot(yret_ref[...], w_ret_ref[...])
                + g_na * _dot(yna_ref[...], w_na_ref[...]))
    mixed = _dot(branches.astype(BF16), w_o_ref[...])
    x = x_ref[...] + g1 * _rms(mixed, n1_ref[...])
    hf = (_rms(x, n2_ref[...]) * (1.0 + sc2) + sh2).astype(BF16)
    act = (_silu(_dot(hf, w_g_ref[...])) * _dot(hf, w_u_ref[...])).astype(BF16)
    o_ref[...] = x + g2 * _rms(_dot(act, w_d_ref[...]), n3_ref[...])


def _tail(x, mod, mod_row, y_ret, y_na, gates, n1, n2, n3, w_ret, w_na, w_o, w_g, w_u, w_d):
    b, l, _ = x.shape
    tm = min(TOKEN_TILE, l)
    tok = lambda w: pl.BlockSpec((None, tm, w), lambda i, j: (i, j, 0))
    vec = lambda a: a.reshape(1, D_MODEL)
    return pl.pallas_call(
        _tail_kernel,
        grid=(b, l // tm),
        in_specs=[
            tok(D_MODEL),
            pl.BlockSpec((1, 1, 6 * D_MODEL), lambda i, j: (mod_row(i), 0, 0)),
            tok(RET_V_W), tok(NA_W), tok(2 * D_MODEL),
            _resident((1, D_MODEL)), _resident((1, D_MODEL)), _resident((1, D_MODEL)),
            _resident(w_ret.shape), _resident(w_na.shape), _resident(w_o.shape),
            _resident(w_g.shape), _resident(w_u.shape), _resident(w_d.shape),
        ],
        out_specs=tok(D_MODEL),
        out_shape=jax.ShapeDtypeStruct((b, l, D_MODEL), F32),
        compiler_params=_params("arbitrary", "arbitrary"),
        name="tail",
    )(x, mod, y_ret, y_na, gates, vec(n1), vec(n2), vec(n3), w_ret, w_na, w_o, w_g, w_u, w_d)


def _rope_tables(l):
    t = jnp.arange(l)
    row = (t // GRID_W).astype(F32)
    col = (t % GRID_W).astype(F32)
    n_freq = RET_DK // 4
    inv = ROPE_BASE ** (-jnp.arange(n_freq, dtype=F32) / n_freq)
    ang = jnp.concatenate([row[:, None] * inv, col[:, None] * inv], axis=-1)
    cos = jnp.cos(ang)
    sin = jnp.sin(ang)
    return jnp.concatenate([cos, cos], axis=-1), jnp.concatenate([-sin, sin], axis=-1)


def kernel(x_prompt, x_sample, cache_na_k, cache_na_v, state_ret, c, c_ctx, w_ada, b_ada,
           norm_pre_mix, norm_post_mix, norm_pre_ffn, norm_post_ffn, w_in, ret_decay_logit,
           ret_gn_gain, na_rpb, w_ret_out, w_na_out, w_gate, w_o, w_ffn_gate, w_ffn_up,
           w_ffn_down):
    dec_b, dec_l, _ = x_sample.shape
    past = cache_na_k.shape[2]
    cvec = jnp.zeros((MOD_ROWS, D_MODEL), F32).at[0].set(c_ctx).at[1:1 + dec_b].set(c)
    mods = _adaln(cvec, w_ada, b_ada).reshape(DEPTH, MOD_ROWS, 1, 6 * D_MODEL)
    log_g = jax.nn.log_sigmoid(ret_decay_logit.astype(F32))
    rope_tabs = _rope_tables(dec_l)
    bf = lambda a: a.astype(BF16)
    layers = lambda w: [bf(w[l]) for l in range(DEPTH)]
    w_in_b, w_gate_b = layers(w_in), layers(w_gate)
    w_ret_b, w_na_b, w_o_b = layers(w_ret_out), layers(w_na_out), layers(w_o)
    w_g_b, w_u_b, w_d_b = layers(w_ffn_gate), layers(w_ffn_up), layers(w_ffn_down)
    ck = [bf(cache_na_k[:, l]).reshape(dec_b, past, NA_W) for l in range(DEPTH)]
    cv = [bf(cache_na_v[:, l]).reshape(dec_b, past, NA_W) for l in range(DEPTH)]
    bias_blocks = _na_bias_blocks(na_rpb.reshape((DEPTH * NA_HEADS,) + na_rpb.shape[2:]))
    bias_blocks = bias_blocks.reshape((DEPTH, NA_HEADS) + bias_blocks.shape[1:])
    ctx_row = lambda i: 0
    dec_row = lambda i: i + 1

    def tail(x, l, row, y_ret, y_na, gates):
        return _tail(x, mods[l], row, y_ret, y_na, gates, norm_post_mix[l], norm_pre_ffn[l],
                     norm_post_ffn[l], w_ret_b[l], w_na_b[l], w_o_b[l], w_g_b[l], w_u_b[l],
                     w_d_b[l])

    ctx_b, ctx_l, _ = x_prompt.shape
    flat = lambda a: a.reshape(1, ctx_b * ctx_l, a.shape[-1])
    per_seq = lambda a: a.reshape(ctx_b, ctx_l, a.shape[-1])
    x = flat(x_prompt)
    new_k = new_v = new_s = None
    for l in range(DEPTH):
        rq, rk, rv, rg, nq, new_k, new_v, gates = _premix(
            x, mods[l], ctx_row, norm_pre_mix[l], w_in_b[l], w_gate_b[l],
            cache=(l, ctx_l, new_k, new_v))
        y_ret, new_s = _retention(log_g[l], per_seq(rq), per_seq(rk), per_seq(rv), per_seq(rg),
                                  ret_gn_gain[l], new_state=(l, new_s))
        y_na = _ctx_attention(per_seq(nq), new_k, new_v, l)
        x = tail(x, l, ctx_row, flat(y_ret), flat(y_na), gates)
    y_prompt = per_seq(x)
    new_k = new_k.reshape(ctx_b, DEPTH, ctx_l, NA_HEADS, NA_DH)
    new_v = new_v.reshape(ctx_b, DEPTH, ctx_l, NA_HEADS, NA_DH)

    x = x_sample
    for l in range(DEPTH):
        rq, rk, rv, rg, nq, nk, nv, gates = _premix(
            x, mods[l], dec_row, norm_pre_mix[l], w_in_b[l], w_gate_b[l], rope_tabs=rope_tabs)
        (y_ret,) = _retention(log_g[l], rq, rk, rv, rg, ret_gn_gain[l], state=state_ret, layer=l)
        y_na = _na_attention(nq, nk, nv, ck[l], cv[l], bias_blocks, l)
        x = tail(x, l, dec_row, y_ret, y_na, gates)
    return (y_prompt, x, new_k, new_v, new_s)
```

```python
import functools
import math

import numpy as np
import jax
import jax.numpy as jnp
from jax import lax
from jax.experimental import pallas as pl
from jax.experimental.pallas import tpu as pltpu

F32 = jnp.float32
BF16 = jnp.bfloat16

D_MODEL = 1024
DEPTH = 2
GRID_W = 64
RET_HEADS = 4
RET_DK = 128
RET_DV = 256
RET_QK_W = RET_HEADS * RET_DK
RET_V_W = RET_HEADS * RET_DV
RET_CHUNK = 128
NA_HEADS = 8
NA_DH = 64
NA_W = NA_HEADS * NA_DH
NA_WIN_H = 8
NA_WIN_W = 16
D_FF = 2816
ROPE_BASE = 10000.0
EPS = 1e-6
IN_WIDTH = 2 * RET_QK_W + 2 * RET_V_W + 3 * NA_W

LANES = 128
HEAD_PAIRS = NA_W // LANES
MOD_ROWS = 16
NA_Q_ROWS = 2
NA_Q_BLOCK = NA_Q_ROWS * GRID_W
NA_KEY_BLOCKS = 5
NA_KEY_ROWS = NA_KEY_BLOCKS * LANES // GRID_W
NA_PATTERNS = (0, 1, 2, 14, 15)
NA_BLOCKS_PER_STEP = 4
MASKED = -1e30
LOG2_E = math.log2(math.e)
NA_Q_SCALE = NA_DH ** -0.5 * LOG2_E
VMEM_LIMIT = 56 * 1024 * 1024
TOKEN_TILE = 512
RET_UNROLL = 16
SHORT_SEQ_TOKENS = 2048


def _sigmoid(x):
    return 1.0 / (1.0 + jnp.exp(-x))


def _silu(x):
    return x * _sigmoid(x)


def _rms(x, g):
    return x * lax.rsqrt(jnp.mean(x * x, axis=-1, keepdims=True) + EPS) * g


def _dot(a, b):
    return jnp.dot(a, b, preferred_element_type=F32)


def _dot_nt(a, b):
    return lax.dot_general(a, b, (((1,), (1,)), ((), ())), preferred_element_type=F32)


def _resident(shape):
    zeros = (0,) * len(shape)
    return pl.BlockSpec(shape, lambda *_: zeros, pipeline_mode=pl.Buffered(1))


def _params(*sem):
    return pltpu.CompilerParams(dimension_semantics=sem, vmem_limit_bytes=VMEM_LIMIT)


def _adaln_kernel(c_ref, w_ref, b_ref, o_ref):
    s = _silu(c_ref[...]).astype(BF16)
    o_ref[0] = _dot(s, w_ref[0].astype(BF16)) + b_ref[0]


def _adaln(cvec, w_ada, b_ada):
    tn = 1536
    n = 6 * D_MODEL
    return pl.pallas_call(
        _adaln_kernel,
        grid=(DEPTH, n // tn),
        in_specs=[
            pl.BlockSpec((MOD_ROWS, D_MODEL), lambda l, j: (0, 0)),
            pl.BlockSpec((1, D_MODEL, tn), lambda l, j: (l, 0, j)),
            pl.BlockSpec((1, 1, tn), lambda l, j: (l, 0, j)),
        ],
        out_specs=pl.BlockSpec((1, MOD_ROWS, tn), lambda l, j: (l, 0, j)),
        out_shape=jax.ShapeDtypeStruct((DEPTH, MOD_ROWS, n), F32),
        compiler_params=_params("arbitrary", "arbitrary"),
        name="adaln",
    )(cvec, w_ada, b_ada.reshape(DEPTH, 1, n))


def _premix_kernel(*refs, rope, cache_layer):
    refs = list(refs)
    x_ref, mod_ref, g_ref, w_in_ref, w_gate_ref = refs[:5]
    pos = 5
    if rope:
        cos_ref, sin_ref = refs[pos:pos + 2]
        pos += 2
    if cache_layer:
        k_prev_ref, v_prev_ref = refs[pos:pos + 2]
        pos += 2
    rq_ref, rk_ref, rv_ref, rg_ref, nq_ref, nk_ref, nv_ref, gate_ref = refs[pos:]
    mod = mod_ref[0]
    sh1 = mod[:, 0:D_MODEL]
    sc1 = mod[:, D_MODEL:2 * D_MODEL]
    hm = (_rms(x_ref[...], g_ref[...]) * (1.0 + sc1) + sh1).astype(BF16)

    def proj(lo, width):
        return _dot(hm, w_in_ref[:, lo:lo + width])

    def rotary(t):
        if not rope:
            return t
        cos = cos_ref[...]
        sin = sin_ref[...]
        heads = []
        for h in range(RET_HEADS):
            blk = t[:, h * RET_DK:(h + 1) * RET_DK]
            heads.append(blk * cos + pltpu.roll(blk, RET_DK // 2, axis=1) * sin)
        return jnp.concatenate(heads, axis=1)

    def store_kv(ref, prev_ref, t):
        if cache_layer is None:
            ref[...] = t.astype(ref.dtype)
            return
        if cache_layer:
            ref[:, 0:cache_layer] = prev_ref[...]
        ref[:, cache_layer] = t.reshape(ref.shape[0], ref.shape[2], ref.shape[3])

    lo = 0
    rq_ref[...] = rotary(proj(lo, RET_QK_W)).astype(rq_ref.dtype)
    lo += RET_QK_W
    rk_ref[...] = (rotary(proj(lo, RET_QK_W)) * (RET_DK ** -0.5)).astype(rk_ref.dtype)
    lo += RET_QK_W
    rv_ref[...] = proj(lo, RET_V_W).astype(rv_ref.dtype)
    lo += RET_V_W
    rg_ref[...] = proj(lo, RET_V_W).astype(rg_ref.dtype)
    lo += RET_V_W
    nq_ref[...] = (proj(lo, NA_W) * NA_Q_SCALE).astype(nq_ref.dtype)
    lo += NA_W
    store_kv(nk_ref, k_prev_ref if cache_layer else None, proj(lo, NA_W))
    lo += NA_W
    store_kv(nv_ref, v_prev_ref if cache_layer else None, proj(lo, NA_W))
    gate_ref[...] = _sigmoid(_dot(hm, w_gate_ref[...])).astype(gate_ref.dtype)


def _premix(x, mod, mod_row, g, w_in, w_gate, rope_tabs=None, cache=None):
    b, l, _ = x.shape
    tm = min(TOKEN_TILE, l)
    tok = lambda w: pl.BlockSpec((None, tm, w), lambda i, j: (i, j, 0))
    in_specs = [
        tok(D_MODEL),
        pl.BlockSpec((1, 1, 6 * D_MODEL), lambda i, j: (mod_row(i), 0, 0)),
        _resident((1, D_MODEL)),
        _resident((D_MODEL, IN_WIDTH)),
        _resident((D_MODEL, 2 * D_MODEL)),
    ]
    args = [x, mod, g.reshape(1, D_MODEL), w_in, w_gate]
    if rope_tabs is not None:
        in_specs += [pl.BlockSpec((tm, RET_DK), lambda i, j: (j, 0))] * 2
        args += list(rope_tabs)
    widths = (RET_QK_W, RET_QK_W, RET_V_W, RET_V_W, NA_W, NA_W, NA_W, 2 * D_MODEL)
    out_specs = [tok(w) for w in widths]
    out_shape = [jax.ShapeDtypeStruct((b, l, w), BF16) for w in widths]
    cache_layer = None
    if cache is not None:
        cache_layer, seq, k_prev, v_prev = cache
        assert b == 1 and tm % seq == 0
        per_tile = tm // seq
        kv_spec = lambda n: pl.BlockSpec((per_tile, n, seq, NA_W), lambda i, j: (j, 0, 0, 0))
        if cache_layer:
            in_specs += [kv_spec(cache_layer)] * 2
            args += [k_prev, v_prev]
        kv_shape = jax.ShapeDtypeStruct((l // seq, cache_layer + 1, seq, NA_W), F32)
        out_specs[5:7] = [kv_spec(cache_layer + 1)] * 2
        out_shape[5:7] = [kv_shape] * 2
    return pl.pallas_call(
        functools.partial(_premix_kernel, rope=rope_tabs is not None, cache_layer=cache_layer),
        grid=(b, l // tm),
        in_specs=in_specs,
        out_specs=out_specs,
        out_shape=out_shape,
        compiler_params=_params("arbitrary", "arbitrary"),
        name="premix",
    )(*args)


def _retention_kernel(*refs, n_chunks, has_s0, state_layer):
    refs = list(refs)
    lg_ref, q_ref, k_ref, v_ref, rg_ref, gain_ref = refs[:6]
    pos = 6
    if has_s0:
        s0_ref = refs[pos]
        pos += 1
    if state_layer:
        sprev_ref = refs[pos]
        pos += 1
    y_ref = refs[pos]
    pos += 1
    if state_layer is not None:
        sout_ref = refs[pos]
        pos += 1
    s_scr = refs[pos]

    c = RET_CHUNK
    h = pl.program_id(1)
    lgf = lg_ref[0, h]
    lgb = lg_ref[1, h]
    row = lax.broadcasted_iota(jnp.int32, (c, c), 0).astype(F32)
    col = lax.broadcasted_iota(jnp.int32, (c, c), 1).astype(F32)
    rel = row - col
    decay = (jnp.where(rel >= 0, jnp.exp(lgf * jnp.maximum(rel, 0.0)), 0.0)
             + jnp.where(rel <= 0, jnp.exp(lgb * jnp.maximum(-rel, 0.0)), 0.0))
    p = lax.broadcasted_iota(jnp.int32, (c, 1), 0).astype(F32)
    xi_f = jnp.exp(lgf * (p + 1.0))
    xi_b = jnp.exp(lgb * (c - p))
    zeta_f = jnp.exp(lgf * (c - 1.0 - p))
    zeta_b = jnp.exp(lgb * p)
    chunk_len = jnp.full((1, RET_DV), float(c), F32)
    g_f = jnp.exp(lgf * chunk_len)
    g_b = jnp.exp(lgb * chunk_len)

    gain = gain_ref[...]
    unroll = min(n_chunks, RET_UNROLL)

    def chunk(ref, seq, i):
        return ref[seq, pl.ds(pl.multiple_of(i * c, c), c), :]

    for seq in range(q_ref.shape[0]):
        def outer_kv(i, zeta):
            kz = (chunk(k_ref, seq, i).astype(F32) * zeta).T.astype(BF16)
            return _dot(kz, chunk(v_ref, seq, i))

        if has_s0:
            s_f0 = s0_ref[seq, 0]
            s_b0 = s0_ref[seq, 1]
        else:
            s_f0 = jnp.zeros((RET_DK, RET_DV), F32)
            s_b0 = s_f0

        def scan(t, carry):
            s_f, s_b = carry
            i_b = n_chunks - 1 - t
            s_scr[seq, t, 0:RET_DK, :] = s_f.astype(BF16)
            s_scr[seq, i_b, RET_DK:2 * RET_DK, :] = s_b.astype(BF16)
            return (g_f * s_f + outer_kv(t, zeta_f), g_b * s_b + outer_kv(i_b, zeta_b))

        s_f, s_b = lax.fori_loop(0, n_chunks, scan, (s_f0, s_b0), unroll=unroll)
        if state_layer is not None:
            if state_layer:
                sout_ref[seq, 0:state_layer] = sprev_ref[seq]
            sout_ref[seq, state_layer, 0] = s_f
            sout_ref[seq, state_layer, 1] = s_b

        def emit(i, _):
            qi = chunk(q_ref, seq, i)
            scores = (_dot_nt(qi, chunk(k_ref, seq, i)) * decay).astype(BF16)
            qf = qi.astype(F32)
            qx = jnp.concatenate([(qf * xi_f).astype(BF16), (qf * xi_b).astype(BF16)], axis=1)
            y = _dot(scores, chunk(v_ref, seq, i)) + _dot(qx, s_scr[seq, i])
            mu = jnp.mean(y, axis=-1, keepdims=True)
            d = y - mu
            var = jnp.mean(d * d, axis=-1, keepdims=True)
            yn = d * lax.rsqrt(var + EPS) * gain
            out = _silu(chunk(rg_ref, seq, i).astype(F32)) * yn
            y_ref[seq, pl.ds(pl.multiple_of(i * c, c), c), :] = out.astype(y_ref.dtype)
            return 0

        lax.fori_loop(0, n_chunks, emit, 0, unroll=unroll)


def _retention(log_g, rq, rk, rv, rg, gain, state=None, layer=0, new_state=None):
    b, l, _ = rq.shape
    n_chunks = l // RET_CHUNK
    seqs = max(1, SHORT_SEQ_TOKENS // l)
    assert b % seqs == 0
    head = lambda w: pl.BlockSpec((seqs, l, w), lambda i, h: (i, 0, h))
    in_specs = [pl.BlockSpec(memory_space=pltpu.SMEM), head(RET_DK), head(RET_DK),
                head(RET_DV), head(RET_DV), pl.BlockSpec((1, RET_DV), lambda i, h: (0, h))]
    args = [log_g, rq, rk, rv, rg, gain.reshape(1, RET_V_W)]
    if state is not None:
        in_specs.append(pl.BlockSpec((seqs, None, 2, None, RET_DK, RET_DV),
                                     lambda i, h: (i, layer, 0, h, 0, 0)))
        args.append(state)
    out_specs = [head(RET_DV)]
    out_shape = [jax.ShapeDtypeStruct((b, l, RET_V_W), BF16)]
    state_layer = None
    if new_state is not None:
        state_layer, earlier = new_state
        state_spec = lambda n: pl.BlockSpec((seqs, n, 2, None, RET_DK, RET_DV),
                                            lambda i, h: (i, 0, 0, h, 0, 0))
        if state_layer:
            in_specs.append(state_spec(state_layer))
            args.append(earlier)
        out_specs.append(state_spec(state_layer + 1))
        out_shape.append(jax.ShapeDtypeStruct(
            (b, state_layer + 1, 2, RET_HEADS, RET_DK, RET_DV), F32))
    return pl.pallas_call(
        functools.partial(_retention_kernel, n_chunks=n_chunks, has_s0=state is not None,
                          state_layer=state_layer),
        grid=(b // seqs, RET_HEADS),
        in_specs=in_specs,
        out_specs=out_specs,
        out_shape=out_shape,
        scratch_shapes=[pltpu.VMEM((seqs, n_chunks, 2 * RET_DK, RET_DV), BF16)],
        compiler_params=_params("arbitrary", "arbitrary"),
        name="retention",
    )(*args)


def _fold_lanes(x, op):
    return functools.reduce(op, [x[:, c:c + LANES] for c in range(0, x.shape[1], LANES)])


def _pair_attention(q2, keys, values, biases):
    n_q = q2.shape[0]
    lane = lax.broadcasted_iota(jnp.int32, (1, LANES), 1)
    lo_mask = lane < NA_DH
    zero = jnp.zeros_like(q2)
    qq = jnp.concatenate([jnp.where(lo_mask, q2, zero), jnp.where(lo_mask, zero, q2)], axis=0)
    scores = []
    for kt, bt in zip(keys, biases):
        s = _dot_nt(qq, kt)
        scores.append(s if bt is None else s + bt)
    m = functools.reduce(jnp.maximum, [_fold_lanes(s, jnp.maximum) for s in scores])
    m = m.max(axis=-1, keepdims=True)
    denom = None
    acc = None
    for s, v in zip(scores, values):
        e = jnp.exp2(s - m)
        part = _fold_lanes(e, jnp.add)
        pv = _dot(e.astype(BF16), v)
        denom = part if denom is None else denom + part
        acc = pv if acc is None else acc + pv
    out = acc * (1.0 / denom.sum(axis=-1, keepdims=True))
    return jnp.where(lo_mask, out[:n_q], out[n_q:])


def _ctx_attn_kernel(q_ref, k_ref, v_ref, o_ref):
    for seq in range(q_ref.shape[0]):
        for hp in range(HEAD_PAIRS):
            sl = slice(hp * LANES, (hp + 1) * LANES)
            out = _pair_attention(q_ref[seq, :, sl], [k_ref[seq, :, sl].astype(BF16)],
                                  [v_ref[seq, :, sl].astype(BF16)], [None])
            o_ref[seq, :, sl] = out.astype(o_ref.dtype)


def _ctx_attention(nq, k_cache, v_cache, layer):
    b, l, _ = nq.shape
    seqs = max(1, SHORT_SEQ_TOKENS // l)
    assert b % seqs == 0
    spec = pl.BlockSpec((seqs, l, NA_W), lambda i: (i, 0, 0))
    kv_spec = pl.BlockSpec((seqs, None, l, NA_W), lambda i: (i, layer, 0, 0))
    return pl.pallas_call(
        _ctx_attn_kernel,
        grid=(b // seqs,),
        in_specs=[spec, kv_spec, kv_spec],
        out_specs=spec,
        out_shape=jax.ShapeDtypeStruct((b, l, NA_W), BF16),
        compiler_params=_params("arbitrary"),
        name="ctx_attention",
    )(nq, k_cache, v_cache)


def _na_window_start(j, rows):
    return int(np.clip(NA_Q_ROWS * j - NA_WIN_H // 2, 0, rows - NA_KEY_ROWS))


def _na_bias_blocks(rpb):
    heads, n_dr, n_dc = rpb.shape
    lead = GRID_W - NA_WIN_W
    padded = jnp.pad(rpb.astype(F32), ((0, 0), (0, 0), (lead, 2 * GRID_W - lead - n_dc)))
    skew = jnp.tile(padded, (1, 1, GRID_W))[..., :GRID_W * (2 * GRID_W - 1)]
    toeplitz = skew.reshape(heads, n_dr, GRID_W, 2 * GRID_W - 1)[..., GRID_W - 1:]
    qc = np.arange(GRID_W)[:, None]
    kc = np.arange(GRID_W)[None, :]
    cs = np.clip(qc - NA_WIN_W // 2, 0, GRID_W - NA_WIN_W)
    col_valid = (kc >= cs) & (kc < cs + NA_WIN_W)
    blocks = jnp.where(col_valid, toeplitz * LOG2_E, MASKED)
    blocks = jnp.concatenate([blocks, jnp.full((heads, 1, GRID_W, GRID_W), MASKED, F32)], axis=1)
    return blocks


def _na_block_index(rows):
    a = np.arange(NA_Q_ROWS)[:, None]
    kk = np.arange(NA_KEY_ROWS)[None, :]
    index = []
    for j in NA_PATTERNS:
        qr = NA_Q_ROWS * j + a
        kr = _na_window_start(j, rows) + kk
        rs = np.clip(qr - NA_WIN_H // 2, 0, rows - NA_WIN_H)
        row_valid = (kr >= rs) & (kr < rs + NA_WIN_H)
        index.append(np.where(row_valid, kr - qr + NA_WIN_H - 1, 2 * NA_WIN_H - 1))
    return np.stack(index)


def _na_kernel(q_ref, k_ref, v_ref, ck_ref, cv_ref, blocks_ref, o_ref, tab_ref, *, index):
    step = pl.program_id(1)
    n_blocks = pl.num_programs(1) * NA_BLOCKS_PER_STEP

    @pl.when((pl.program_id(0) == 0) & (step == 0))
    def _():
        for p, rows_p in enumerate(index):
            for a, row_a in enumerate(rows_p):
                for kk, blk in enumerate(row_a):
                    tab_ref[p, :, a * GRID_W:(a + 1) * GRID_W,
                            kk * GRID_W:(kk + 1) * GRID_W] = blocks_ref[:, int(blk)]

    for sub in range(NA_BLOCKS_PER_STEP):
        j = step * NA_BLOCKS_PER_STEP + sub
        rows = slice(sub * NA_Q_BLOCK, (sub + 1) * NA_Q_BLOCK)
        pattern = jnp.minimum(j, 2) + jnp.maximum(j - (n_blocks - 3), 0)
        start = jnp.clip(j - 2, 0, n_blocks - NA_KEY_BLOCKS) * NA_Q_BLOCK
        window = pl.ds(pl.multiple_of(start, NA_Q_BLOCK), NA_KEY_BLOCKS * LANES)
        for hp in range(HEAD_PAIRS):
            sl = slice(hp * LANES, (hp + 1) * LANES)
            bias = tab_ref[pattern, 2 * hp:2 * hp + 2].reshape(2 * NA_Q_BLOCK, -1)
            out = _pair_attention(q_ref[rows, sl], [k_ref[window, sl], ck_ref[:, sl]],
                                  [v_ref[window, sl], cv_ref[:, sl]], [bias, None])
            o_ref[rows, sl] = out.astype(o_ref.dtype)


def _na_attention(nq, nk, nv, ck, cv, blocks, layer):
    b, l, _ = nq.shape
    n_blocks = l // NA_Q_BLOCK
    assert [min(j, 2) + max(j - (n_blocks - 3), 0) for j in NA_PATTERNS] == list(range(5))
    assert n_blocks % NA_BLOCKS_PER_STEP == 0
    tile = pl.BlockSpec((None, NA_BLOCKS_PER_STEP * NA_Q_BLOCK, NA_W), lambda i, j: (i, j, 0))
    whole = lambda a: pl.BlockSpec((None,) + a.shape[1:], lambda i, j: (i, 0, 0))
    table = (len(NA_PATTERNS), NA_HEADS, NA_Q_BLOCK, NA_KEY_ROWS * GRID_W)
    return pl.pallas_call(
        functools.partial(_na_kernel, index=_na_block_index(l // GRID_W)),
        grid=(b, n_blocks // NA_BLOCKS_PER_STEP),
        in_specs=[tile, whole(nk), whole(nv), whole(ck), whole(cv),
                  pl.BlockSpec((None,) + blocks.shape[1:], lambda i, j: (layer, 0, 0, 0, 0),
                               pipeline_mode=pl.Buffered(1))],
        out_specs=tile,
        out_shape=jax.ShapeDtypeStruct((b, l, NA_W), BF16),
        scratch_shapes=[pltpu.VMEM(table, F32)],
        compiler_params=_params("arbitrary", "arbitrary"),
        name="na_attention",
    )(nq, nk, nv, ck, cv, blocks)


def _tail_kernel(x_ref, mod_ref, yret_ref, yna_ref, gate_ref, n1_ref, n2_ref, n3_ref,
                 w_ret_ref, w_na_ref, w_o_ref, w_g_ref, w_u_ref, w_d_ref, o_ref):
    mod = mod_ref[0]
    g1 = mod[:, 2 * D_MODEL:3 * D_MODEL]
    sh2 = mod[:, 3 * D_MODEL:4 * D_MODEL]
    sc2 = mod[:, 4 * D_MODEL:5 * D_MODEL]
    g2 = mod[:, 5 * D_MODEL:6 * D_MODEL]
    g_ret = gate_ref[:, 0:D_MODEL].astype(F32)
    g_na = gate_ref[:, D_MODEL:2 * D_MODEL].astype(F32)
    branches = (g_ret * _dot(yret_ref[...], w_ret_ref[...])
                + g_na * _dot(yna_ref[...], w_na_ref[...]))
    mixed = _dot(branches.astype(BF16), w_o_ref[...])
    x = x_ref[...] + g1 * _rms(mixed, n1_ref[...])
    hf = (_rms(x, n2_ref[...]) * (1.0 + sc2) + sh2).astype(BF16)
    act = (_silu(_dot(hf, w_g_ref[...])) * _dot(hf, w_u_ref[...])).astype(BF16)
    o_ref[...] = x + g2 * _rms(_dot(act, w_d_ref[...]), n3_ref[...])


def _tail(x, mod, mod_row, y_ret, y_na, gates, n1, n2, n3, w_ret, w_na, w_o, w_g, w_u, w_d):
    b, l, _ = x.shape
    tm = min(TOKEN_TILE, l)
    tok = lambda w: pl.BlockSpec((None, tm, w), lambda i, j: (i, j, 0))
    vec = lambda a: a.reshape(1, D_MODEL)
    return pl.pallas_call(
        _tail_kernel,
        grid=(b, l // tm),
        in_specs=[
            tok(D_MODEL),
            pl.BlockSpec((1, 1, 6 * D_MODEL), lambda i, j: (mod_row(i), 0, 0)),
            tok(RET_V_W), tok(NA_W), tok(2 * D_MODEL),
            _resident((1, D_MODEL)), _resident((1, D_MODEL)), _resident((1, D_MODEL)),
            _resident(w_ret.shape), _resident(w_na.shape), _resident(w_o.shape),
            _resident(w_g.shape), _resident(w_u.shape), _resident(w_d.shape),
        ],
        out_specs=tok(D_MODEL),
        out_shape=jax.ShapeDtypeStruct((b, l, D_MODEL), F32),
        compiler_params=_params("arbitrary", "arbitrary"),
        name="tail",
    )(x, mod, y_ret, y_na, gates, vec(n1), vec(n2), vec(n3), w_ret, w_na, w_o, w_g, w_u, w_d)


def _rope_tables(l):
    t = jnp.arange(l)
    row = (t // GRID_W).astype(F32)
    col = (t % GRID_W).astype(F32)
    n_freq = RET_DK // 4
    inv = ROPE_BASE ** (-jnp.arange(n_freq, dtype=F32) / n_freq)
    ang = jnp.concatenate([row[:, None] * inv, col[:, None] * inv], axis=-1)
    cos = jnp.cos(ang)
    sin = jnp.sin(ang)
    return jnp.concatenate([cos, cos], axis=-1), jnp.concatenate([-sin, sin], axis=-1)


def kernel(x_prompt, x_sample, cache_na_k, cache_na_v, state_ret, c, c_ctx, w_ada, b_ada,
           norm_pre_mix, norm_post_mix, norm_pre_ffn, norm_post_ffn, w_in, ret_decay_logit,
           ret_gn_gain, na_rpb, w_ret_out, w_na_out, w_gate, w_o, w_ffn_gate, w_ffn_up,
           w_ffn_down):
    dec_b, dec_l, _ = x_sample.shape
    past = cache_na_k.shape[2]
    cvec = jnp.zeros((MOD_ROWS, D_MODEL), F32).at[0].set(c_ctx).at[1:1 + dec_b].set(c)
    mods = _adaln(cvec, w_ada, b_ada).reshape(DEPTH, MOD_ROWS, 1, 6 * D_MODEL)
    log_g = jax.nn.log_sigmoid(ret_decay_logit.astype(F32))
    rope_tabs = _rope_tables(dec_l)
    bf = lambda a: a.astype(BF16)
    layers = lambda w: [bf(w[l]) for l in range(DEPTH)]
    w_in_b, w_gate_b = layers(w_in), layers(w_gate)
    w_ret_b, w_na_b, w_o_b = layers(w_ret_out), layers(w_na_out), layers(w_o)
    w_g_b, w_u_b, w_d_b = layers(w_ffn_gate), layers(w_ffn_up), layers(w_ffn_down)
    ck = [bf(cache_na_k[:, l]).reshape(dec_b, past, NA_W) for l in range(DEPTH)]
    cv = [bf(cache_na_v[:, l]).reshape(dec_b, past, NA_W) for l in range(DEPTH)]
    bias_blocks = _na_bias_blocks(na_rpb.reshape((DEPTH * NA_HEADS,) + na_rpb.shape[2:]))
    bias_blocks = bias_blocks.reshape((DEPTH, NA_HEADS) + bias_blocks.shape[1:])
    ctx_row = lambda i: 0
    dec_row = lambda i: i + 1

    def tail(x, l, row, y_ret, y_na, gates):
        return _tail(x, mods[l], row, y_ret, y_na, gates, norm_post_mix[l], norm_pre_ffn[l],
                     norm_post_ffn[l], w_ret_b[l], w_na_b[l], w_o_b[l], w_g_b[l], w_u_b[l],
                     w_d_b[l])

    ctx_b, ctx_l, _ = x_prompt.shape
    flat = lambda a: a.reshape(1, ctx_b * ctx_l, a.shape[-1])
    per_seq = lambda a: a.reshape(ctx_b, ctx_l, a.shape[-1])
    x = flat(x_prompt)
    new_k = new_v = new_s = None
    for l in range(DEPTH):
        rq, rk, rv, rg, nq, new_k, new_v, gates = _premix(
            x, mods[l], ctx_row, norm_pre_mix[l], w_in_b[l], w_gate_b[l],
            cache=(l, ctx_l, new_k, new_v))
        y_ret, new_s = _retention(log_g[l], per_seq(rq), per_seq(rk), per_seq(rv), per_seq(rg),
                                  ret_gn_gain[l], new_state=(l, new_s))
        y_na = _ctx_attention(per_seq(nq), new_k, new_v, l)
        x = tail(x, l, ctx_row, flat(y_ret), flat(y_na), gates)
    y_prompt = per_seq(x)
    new_k = new_k.reshape(ctx_b, DEPTH, ctx_l, NA_HEADS, NA_DH)
    new_v = new_v.reshape(ctx_b, DEPTH, ctx_l, NA_HEADS, NA_DH)

    x = x_sample
    for l in range(DEPTH):
        rq, rk, rv, rg, nq, nk, nv, gates = _premix(
            x, mods[l], dec_row, norm_pre_mix[l], w_in_b[l], w_gate_b[l], rope_tabs=rope_tabs)
        (y_ret,) = _retention(log_g[l], rq, rk, rv, rg, ret_gn_gain[l], state=state_ret, layer=l)
        y_na = _na_attention(nq, nk, nv, ck[l], cv[l], bias_blocks, l)
        x = tail(x, l, dec_row, y_ret, y_na, gates)
    return (y_prompt, x, new_k, new_v, new_s)
```

```python
import functools
import math

import numpy as np
import jax
import jax.numpy as jnp
from jax import lax
from jax.experimental import pallas as pl
from jax.experimental.pallas import tpu as pltpu

F32 = jnp.float32
BF16 = jnp.bfloat16

D_MODEL = 1024
DEPTH = 2
GRID_W = 64
RET_HEADS = 4
RET_DK = 128
RET_DV = 256
RET_QK_W = RET_HEADS * RET_DK
RET_V_W = RET_HEADS * RET_DV
RET_CHUNK = 128
NA_HEADS = 8
NA_DH = 64
NA_W = NA_HEADS * NA_DH
NA_WIN_H = 8
NA_WIN_W = 16
D_FF = 2816
ROPE_BASE = 10000.0
EPS = 1e-6
IN_WIDTH = 2 * RET_QK_W + 2 * RET_V_W + 3 * NA_W

LANES = 128
HEAD_PAIRS = NA_W // LANES
MOD_ROWS = 16
NA_Q_ROWS = 2
NA_Q_BLOCK = NA_Q_ROWS * GRID_W
NA_KEY_BLOCKS = 5
NA_KEY_ROWS = NA_KEY_BLOCKS * LANES // GRID_W
NA_PATTERNS = (0, 1, 2, 14, 15)
NA_BLOCKS_PER_STEP = 4
MASKED = -1e30
LOG2_E = math.log2(math.e)
NA_Q_SCALE = NA_DH ** -0.5 * LOG2_E
VMEM_LIMIT = 56 * 1024 * 1024
TOKEN_TILE = 512
RET_UNROLL = 16
SHORT_SEQ_TOKENS = 2048


def _sigmoid(x):
    return 1.0 / (1.0 + jnp.exp(-x))


def _silu(x):
    return x * _sigmoid(x)


def _rms(x, g):
    return x * lax.rsqrt(jnp.mean(x * x, axis=-1, keepdims=True) + EPS) * g


def _dot(a, b):
    return jnp.dot(a, b, preferred_element_type=F32)


def _dot_nt(a, b):
    return lax.dot_general(a, b, (((1,), (1,)), ((), ())), preferred_element_type=F32)


def _resident(shape):
    zeros = (0,) * len(shape)
    return pl.BlockSpec(shape, lambda *_: zeros, pipeline_mode=pl.Buffered(1))


def _params(*sem):
    return pltpu.CompilerParams(dimension_semantics=sem, vmem_limit_bytes=VMEM_LIMIT)


def _adaln_kernel(c_ref, w_ref, b_ref, o_ref):
    s = _silu(c_ref[...]).astype(BF16)
    o_ref[0] = _dot(s, w_ref[0].astype(BF16)) + b_ref[0]


def _adaln(cvec, w_ada, b_ada):
    tn = 1536
    n = 6 * D_MODEL
    return pl.pallas_call(
        _adaln_kernel,
        grid=(DEPTH, n // tn),
        in_specs=[
            pl.BlockSpec((MOD_ROWS, D_MODEL), lambda l, j: (0, 0)),
            pl.BlockSpec((1, D_MODEL, tn), lambda l, j: (l, 0, j)),
            pl.BlockSpec((1, 1, tn), lambda l, j: (l, 0, j)),
        ],
        out_specs=pl.BlockSpec((1, MOD_ROWS, tn), lambda l, j: (l, 0, j)),
        out_shape=jax.ShapeDtypeStruct((DEPTH, MOD_ROWS, n), F32),
        compiler_params=_params("arbitrary", "arbitrary"),
        name="adaln",
    )(cvec, w_ada, b_ada.reshape(DEPTH, 1, n))


def _premix_kernel(*refs, rope, cache_layer):
    refs = list(refs)
    x_ref, mod_ref, g_ref, w_in_ref, w_gate_ref = refs[:5]
    pos = 5
    if rope:
        cos_ref, sin_ref = refs[pos:pos + 2]
        pos += 2
    if cache_layer:
        k_prev_ref, v_prev_ref = refs[pos:pos + 2]
        pos += 2
    rq_ref, rk_ref, rv_ref, rg_ref, nq_ref, nk_ref, nv_ref, gate_ref = refs[pos:]
    mod = mod_ref[0]
    sh1 = mod[:, 0:D_MODEL]
    sc1 = mod[:, D_MODEL:2 * D_MODEL]
    hm = (_rms(x_ref[...], g_ref[...]) * (1.0 + sc1) + sh1).astype(BF16)

    def proj(lo, width):
        return _dot(hm, w_in_ref[:, lo:lo + width])

    def rotary(t):
        if not rope:
            return t
        cos = cos_ref[...]
        sin = sin_ref[...]
        heads = []
        for h in range(RET_HEADS):
            blk = t[:, h * RET_DK:(h + 1) * RET_DK]
            heads.append(blk * cos + pltpu.roll(blk, RET_DK // 2, axis=1) * sin)
        return jnp.concatenate(heads, axis=1)

    def store_kv(ref, prev_ref, t):
        if cache_layer is None:
            ref[...] = t.astype(ref.dtype)
            return
        if cache_layer:
            ref[:, 0:cache_layer] = prev_ref[...]
        ref[:, cache_layer] = t.reshape(ref.shape[0], ref.shape[2], ref.shape[3])

    lo = 0
    rq_ref[...] = rotary(proj(lo, RET_QK_W)).astype(rq_ref.dtype)
    lo += RET_QK_W
    rk_ref[...] = (rotary(proj(lo, RET_QK_W)) * (RET_DK ** -0.5)).astype(rk_ref.dtype)
    lo += RET_QK_W
    rv_ref[...] = proj(lo, RET_V_W).astype(rv_ref.dtype)
    lo += RET_V_W
    rg_ref[...] = proj(lo, RET_V_W).astype(rg_ref.dtype)
    lo += RET_V_W
    nq_ref[...] = (proj(lo, NA_W) * NA_Q_SCALE).astype(nq_ref.dtype)
    lo += NA_W
    store_kv(nk_ref, k_prev_ref if cache_layer else None, proj(lo, NA_W))
    lo += NA_W
    store_kv(nv_ref, v_prev_ref if cache_layer else None, proj(lo, NA_W))
    gate_ref[...] = _sigmoid(_dot(hm, w_gate_ref[...])).astype(gate_ref.dtype)


def _premix(x, mod, mod_row, g, w_in, w_gate, rope_tabs=None, cache=None):
    b, l, _ = x.shape
    tm = min(TOKEN_TILE, l)
    tok = lambda w: pl.BlockSpec((None, tm, w), lambda i, j: (i, j, 0))
    in_specs = [
        tok(D_MODEL),
        pl.BlockSpec((1, 1, 6 * D_MODEL), lambda i, j: (mod_row(i), 0, 0)),
        _resident((1, D_MODEL)),
        _resident((D_MODEL, IN_WIDTH)),
        _resident((D_MODEL, 2 * D_MODEL)),
    ]
    args = [x, mod, g.reshape(1, D_MODEL), w_in, w_gate]
    if rope_tabs is not None:
        in_specs += [pl.BlockSpec((tm, RET_DK), lambda i, j: (j, 0))] * 2
        args += list(rope_tabs)
    widths = (RET_QK_W, RET_QK_W, RET_V_W, RET_V_W, NA_W, NA_W, NA_W, 2 * D_MODEL)
    out_specs = [tok(w) for w in widths]
    out_shape = [jax.ShapeDtypeStruct((b, l, w), BF16) for w in widths]
    cache_layer = None
    if cache is not None:
        cache_layer, seq, k_prev, v_prev = cache
        assert b == 1 and tm % seq == 0
        per_tile = tm // seq
        kv_spec = lambda n: pl.BlockSpec((per_tile, n, seq, NA_W), lambda i, j: (j, 0, 0, 0))
        if cache_layer:
            in_specs += [kv_spec(cache_layer)] * 2
            args += [k_prev, v_prev]
        kv_shape = jax.ShapeDtypeStruct((l // seq, cache_layer + 1, seq, NA_W), F32)
        out_specs[5:7] = [kv_spec(cache_layer + 1)] * 2
        out_shape[5:7] = [kv_shape] * 2
    return pl.pallas_call(
        functools.partial(_premix_kernel, rope=rope_tabs is not None, cache_layer=cache_layer),
        grid=(b, l // tm),
        in_specs=in_specs,
        out_specs=out_specs,
        out_shape=out_shape,
        compiler_params=_params("arbitrary", "arbitrary"),
        name="premix",
    )(*args)


def _retention_kernel(*refs, n_chunks, has_s0, state_layer):
    refs = list(refs)
    lg_ref, q_ref, k_ref, v_ref, rg_ref, gain_ref = refs[:6]
    pos = 6
    if has_s0:
        s0_ref = refs[pos]
        pos += 1
    if state_layer:
        sprev_ref = refs[pos]
        pos += 1
    y_ref = refs[pos]
    pos += 1
    if state_layer is not None:
        sout_ref = refs[pos]
        pos += 1
    s_scr = refs[pos]

    c = RET_CHUNK
    h = pl.program_id(1)
    lgf = lg_ref[0, h]
    lgb = lg_ref[1, h]
    row = lax.broadcasted_iota(jnp.int32, (c, c), 0).astype(F32)
    col = lax.broadcasted_iota(jnp.int32, (c, c), 1).astype(F32)
    rel = row - col
    decay = (jnp.where(rel >= 0, jnp.exp(lgf * jnp.maximum(rel, 0.0)), 0.0)
             + jnp.where(rel <= 0, jnp.exp(lgb * jnp.maximum(-rel, 0.0)), 0.0))
    p = lax.broadcasted_iota(jnp.int32, (c, 1), 0).astype(F32)
    xi_f = jnp.exp(lgf * (p + 1.0))
    xi_b = jnp.exp(lgb * (c - p))
    zeta_f = jnp.exp(lgf * (c - 1.0 - p))
    zeta_b = jnp.exp(lgb * p)
    chunk_len = jnp.full((1, RET_DV), float(c), F32)
    g_f = jnp.exp(lgf * chunk_len)
    g_b = jnp.exp(lgb * chunk_len)

    gain = gain_ref[...]
    unroll = min(n_chunks, RET_UNROLL)

    def chunk(ref, seq, i):
        return ref[seq, pl.ds(pl.multiple_of(i * c, c), c), :]

    for seq in range(q_ref.shape[0]):
        def outer_kv(i, zeta):
            kz = (chunk(k_ref, seq, i).astype(F32) * zeta).T.astype(BF16)
            return _dot(kz, chunk(v_ref, seq, i))

        if has_s0:
            s_f0 = s0_ref[seq, 0]
            s_b0 = s0_ref[seq, 1]
        else:
            s_f0 = jnp.zeros((RET_DK, RET_DV), F32)
            s_b0 = s_f0

        def scan(t, carry):
            s_f, s_b = carry
            i_b = n_chunks - 1 - t
            s_scr[seq, t, 0:RET_DK, :] = s_f.astype(BF16)
            s_scr[seq, i_b, RET_DK:2 * RET_DK, :] = s_b.astype(BF16)
            return (g_f * s_f + outer_kv(t, zeta_f), g_b * s_b + outer_kv(i_b, zeta_b))

        s_f, s_b = lax.fori_loop(0, n_chunks, scan, (s_f0, s_b0), unroll=unroll)
        if state_layer is not None:
            if state_layer:
                sout_ref[seq, 0:state_layer] = sprev_ref[seq]
            sout_ref[seq, state_layer, 0] = s_f
            sout_ref[seq, state_layer, 1] = s_b

        def emit(i, _):
            qi = chunk(q_ref, seq, i)
            scores = (_dot_nt(qi, chunk(k_ref, seq, i)) * decay).astype(BF16)
            qf = qi.astype(F32)
            qx = jnp.concatenate([(qf * xi_f).astype(BF16), (qf * xi_b).astype(BF16)], axis=1)
            y = _dot(scores, chunk(v_ref, seq, i)) + _dot(qx, s_scr[seq, i])
            mu = jnp.mean(y, axis=-1, keepdims=True)
            d = y - mu
            var = jnp.mean(d * d, axis=-1, keepdims=True)
            yn = d * lax.rsqrt(var + EPS) * gain
            out = _silu(chunk(rg_ref, seq, i).astype(F32)) * yn
            y_ref[seq, pl.ds(pl.multiple_of(i * c, c), c), :] = out.astype(y_ref.dtype)
            return 0

        lax.fori_loop(0, n_chunks, emit, 0, unroll=unroll)


def _retention(log_g, rq, rk, rv, rg, gain, state=None, layer=0, new_state=None):
    b, l, _ = rq.shape
    n_chunks = l // RET_CHUNK
    seqs = max(1, SHORT_SEQ_TOKENS // l)
    assert b % seqs == 0
    head = lambda w: pl.BlockSpec((seqs, l, w), lambda i, h: (i, 0, h))
    in_specs = [pl.BlockSpec(memory_space=pltpu.SMEM), head(RET_DK), head(RET_DK),
                head(RET_DV), head(RET_DV), pl.BlockSpec((1, RET_DV), lambda i, h: (0, h))]
    args = [log_g, rq, rk, rv, rg, gain.reshape(1, RET_V_W)]
    if state is not None:
        in_specs.append(pl.BlockSpec((seqs, None, 2, None, RET_DK, RET_DV),
                                     lambda i, h: (i, layer, 0, h, 0, 0)))
        args.append(state)
    out_specs = [head(RET_DV)]
    out_shape = [jax.ShapeDtypeStruct((b, l, RET_V_W), BF16)]
    state_layer = None
    if new_state is not None:
        state_layer, earlier = new_state
        state_spec = lambda n: pl.BlockSpec((seqs, n, 2, None, RET_DK, RET_DV),
                                            lambda i, h: (i, 0, 0, h, 0, 0))
        if state_layer:
            in_specs.append(state_spec(state_layer))
            args.append(earlier)
        out_specs.append(state_spec(state_layer + 1))
        out_shape.append(jax.ShapeDtypeStruct(
            (b, state_layer + 1, 2, RET_HEADS, RET_DK, RET_DV), F32))
    return pl.pallas_call(
        functools.partial(_retention_kernel, n_chunks=n_chunks, has_s0=state is not None,
                          state_layer=state_layer),
        grid=(b // seqs, RET_HEADS),
        in_specs=in_specs,
        out_specs=out_specs,
        out_shape=out_shape,
        scratch_shapes=[pltpu.VMEM((seqs, n_chunks, 2 * RET_DK, RET_DV), BF16)],
        compiler_params=_params("arbitrary", "arbitrary"),
        name="retention",
    )(*args)


def _fold_lanes(x, op):
    return functools.reduce(op, [x[:, c:c + LANES] for c in range(0, x.shape[1], LANES)])


def _pair_attention(q2, keys, values, biases):
    n_q = q2.shape[0]
    lane = lax.broadcasted_iota(jnp.int32, (1, LANES), 1)
    lo_mask = lane < NA_DH
    zero = jnp.zeros_like(q2)
    qq = jnp.concatenate([jnp.where(lo_mask, q2, zero), jnp.where(lo_mask, zero, q2)], axis=0)
    scores = []
    for kt, bt in zip(keys, biases):
        s = _dot_nt(qq, kt)
        scores.append(s if bt is None else s + bt)
    m = functools.reduce(jnp.maximum, [_fold_lanes(s, jnp.maximum) for s in scores])
    m = m.max(axis=-1, keepdims=True)
    denom = None
    acc = None
    for s, v in zip(scores, values):
        e = jnp.exp2(s - m)
        part = _fold_lanes(e, jnp.add)
        pv = _dot(e.astype(BF16), v)
        denom = part if denom is None else denom + part
        acc = pv if acc is None else acc + pv
    out = acc * (1.0 / denom.sum(axis=-1, keepdims=True))
    return jnp.where(lo_mask, out[:n_q], out[n_q:])


def _ctx_attn_kernel(q_ref, k_ref, v_ref, o_ref):
    for seq in range(q_ref.shape[0]):
        for hp in range(HEAD_PAIRS):
            sl = slice(hp * LANES, (hp + 1) * LANES)
            out = _pair_attention(q_ref[seq, :, sl], [k_ref[seq, :, sl].astype(BF16)],
                                  [v_ref[seq, :, sl].astype(BF16)], [None])
            o_ref[seq, :, sl] = out.astype(o_ref.dtype)


def _ctx_attention(nq, k_cache, v_cache, layer):
    b, l, _ = nq.shape
    seqs = max(1, SHORT_SEQ_TOKENS // l)
    assert b % seqs == 0
    spec = pl.BlockSpec((seqs, l, NA_W), lambda i: (i, 0, 0))
    kv_spec = pl.BlockSpec((seqs, None, l, NA_W), lambda i: (i, layer, 0, 0))
    return pl.pallas_call(
        _ctx_attn_kernel,
        grid=(b // seqs,),
        in_specs=[spec, kv_spec, kv_spec],
        out_specs=spec,
        out_shape=jax.ShapeDtypeStruct((b, l, NA_W), BF16),
        compiler_params=_params("arbitrary"),
        name="ctx_attention",
    )(nq, k_cache, v_cache)


def _na_window_start(j, rows):
    return int(np.clip(NA_Q_ROWS * j - NA_WIN_H // 2, 0, rows - NA_KEY_ROWS))


def _na_bias_rows(rpb):
    half = NA_WIN_W - 1
    gap = jnp.zeros(rpb.shape[:-1] + (LANES - rpb.shape[-1],), F32)
    return jnp.concatenate([rpb[..., half:], gap, rpb[..., :half]], axis=-1) * LOG2_E


def _na_block_index(rows):
    a = np.arange(NA_Q_ROWS)[:, None]
    kk = np.arange(NA_KEY_ROWS)[None, :]
    index = []
    for j in NA_PATTERNS:
        qr = NA_Q_ROWS * j + a
        kr = _na_window_start(j, rows) + kk
        rs = np.clip(qr - NA_WIN_H // 2, 0, rows - NA_WIN_H)
        row_valid = (kr >= rs) & (kr < rs + NA_WIN_H)
        index.append(np.where(row_valid, kr - qr + NA_WIN_H - 1, 2 * NA_WIN_H - 1))
    return np.stack(index)


def _na_kernel(q_ref, k_ref, v_ref, ck_ref, cv_ref, rows_ref, o_ref, tab_ref, blocks_ref, *,
               index):
    step = pl.program_id(1)
    n_blocks = pl.num_programs(1) * NA_BLOCKS_PER_STEP

    @pl.when((pl.program_id(0) == 0) & (step == 0))
    def _():
        heads, n_dr, _ = rows_ref.shape
        qc = lax.broadcasted_iota(jnp.int32, (GRID_W, LANES), 0)
        kc = lax.broadcasted_iota(jnp.int32, (GRID_W, LANES), 1)
        cs = jnp.clip(qc - NA_WIN_W // 2, 0, GRID_W - NA_WIN_W)
        col_valid = (kc >= cs) & (kc < cs + NA_WIN_W)
        for h in range(heads):
            for dr in range(n_dr):
                row = jnp.broadcast_to(rows_ref[h, dr:dr + 1, :], (GRID_W, LANES))
                skew = pltpu.roll(row, 0, 1, stride=1, stride_axis=0)
                blocks_ref[h, dr] = jnp.where(col_valid, skew, MASKED)
        blocks_ref[:, n_dr] = jnp.full((heads, GRID_W, LANES), MASKED, F32)
        for p, rows_p in enumerate(index):
            for a, row_a in enumerate(rows_p):
                for kk, blk in enumerate(row_a):
                    tab_ref[p, :, a * GRID_W:(a + 1) * GRID_W, kk * GRID_W:(kk + 1) * GRID_W] = (
                        blocks_ref[:, int(blk), :, 0:GRID_W])

    for sub in range(NA_BLOCKS_PER_STEP):
        j = step * NA_BLOCKS_PER_STEP + sub
        rows = slice(sub * NA_Q_BLOCK, (sub + 1) * NA_Q_BLOCK)
        pattern = jnp.minimum(j, 2) + jnp.maximum(j - (n_blocks - 3), 0)
        start = jnp.clip(j - 2, 0, n_blocks - NA_KEY_BLOCKS) * NA_Q_BLOCK
        window = pl.ds(pl.multiple_of(start, NA_Q_BLOCK), NA_KEY_BLOCKS * LANES)
        for hp in range(HEAD_PAIRS):
            sl = slice(hp * LANES, (hp + 1) * LANES)
            bias = tab_ref[pattern, 2 * hp:2 * hp + 2].reshape(2 * NA_Q_BLOCK, -1)
            out = _pair_attention(q_ref[rows, sl], [k_ref[window, sl], ck_ref[:, sl]],
                                  [v_ref[window, sl], cv_ref[:, sl]], [bias, None])
            o_ref[rows, sl] = out.astype(o_ref.dtype)


def _na_attention(nq, nk, nv, ck, cv, bias_rows, layer):
    b, l, _ = nq.shape
    n_blocks = l // NA_Q_BLOCK
    assert [min(j, 2) + max(j - (n_blocks - 3), 0) for j in NA_PATTERNS] == list(range(5))
    assert n_blocks % NA_BLOCKS_PER_STEP == 0
    tile = pl.BlockSpec((None, NA_BLOCKS_PER_STEP * NA_Q_BLOCK, NA_W), lambda i, j: (i, j, 0))
    whole = lambda a: pl.BlockSpec((None,) + a.shape[1:], lambda i, j: (i, 0, 0))
    table = (len(NA_PATTERNS), NA_HEADS, NA_Q_BLOCK, NA_KEY_ROWS * GRID_W)
    return pl.pallas_call(
        functools.partial(_na_kernel, index=_na_block_index(l // GRID_W)),
        grid=(b, n_blocks // NA_BLOCKS_PER_STEP),
        in_specs=[tile, whole(nk), whole(nv), whole(ck), whole(cv),
                  pl.BlockSpec((None,) + bias_rows.shape[1:], lambda i, j: (layer, 0, 0, 0),
                               pipeline_mode=pl.Buffered(1))],
        out_specs=tile,
        out_shape=jax.ShapeDtypeStruct((b, l, NA_W), BF16),
        scratch_shapes=[pltpu.VMEM(table, F32),
                        pltpu.VMEM((NA_HEADS, 2 * NA_WIN_H, GRID_W, LANES), F32)],
        compiler_params=_params("arbitrary", "arbitrary"),
        name="na_attention",
    )(nq, nk, nv, ck, cv, bias_rows)


def _tail_kernel(x_ref, mod_ref, yret_ref, yna_ref, gate_ref, n1_ref, n2_ref, n3_ref,
                 w_ret_ref, w_na_ref, w_o_ref, w_g_ref, w_u_ref, w_d_ref, o_ref):
    mod = mod_ref[0]
    g1 = mod[:, 2 * D_MODEL:3 * D_MODEL]
    sh2 = mod[:, 3 * D_MODEL:4 * D_MODEL]
    sc2 = mod[:, 4 * D_MODEL:5 * D_MODEL]
    g2 = mod[:, 5 * D_MODEL:6 * D_MODEL]
    g_ret = gate_ref[:, 0:D_MODEL].astype(F32)
    g_na = gate_ref[:, D_MODEL:2 * D_MODEL].astype(F32)
    branches = (g_ret * _dot(yret_ref[...], w_ret_ref[...])
                + g_na * _dot(yna_ref[...], w_na_ref[...]))
    mixed = _dot(branches.astype(BF16), w_o_ref[...])
    x = x_ref[...] + g1 * _rms(mixed, n1_ref[...])
    hf = (_rms(x, n2_ref[...]) * (1.0 + sc2) + sh2).astype(BF16)
    act = (_silu(_dot(hf, w_g_ref[...])) * _dot(hf, w_u_ref[...])).astype(BF16)
    o_ref[...] = x + g2 * _rms(_dot(act, w_d_ref[...]), n3_ref[...])


def _tail(x, mod, mod_row, y_ret, y_na, gates, n1, n2, n3, w_ret, w_na, w_o, w_g, w_u, w_d):
    b, l, _ = x.shape
    tm = min(TOKEN_TILE, l)
    tok = lambda w: pl.BlockSpec((None, tm, w), lambda i, j: (i, j, 0))
    vec = lambda a: a.reshape(1, D_MODEL)
    return pl.pallas_call(
        _tail_kernel,
        grid=(b, l // tm),
        in_specs=[
            tok(D_MODEL),
            pl.BlockSpec((1, 1, 6 * D_MODEL), lambda i, j: (mod_row(i), 0, 0)),
            tok(RET_V_W), tok(NA_W), tok(2 * D_MODEL),
            _resident((1, D_MODEL)), _resident((1, D_MODEL)), _resident((1, D_MODEL)),
            _resident(w_ret.shape), _resident(w_na.shape), _resident(w_o.shape),
            _resident(w_g.shape), _resident(w_u.shape), _resident(w_d.shape),
        ],
        out_specs=tok(D_MODEL),
        out_shape=jax.ShapeDtypeStruct((b, l, D_MODEL), F32),
        compiler_params=_params("arbitrary", "arbitrary"),
        name="tail",
    )(x, mod, y_ret, y_na, gates, vec(n1), vec(n2), vec(n3), w_ret, w_na, w_o, w_g, w_u, w_d)


def _rope_tables(l):
    t = jnp.arange(l)
    row = (t // GRID_W).astype(F32)
    col = (t % GRID_W).astype(F32)
    n_freq = RET_DK // 4
    inv = ROPE_BASE ** (-jnp.arange(n_freq, dtype=F32) / n_freq)
    ang = jnp.concatenate([row[:, None] * inv, col[:, None] * inv], axis=-1)
    cos = jnp.cos(ang)
    sin = jnp.sin(ang)
    return jnp.concatenate([cos, cos], axis=-1), jnp.concatenate([-sin, sin], axis=-1)


def kernel(x_prompt, x_sample, cache_na_k, cache_na_v, state_ret, c, c_ctx, w_ada, b_ada,
           norm_pre_mix, norm_post_mix, norm_pre_ffn, norm_post_ffn, w_in, ret_decay_logit,
           ret_gn_gain, na_rpb, w_ret_out, w_na_out, w_gate, w_o, w_ffn_gate, w_ffn_up,
           w_ffn_down):
    dec_b, dec_l, _ = x_sample.shape
    past = cache_na_k.shape[2]
    cvec = jnp.zeros((MOD_ROWS, D_MODEL), F32).at[0].set(c_ctx).at[1:1 + dec_b].set(c)
    mods = _adaln(cvec, w_ada, b_ada).reshape(DEPTH, MOD_ROWS, 1, 6 * D_MODEL)
    log_g = jax.nn.log_sigmoid(ret_decay_logit.astype(F32))
    rope_tabs = _rope_tables(dec_l)
    bf = lambda a: a.astype(BF16)
    layers = lambda w: [bf(w[l]) for l in range(DEPTH)]
    w_in_b, w_gate_b = layers(w_in), layers(w_gate)
    w_ret_b, w_na_b, w_o_b = layers(w_ret_out), layers(w_na_out), layers(w_o)
    w_g_b, w_u_b, w_d_b = layers(w_ffn_gate), layers(w_ffn_up), layers(w_ffn_down)
    ck = [bf(cache_na_k[:, l]).reshape(dec_b, past, NA_W) for l in range(DEPTH)]
    cv = [bf(cache_na_v[:, l]).reshape(dec_b, past, NA_W) for l in range(DEPTH)]
    bias_rows = _na_bias_rows(na_rpb.astype(F32))
    ctx_row = lambda i: 0
    dec_row = lambda i: i + 1

    def tail(x, l, row, y_ret, y_na, gates):
        return _tail(x, mods[l], row, y_ret, y_na, gates, norm_post_mix[l], norm_pre_ffn[l],
                     norm_post_ffn[l], w_ret_b[l], w_na_b[l], w_o_b[l], w_g_b[l], w_u_b[l],
                     w_d_b[l])

    ctx_b, ctx_l, _ = x_prompt.shape
    flat = lambda a: a.reshape(1, ctx_b * ctx_l, a.shape[-1])
    per_seq = lambda a: a.reshape(ctx_b, ctx_l, a.shape[-1])
    x = flat(x_prompt)
    new_k = new_v = new_s = None
    for l in range(DEPTH):
        rq, rk, rv, rg, nq, new_k, new_v, gates = _premix(
            x, mods[l], ctx_row, norm_pre_mix[l], w_in_b[l], w_gate_b[l],
            cache=(l, ctx_l, new_k, new_v))
        y_ret, new_s = _retention(log_g[l], per_seq(rq), per_seq(rk), per_seq(rv), per_seq(rg),
                                  ret_gn_gain[l], new_state=(l, new_s))
        y_na = _ctx_attention(per_seq(nq), new_k, new_v, l)
        x = tail(x, l, ctx_row, flat(y_ret), flat(y_na), gates)
    y_prompt = per_seq(x)
    new_k = new_k.reshape(ctx_b, DEPTH, ctx_l, NA_HEADS, NA_DH)
    new_v = new_v.reshape(ctx_b, DEPTH, ctx_l, NA_HEADS, NA_DH)

    x = x_sample
    for l in range(DEPTH):
        rq, rk, rv, rg, nq, nk, nv, gates = _premix(
            x, mods[l], dec_row, norm_pre_mix[l], w_in_b[l], w_gate_b[l], rope_tabs=rope_tabs)
        (y_ret,) = _retention(log_g[l], rq, rk, rv, rg, ret_gn_gain[l], state=state_ret, layer=l)
        y_na = _na_attention(nq, nk, nv, ck[l], cv[l], bias_rows, l)
        x = tail(x, l, dec_row, y_ret, y_na, gates)
    return (y_prompt, x, new_k, new_v, new_s)
```

```python
import functools
import math

import numpy as np
import jax
import jax.numpy as jnp
from jax import lax
from jax.experimental import pallas as pl
from jax.experimental.pallas import tpu as pltpu

F32 = jnp.float32
BF16 = jnp.bfloat16

D_MODEL = 1024
DEPTH = 2
GRID_W = 64
RET_HEADS = 4
RET_DK = 128
RET_DV = 256
RET_QK_W = RET_HEADS * RET_DK
RET_V_W = RET_HEADS * RET_DV
RET_CHUNK = 128
NA_HEADS = 8
NA_DH = 64
NA_W = NA_HEADS * NA_DH
NA_WIN_H = 8
NA_WIN_W = 16
D_FF = 2816
ROPE_BASE = 10000.0
EPS = 1e-6
IN_WIDTH = 2 * RET_QK_W + 2 * RET_V_W + 3 * NA_W

LANES = 128
HEAD_PAIRS = NA_W // LANES
MOD_ROWS = 16
NA_Q_ROWS = 2
NA_Q_BLOCK = NA_Q_ROWS * GRID_W
NA_KEY_BLOCKS = 5
NA_KEY_ROWS = NA_KEY_BLOCKS * LANES // GRID_W
NA_PATTERNS = (0, 1, 2, 14, 15)
NA_BLOCKS_PER_STEP = 4
MASKED = -1e30
LOG2_E = math.log2(math.e)
NA_Q_SCALE = NA_DH ** -0.5 * LOG2_E
VMEM_LIMIT = 56 * 1024 * 1024
TOKEN_TILE = 512
RET_UNROLL = 16
SHORT_SEQ_TOKENS = 2048


def _sigmoid(x):
    return 1.0 / (1.0 + jnp.exp(-x))


def _silu(x):
    return x * _sigmoid(x)


def _rms(x, g):
    return x * lax.rsqrt(jnp.mean(x * x, axis=-1, keepdims=True) + EPS) * g


def _dot(a, b):
    return jnp.dot(a, b, preferred_element_type=F32)


def _dot_nt(a, b):
    return lax.dot_general(a, b, (((1,), (1,)), ((), ())), preferred_element_type=F32)


def _resident(shape):
    zeros = (0,) * len(shape)
    return pl.BlockSpec(shape, lambda *_: zeros, pipeline_mode=pl.Buffered(1))


def _params(*sem):
    return pltpu.CompilerParams(dimension_semantics=sem, vmem_limit_bytes=VMEM_LIMIT)


def _adaln_kernel(c_ref, w_ref, b_ref, o_ref):
    s = _silu(c_ref[...]).astype(BF16)
    o_ref[0] = _dot(s, w_ref[0].astype(BF16)) + b_ref[0]


def _adaln(cvec, w_ada, b_ada):
    tn = 1536
    n = 6 * D_MODEL
    return pl.pallas_call(
        _adaln_kernel,
        grid=(DEPTH, n // tn),
        in_specs=[
            pl.BlockSpec((MOD_ROWS, D_MODEL), lambda l, j: (0, 0)),
            pl.BlockSpec((1, D_MODEL, tn), lambda l, j: (l, 0, j)),
            pl.BlockSpec((1, 1, tn), lambda l, j: (l, 0, j)),
        ],
        out_specs=pl.BlockSpec((1, MOD_ROWS, tn), lambda l, j: (l, 0, j)),
        out_shape=jax.ShapeDtypeStruct((DEPTH, MOD_ROWS, n), F32),
        compiler_params=_params("arbitrary", "arbitrary"),
        name="adaln",
    )(cvec, w_ada, b_ada.reshape(DEPTH, 1, n))


def _premix_kernel(*refs, rope, cache_layer):
    refs = list(refs)
    x_ref, mod_ref, g_ref, w_in_ref, w_gate_ref = refs[:5]
    pos = 5
    if rope:
        cos_ref, sin_ref = refs[pos:pos + 2]
        pos += 2
    if cache_layer:
        k_prev_ref, v_prev_ref = refs[pos:pos + 2]
        pos += 2
    rq_ref, rk_ref, rv_ref, rg_ref, nq_ref, nk_ref, nv_ref, gate_ref = refs[pos:]
    mod = mod_ref[0]
    sh1 = mod[:, 0:D_MODEL]
    sc1 = mod[:, D_MODEL:2 * D_MODEL]
    hm = (_rms(x_ref[...], g_ref[...]) * (1.0 + sc1) + sh1).astype(BF16)

    def proj(lo, width):
        return _dot(hm, w_in_ref[:, lo:lo + width])

    def rotary(t):
        if not rope:
            return t
        cos = cos_ref[...]
        sin = sin_ref[...]
        heads = []
        for h in range(RET_HEADS):
            blk = t[:, h * RET_DK:(h + 1) * RET_DK]
            heads.append(blk * cos + pltpu.roll(blk, RET_DK // 2, axis=1) * sin)
        return jnp.concatenate(heads, axis=1)

    def store_kv(ref, prev_ref, t):
        if cache_layer is None:
            ref[...] = t.astype(ref.dtype)
            return
        if cache_layer:
            ref[:, 0:cache_layer] = prev_ref[...]
        ref[:, cache_layer] = t.reshape(ref.shape[0], ref.shape[2], ref.shape[3])

    lo = 0
    rq_ref[...] = rotary(proj(lo, RET_QK_W)).astype(rq_ref.dtype)
    lo += RET_QK_W
    rk_ref[...] = (rotary(proj(lo, RET_QK_W)) * (RET_DK ** -0.5)).astype(rk_ref.dtype)
    lo += RET_QK_W
    rv_ref[...] = proj(lo, RET_V_W).astype(rv_ref.dtype)
    lo += RET_V_W
    rg_ref[...] = proj(lo, RET_V_W).astype(rg_ref.dtype)
    lo += RET_V_W
    nq_ref[...] = (proj(lo, NA_W) * NA_Q_SCALE).astype(nq_ref.dtype)
    lo += NA_W
    store_kv(nk_ref, k_prev_ref if cache_layer else None, proj(lo, NA_W))
    lo += NA_W
    store_kv(nv_ref, v_prev_ref if cache_layer else None, proj(lo, NA_W))
    gate_ref[...] = _sigmoid(_dot(hm, w_gate_ref[...])).astype(gate_ref.dtype)


def _premix(x, mod, mod_row, g, w_in, w_gate, rope_tabs=None, cache=None):
    b, l, _ = x.shape
    tm = min(TOKEN_TILE, l)
    tok = lambda w: pl.BlockSpec((None, tm, w), lambda i, j: (i, j, 0))
    in_specs = [
        tok(D_MODEL),
        pl.BlockSpec((1, 1, 6 * D_MODEL), lambda i, j: (mod_row(i), 0, 0)),
        _resident((1, D_MODEL)),
        _resident((D_MODEL, IN_WIDTH)),
        _resident((D_MODEL, 2 * D_MODEL)),
    ]
    args = [x, mod, g.reshape(1, D_MODEL), w_in, w_gate]
    if rope_tabs is not None:
        in_specs += [pl.BlockSpec((tm, RET_DK), lambda i, j: (j, 0))] * 2
        args += list(rope_tabs)
    widths = (RET_QK_W, RET_QK_W, RET_V_W, RET_V_W, NA_W, NA_W, NA_W, 2 * D_MODEL)
    out_specs = [tok(w) for w in widths]
    out_shape = [jax.ShapeDtypeStruct((b, l, w), BF16) for w in widths]
    cache_layer = None
    if cache is not None:
        cache_layer, seq, k_prev, v_prev = cache
        assert b == 1 and tm % seq == 0
        per_tile = tm // seq
        kv_spec = lambda n: pl.BlockSpec((per_tile, n, seq, NA_W), lambda i, j: (j, 0, 0, 0))
        if cache_layer:
            in_specs += [kv_spec(cache_layer)] * 2
            args += [k_prev, v_prev]
        kv_shape = jax.ShapeDtypeStruct((l // seq, cache_layer + 1, seq, NA_W), F32)
        out_specs[5:7] = [kv_spec(cache_layer + 1)] * 2
        out_shape[5:7] = [kv_shape] * 2
    return pl.pallas_call(
        functools.partial(_premix_kernel, rope=rope_tabs is not None, cache_layer=cache_layer),
        grid=(b, l // tm),
        in_specs=in_specs,
        out_specs=out_specs,
        out_shape=out_shape,
        compiler_params=_params("arbitrary", "arbitrary"),
        name="premix",
    )(*args)


def _retention_kernel(*refs, n_chunks, has_s0, state_layer):
    refs = list(refs)
    lg_ref, q_ref, k_ref, v_ref, rg_ref, gain_ref = refs[:6]
    pos = 6
    if has_s0:
        s0_ref = refs[pos]
        pos += 1
    if state_layer:
        sprev_ref = refs[pos]
        pos += 1
    y_ref = refs[pos]
    pos += 1
    if state_layer is not None:
        sout_ref = refs[pos]
        pos += 1
    s_scr = refs[pos]

    c = RET_CHUNK
    h = pl.program_id(1)
    lgf = lg_ref[0, h]
    lgb = lg_ref[1, h]
    row = lax.broadcasted_iota(jnp.int32, (c, c), 0).astype(F32)
    col = lax.broadcasted_iota(jnp.int32, (c, c), 1).astype(F32)
    rel = row - col
    decay = (jnp.where(rel >= 0, jnp.exp(lgf * jnp.maximum(rel, 0.0)), 0.0)
             + jnp.where(rel <= 0, jnp.exp(lgb * jnp.maximum(-rel, 0.0)), 0.0))
    p = lax.broadcasted_iota(jnp.int32, (c, 1), 0).astype(F32)
    xi_f = jnp.exp(lgf * (p + 1.0))
    xi_b = jnp.exp(lgb * (c - p))
    zeta_f = jnp.exp(lgf * (c - 1.0 - p))
    zeta_b = jnp.exp(lgb * p)
    chunk_len = jnp.full((1, RET_DV), float(c), F32)
    g_f = jnp.exp(lgf * chunk_len)
    g_b = jnp.exp(lgb * chunk_len)

    gain = gain_ref[...]
    unroll = min(n_chunks, RET_UNROLL)

    def chunk(ref, seq, i):
        return ref[seq, pl.ds(pl.multiple_of(i * c, c), c), :]

    for seq in range(q_ref.shape[0]):
        def outer_kv(i, zeta):
            kz = (chunk(k_ref, seq, i).astype(F32) * zeta).T.astype(BF16)
            return _dot(kz, chunk(v_ref, seq, i))

        if has_s0:
            s_f0 = s0_ref[seq, 0]
            s_b0 = s0_ref[seq, 1]
        else:
            s_f0 = jnp.zeros((RET_DK, RET_DV), F32)
            s_b0 = s_f0

        def scan(t, carry):
            s_f, s_b = carry
            i_b = n_chunks - 1 - t
            s_scr[seq, t, 0:RET_DK, :] = s_f.astype(BF16)
            s_scr[seq, i_b, RET_DK:2 * RET_DK, :] = s_b.astype(BF16)
            return (g_f * s_f + outer_kv(t, zeta_f), g_b * s_b + outer_kv(i_b, zeta_b))

        s_f, s_b = lax.fori_loop(0, n_chunks, scan, (s_f0, s_b0), unroll=unroll)
        if state_layer is not None:
            if state_layer:
                sout_ref[seq, 0:state_layer] = sprev_ref[seq]
            sout_ref[seq, state_layer, 0] = s_f
            sout_ref[seq, state_layer, 1] = s_b

        def emit(i, _):
            qi = chunk(q_ref, seq, i)
            scores = (_dot_nt(qi, chunk(k_ref, seq, i)) * decay).astype(BF16)
            qf = qi.astype(F32)
            qx = jnp.concatenate([(qf * xi_f).astype(BF16), (qf * xi_b).astype(BF16)], axis=1)
            y = _dot(scores, chunk(v_ref, seq, i)) + _dot(qx, s_scr[seq, i])
            mu = jnp.mean(y, axis=-1, keepdims=True)
            d = y - mu
            var = jnp.mean(d * d, axis=-1, keepdims=True)
            yn = d * lax.rsqrt(var + EPS) * gain
            out = _silu(chunk(rg_ref, seq, i).astype(F32)) * yn
            y_ref[seq, pl.ds(pl.multiple_of(i * c, c), c), :] = out.astype(y_ref.dtype)
            return 0

        lax.fori_loop(0, n_chunks, emit, 0, unroll=unroll)


def _retention(log_g, rq, rk, rv, rg, gain, state=None, layer=0, new_state=None):
    b, l, _ = rq.shape
    n_chunks = l // RET_CHUNK
    seqs = max(1, SHORT_SEQ_TOKENS // l)
    assert b % seqs == 0
    head = lambda w: pl.BlockSpec((seqs, l, w), lambda i, h: (i, 0, h))
    in_specs = [pl.BlockSpec(memory_space=pltpu.SMEM), head(RET_DK), head(RET_DK),
                head(RET_DV), head(RET_DV), pl.BlockSpec((1, RET_DV), lambda i, h: (0, h))]
    args = [log_g, rq, rk, rv, rg, gain.reshape(1, RET_V_W)]
    if state is not None:
        in_specs.append(pl.BlockSpec((seqs, None, 2, None, RET_DK, RET_DV),
                                     lambda i, h: (i, layer, 0, h, 0, 0)))
        args.append(state)
    out_specs = [head(RET_DV)]
    out_shape = [jax.ShapeDtypeStruct((b, l, RET_V_W), BF16)]
    state_layer = None
    if new_state is not None:
        state_layer, earlier = new_state
        state_spec = lambda n: pl.BlockSpec((seqs, n, 2, None, RET_DK, RET_DV),
                                            lambda i, h: (i, 0, 0, h, 0, 0))
        if state_layer:
            in_specs.append(state_spec(state_layer))
            args.append(earlier)
        out_specs.append(state_spec(state_layer + 1))
        out_shape.append(jax.ShapeDtypeStruct(
            (b, state_layer + 1, 2, RET_HEADS, RET_DK, RET_DV), F32))
    return pl.pallas_call(
        functools.partial(_retention_kernel, n_chunks=n_chunks, has_s0=state is not None,
                          state_layer=state_layer),
        grid=(b // seqs, RET_HEADS),
        in_specs=in_specs,
        out_specs=out_specs,
        out_shape=out_shape,
        scratch_shapes=[pltpu.VMEM((seqs, n_chunks, 2 * RET_DK, RET_DV), BF16)],
        compiler_params=_params("arbitrary", "arbitrary"),
        name="retention",
    )(*args)


def _fold_lanes(x, op):
    return functools.reduce(op, [x[:, c:c + LANES] for c in range(0, x.shape[1], LANES)])


def _pair_attention(q2, keys, values, biases):
    n_q = q2.shape[0]
    lane = lax.broadcasted_iota(jnp.int32, (1, LANES), 1)
    lo_mask = lane < NA_DH
    zero = jnp.zeros_like(q2)
    qq = jnp.concatenate([jnp.where(lo_mask, q2, zero), jnp.where(lo_mask, zero, q2)], axis=0)
    scores = []
    for kt, bt in zip(keys, biases):
        s = _dot_nt(qq, kt)
        scores.append(s if bt is None else s + bt)
    m = functools.reduce(jnp.maximum, [_fold_lanes(s, jnp.maximum) for s in scores])
    m = m.max(axis=-1, keepdims=True)
    denom = None
    acc = None
    for s, v in zip(scores, values):
        e = jnp.exp2(s - m)
        part = _fold_lanes(e, jnp.add)
        pv = _dot(e.astype(BF16), v)
        denom = part if denom is None else denom + part
        acc = pv if acc is None else acc + pv
    out = acc * (1.0 / denom.sum(axis=-1, keepdims=True))
    return jnp.where(lo_mask, out[:n_q], out[n_q:])


def _ctx_attn_kernel(q_ref, k_ref, v_ref, o_ref):
    for seq in range(q_ref.shape[0]):
        for hp in range(HEAD_PAIRS):
            sl = slice(hp * LANES, (hp + 1) * LANES)
            out = _pair_attention(q_ref[seq, :, sl], [k_ref[seq, :, sl].astype(BF16)],
                                  [v_ref[seq, :, sl].astype(BF16)], [None])
            o_ref[seq, :, sl] = out.astype(o_ref.dtype)


def _ctx_attention(nq, k_cache, v_cache, layer):
    b, l, _ = nq.shape
    seqs = max(1, SHORT_SEQ_TOKENS // l)
    assert b % seqs == 0
    spec = pl.BlockSpec((seqs, l, NA_W), lambda i: (i, 0, 0))
    kv_spec = pl.BlockSpec((seqs, None, l, NA_W), lambda i: (i, layer, 0, 0))
    return pl.pallas_call(
        _ctx_attn_kernel,
        grid=(b // seqs,),
        in_specs=[spec, kv_spec, kv_spec],
        out_specs=spec,
        out_shape=jax.ShapeDtypeStruct((b, l, NA_W), BF16),
        compiler_params=_params("arbitrary"),
        name="ctx_attention",
    )(nq, k_cache, v_cache)


def _na_window_start(j, rows):
    return int(np.clip(NA_Q_ROWS * j - NA_WIN_H // 2, 0, rows - NA_KEY_ROWS))


def _na_bias_rows(rpb):
    half = NA_WIN_W - 1
    gap = jnp.zeros(rpb.shape[:-1] + (LANES - rpb.shape[-1],), F32)
    return jnp.concatenate([rpb[..., half:], gap, rpb[..., :half]], axis=-1) * LOG2_E


def _na_block_index(rows):
    a = np.arange(NA_Q_ROWS)[:, None]
    kk = np.arange(NA_KEY_ROWS)[None, :]
    index = []
    for j in NA_PATTERNS:
        qr = NA_Q_ROWS * j + a
        kr = _na_window_start(j, rows) + kk
        rs = np.clip(qr - NA_WIN_H // 2, 0, rows - NA_WIN_H)
        row_valid = (kr >= rs) & (kr < rs + NA_WIN_H)
        index.append(np.where(row_valid, kr - qr + NA_WIN_H - 1, 2 * NA_WIN_H - 1))
    return np.stack(index)


def _na_kernel(q_ref, k_ref, v_ref, ck_ref, cv_ref, rows_ref, o_ref, tab_ref, blocks_ref,
               stage_ref, ck_scr, cv_scr, *, index):
    step = pl.program_id(1)
    n_blocks = pl.num_programs(1) * NA_BLOCKS_PER_STEP

    @pl.when((pl.program_id(0) == 0) & (step == 0))
    def _():
        heads, n_dr, _ = rows_ref.shape
        qc = lax.broadcasted_iota(jnp.int32, (GRID_W, LANES), 0)
        kc = lax.broadcasted_iota(jnp.int32, (GRID_W, LANES), 1)
        cs = jnp.clip(qc - NA_WIN_W // 2, 0, GRID_W - NA_WIN_W)
        col_valid = (kc >= cs) & (kc < cs + NA_WIN_W)
        for h in range(heads):
            for dr in range(n_dr):
                row = jnp.broadcast_to(rows_ref[h, dr:dr + 1, :], (GRID_W, LANES))
                skew = pltpu.roll(row, 0, 1, stride=1, stride_axis=0)
                blocks_ref[h, dr] = jnp.where(col_valid, skew, MASKED)
        blocks_ref[:, n_dr] = jnp.full((heads, GRID_W, LANES), MASKED, F32)
        for p, rows_p in enumerate(index):
            for a, row_a in enumerate(rows_p):
                for kk, blk in enumerate(row_a):
                    tab_ref[p, :, a * GRID_W:(a + 1) * GRID_W, kk * GRID_W:(kk + 1) * GRID_W] = (
                        blocks_ref[:, int(blk), :, 0:GRID_W])

    @pl.when(step == 0)
    def _():
        for src, dst in ((ck_ref, ck_scr), (cv_ref, cv_scr)):
            for h in range(NA_HEADS):
                stage_ref[:, h * NA_DH:(h + 1) * NA_DH] = src[:, h, :]
            dst[...] = stage_ref[...].astype(dst.dtype)

    for sub in range(NA_BLOCKS_PER_STEP):
        j = step * NA_BLOCKS_PER_STEP + sub
        rows = slice(sub * NA_Q_BLOCK, (sub + 1) * NA_Q_BLOCK)
        pattern = jnp.minimum(j, 2) + jnp.maximum(j - (n_blocks - 3), 0)
        start = jnp.clip(j - 2, 0, n_blocks - NA_KEY_BLOCKS) * NA_Q_BLOCK
        window = pl.ds(pl.multiple_of(start, NA_Q_BLOCK), NA_KEY_BLOCKS * LANES)
        for hp in range(HEAD_PAIRS):
            sl = slice(hp * LANES, (hp + 1) * LANES)
            bias = tab_ref[pattern, 2 * hp:2 * hp + 2].reshape(2 * NA_Q_BLOCK, -1)
            out = _pair_attention(q_ref[rows, sl], [k_ref[window, sl], ck_scr[:, sl]],
                                  [v_ref[window, sl], cv_scr[:, sl]], [bias, None])
            o_ref[rows, sl] = out.astype(o_ref.dtype)


def _na_attention(nq, nk, nv, ck, cv, bias_rows, layer):
    b, l, _ = nq.shape
    past = ck.shape[2]
    cached = pl.BlockSpec((None, None, past, NA_HEADS, NA_DH), lambda i, j: (i, layer, 0, 0, 0))
    n_blocks = l // NA_Q_BLOCK
    assert [min(j, 2) + max(j - (n_blocks - 3), 0) for j in NA_PATTERNS] == list(range(5))
    assert n_blocks % NA_BLOCKS_PER_STEP == 0
    tile = pl.BlockSpec((None, NA_BLOCKS_PER_STEP * NA_Q_BLOCK, NA_W), lambda i, j: (i, j, 0))
    whole = lambda a: pl.BlockSpec((None,) + a.shape[1:], lambda i, j: (i, 0, 0))
    table = (len(NA_PATTERNS), NA_HEADS, NA_Q_BLOCK, NA_KEY_ROWS * GRID_W)
    return pl.pallas_call(
        functools.partial(_na_kernel, index=_na_block_index(l // GRID_W)),
        grid=(b, n_blocks // NA_BLOCKS_PER_STEP),
        in_specs=[tile, whole(nk), whole(nv), cached, cached,
                  pl.BlockSpec((None,) + bias_rows.shape[1:], lambda i, j: (layer, 0, 0, 0),
                               pipeline_mode=pl.Buffered(1))],
        out_specs=tile,
        out_shape=jax.ShapeDtypeStruct((b, l, NA_W), BF16),
        scratch_shapes=[pltpu.VMEM(table, F32),
                        pltpu.VMEM((NA_HEADS, 2 * NA_WIN_H, GRID_W, LANES), F32),
                        pltpu.VMEM((past, NA_W), F32),
                        pltpu.VMEM((past, NA_W), BF16), pltpu.VMEM((past, NA_W), BF16)],
        compiler_params=_params("arbitrary", "arbitrary"),
        name="na_attention",
    )(nq, nk, nv, ck, cv, bias_rows)


def _tail_kernel(x_ref, mod_ref, yret_ref, yna_ref, gate_ref, n1_ref, n2_ref, n3_ref,
                 w_ret_ref, w_na_ref, w_o_ref, w_g_ref, w_u_ref, w_d_ref, o_ref):
    mod = mod_ref[0]
    g1 = mod[:, 2 * D_MODEL:3 * D_MODEL]
    sh2 = mod[:, 3 * D_MODEL:4 * D_MODEL]
    sc2 = mod[:, 4 * D_MODEL:5 * D_MODEL]
    g2 = mod[:, 5 * D_MODEL:6 * D_MODEL]
    g_ret = gate_ref[:, 0:D_MODEL].astype(F32)
    g_na = gate_ref[:, D_MODEL:2 * D_MODEL].astype(F32)
    branches = (g_ret * _dot(yret_ref[...], w_ret_ref[...])
                + g_na * _dot(yna_ref[...], w_na_ref[...]))
    mixed = _dot(branches.astype(BF16), w_o_ref[...])
    x = x_ref[...] + g1 * _rms(mixed, n1_ref[...])
    hf = (_rms(x, n2_ref[...]) * (1.0 + sc2) + sh2).astype(BF16)
    act = (_silu(_dot(hf, w_g_ref[...])) * _dot(hf, w_u_ref[...])).astype(BF16)
    o_ref[...] = x + g2 * _rms(_dot(act, w_d_ref[...]), n3_ref[...])


def _tail(x, mod, mod_row, y_ret, y_na, gates, n1, n2, n3, w_ret, w_na, w_o, w_g, w_u, w_d):
    b, l, _ = x.shape
    tm = min(TOKEN_TILE, l)
    tok = lambda w: pl.BlockSpec((None, tm, w), lambda i, j: (i, j, 0))
    vec = lambda a: a.reshape(1, D_MODEL)
    return pl.pallas_call(
        _tail_kernel,
        grid=(b, l // tm),
        in_specs=[
            tok(D_MODEL),
            pl.BlockSpec((1, 1, 6 * D_MODEL), lambda i, j: (mod_row(i), 0, 0)),
            tok(RET_V_W), tok(NA_W), tok(2 * D_MODEL),
            _resident((1, D_MODEL)), _resident((1, D_MODEL)), _resident((1, D_MODEL)),
            _resident(w_ret.shape), _resident(w_na.shape), _resident(w_o.shape),
            _resident(w_g.shape), _resident(w_u.shape), _resident(w_d.shape),
        ],
        out_specs=tok(D_MODEL),
        out_shape=jax.ShapeDtypeStruct((b, l, D_MODEL), F32),
        compiler_params=_params("arbitrary", "arbitrary"),
        name="tail",
    )(x, mod, y_ret, y_na, gates, vec(n1), vec(n2), vec(n3), w_ret, w_na, w_o, w_g, w_u, w_d)


def _rope_tables(l):
    t = jnp.arange(l)
    row = (t // GRID_W).astype(F32)
    col = (t % GRID_W).astype(F32)
    n_freq = RET_DK // 4
    inv = ROPE_BASE ** (-jnp.arange(n_freq, dtype=F32) / n_freq)
    ang = jnp.concatenate([row[:, None] * inv, col[:, None] * inv], axis=-1)
    cos = jnp.cos(ang)
    sin = jnp.sin(ang)
    return jnp.concatenate([cos, cos], axis=-1), jnp.concatenate([-sin, sin], axis=-1)


def kernel(x_prompt, x_sample, cache_na_k, cache_na_v, state_ret, c, c_ctx, w_ada, b_ada,
           norm_pre_mix, norm_post_mix, norm_pre_ffn, norm_post_ffn, w_in, ret_decay_logit,
           ret_gn_gain, na_rpb, w_ret_out, w_na_out, w_gate, w_o, w_ffn_gate, w_ffn_up,
           w_ffn_down):
    dec_b, dec_l, _ = x_sample.shape
    cvec = jnp.zeros((MOD_ROWS, D_MODEL), F32).at[0].set(c_ctx).at[1:1 + dec_b].set(c)
    mods = _adaln(cvec, w_ada, b_ada).reshape(DEPTH, MOD_ROWS, 1, 6 * D_MODEL)
    log_g = jax.nn.log_sigmoid(ret_decay_logit.astype(F32))
    rope_tabs = _rope_tables(dec_l)
    bf = lambda a: a.astype(BF16)
    layers = lambda w: [bf(w[l]) for l in range(DEPTH)]
    w_in_b, w_gate_b = layers(w_in), layers(w_gate)
    w_ret_b, w_na_b, w_o_b = layers(w_ret_out), layers(w_na_out), layers(w_o)
    w_g_b, w_u_b, w_d_b = layers(w_ffn_gate), layers(w_ffn_up), layers(w_ffn_down)
    bias_rows = _na_bias_rows(na_rpb.astype(F32))
    ctx_row = lambda i: 0
    dec_row = lambda i: i + 1

    def tail(x, l, row, y_ret, y_na, gates):
        return _tail(x, mods[l], row, y_ret, y_na, gates, norm_post_mix[l], norm_pre_ffn[l],
                     norm_post_ffn[l], w_ret_b[l], w_na_b[l], w_o_b[l], w_g_b[l], w_u_b[l],
                     w_d_b[l])

    ctx_b, ctx_l, _ = x_prompt.shape
    flat = lambda a: a.reshape(1, ctx_b * ctx_l, a.shape[-1])
    per_seq = lambda a: a.reshape(ctx_b, ctx_l, a.shape[-1])
    x = flat(x_prompt)
    new_k = new_v = new_s = None
    for l in range(DEPTH):
        rq, rk, rv, rg, nq, new_k, new_v, gates = _premix(
            x, mods[l], ctx_row, norm_pre_mix[l], w_in_b[l], w_gate_b[l],
            cache=(l, ctx_l, new_k, new_v))
        y_ret, new_s = _retention(log_g[l], per_seq(rq), per_seq(rk), per_seq(rv), per_seq(rg),
                                  ret_gn_gain[l], new_state=(l, new_s))
        y_na = _ctx_attention(per_seq(nq), new_k, new_v, l)
        x = tail(x, l, ctx_row, flat(y_ret), flat(y_na), gates)
    y_prompt = per_seq(x)
    new_k = new_k.reshape(ctx_b, DEPTH, ctx_l, NA_HEADS, NA_DH)
    new_v = new_v.reshape(ctx_b, DEPTH, ctx_l, NA_HEADS, NA_DH)

    x = x_sample
    for l in range(DEPTH):
        rq, rk, rv, rg, nq, nk, nv, gates = _premix(
            x, mods[l], dec_row, norm_pre_mix[l], w_in_b[l], w_gate_b[l], rope_tabs=rope_tabs)
        (y_ret,) = _retention(log_g[l], rq, rk, rv, rg, ret_gn_gain[l], state=state_ret, layer=l)
        y_na = _na_attention(nq, nk, nv, cache_na_k, cache_na_v, bias_rows, l)
        x = tail(x, l, dec_row, y_ret, y_na, gates)
    return (y_prompt, x, new_k, new_v, new_s)
```

```python
import functools
import math

import numpy as np
import jax
import jax.numpy as jnp
from jax import lax
from jax.experimental import pallas as pl
from jax.experimental.pallas import tpu as pltpu

F32 = jnp.float32
BF16 = jnp.bfloat16

D_MODEL = 1024
DEPTH = 2
GRID_W = 64
RET_HEADS = 4
RET_DK = 128
RET_DV = 256
RET_QK_W = RET_HEADS * RET_DK
RET_V_W = RET_HEADS * RET_DV
RET_CHUNK = 128
NA_HEADS = 8
NA_DH = 64
NA_W = NA_HEADS * NA_DH
NA_WIN_H = 8
NA_WIN_W = 16
D_FF = 2816
ROPE_BASE = 10000.0
EPS = 1e-6
IN_WIDTH = 2 * RET_QK_W + 2 * RET_V_W + 3 * NA_W

LANES = 128
HEAD_PAIRS = NA_W // LANES
MOD_ROWS = 16
NA_Q_ROWS = 2
NA_Q_BLOCK = NA_Q_ROWS * GRID_W
NA_KEY_BLOCKS = 5
NA_KEY_ROWS = NA_KEY_BLOCKS * LANES // GRID_W
NA_PATTERNS = (0, 1, 2, 14, 15)
NA_BLOCKS_PER_STEP = 4
MASKED = -1e30
LOG2_E = math.log2(math.e)
NA_Q_SCALE = NA_DH ** -0.5 * LOG2_E
VMEM_LIMIT = 56 * 1024 * 1024
TOKEN_TILE = 512
RET_HEADS_PER_STEP = 2
RET_UNROLL = 16
SHORT_SEQ_TOKENS = 2048


def _sigmoid(x):
    return 1.0 / (1.0 + jnp.exp(-x))


def _silu(x):
    return x * _sigmoid(x)


def _rms(x, g):
    return x * lax.rsqrt(jnp.mean(x * x, axis=-1, keepdims=True) + EPS) * g


def _dot(a, b):
    return jnp.dot(a, b, preferred_element_type=F32)


def _dot_nt(a, b):
    return lax.dot_general(a, b, (((1,), (1,)), ((), ())), preferred_element_type=F32)


def _resident(shape):
    zeros = (0,) * len(shape)
    return pl.BlockSpec(shape, lambda *_: zeros, pipeline_mode=pl.Buffered(1))


def _params(*sem):
    return pltpu.CompilerParams(dimension_semantics=sem, vmem_limit_bytes=VMEM_LIMIT)


def _adaln_kernel(c_ref, w_ref, b_ref, o_ref):
    s = _silu(c_ref[...]).astype(BF16)
    o_ref[0] = _dot(s, w_ref[0].astype(BF16)) + b_ref[0]


def _adaln(cvec, w_ada, b_ada):
    tn = 1536
    n = 6 * D_MODEL
    return pl.pallas_call(
        _adaln_kernel,
        grid=(DEPTH, n // tn),
        in_specs=[
            pl.BlockSpec((MOD_ROWS, D_MODEL), lambda l, j: (0, 0)),
            pl.BlockSpec((1, D_MODEL, tn), lambda l, j: (l, 0, j)),
            pl.BlockSpec((1, 1, tn), lambda l, j: (l, 0, j)),
        ],
        out_specs=pl.BlockSpec((1, MOD_ROWS, tn), lambda l, j: (l, 0, j)),
        out_shape=jax.ShapeDtypeStruct((DEPTH, MOD_ROWS, n), F32),
        compiler_params=_params("arbitrary", "arbitrary"),
        name="adaln",
    )(cvec, w_ada, b_ada.reshape(DEPTH, 1, n))


def _premix_kernel(*refs, rope, cache_layer):
    refs = list(refs)
    x_ref, mod_ref, g_ref, w_in_ref, w_gate_ref = refs[:5]
    pos = 5
    if rope:
        cos_ref, sin_ref = refs[pos:pos + 2]
        pos += 2
    if cache_layer:
        k_prev_ref, v_prev_ref = refs[pos:pos + 2]
        pos += 2
    rq_ref, rk_ref, rv_ref, rg_ref, nq_ref, nk_ref, nv_ref, gate_ref = refs[pos:]
    mod = mod_ref[0]
    sh1 = mod[:, 0:D_MODEL]
    sc1 = mod[:, D_MODEL:2 * D_MODEL]
    hm = (_rms(x_ref[...], g_ref[...]) * (1.0 + sc1) + sh1).astype(BF16)

    def proj(lo, width):
        return _dot(hm, w_in_ref[:, lo:lo + width])

    def rotary(t):
        if not rope:
            return t
        cos = cos_ref[...]
        sin = sin_ref[...]
        heads = []
        for h in range(RET_HEADS):
            blk = t[:, h * RET_DK:(h + 1) * RET_DK]
            heads.append(blk * cos + pltpu.roll(blk, RET_DK // 2, axis=1) * sin)
        return jnp.concatenate(heads, axis=1)

    def store_kv(ref, prev_ref, t):
        if cache_layer is None:
            ref[...] = t.astype(ref.dtype)
            return
        if cache_layer:
            ref[:, 0:cache_layer] = prev_ref[...]
        ref[:, cache_layer] = t.reshape(ref.shape[0], ref.shape[2], ref.shape[3])

    lo = 0
    rq_ref[...] = rotary(proj(lo, RET_QK_W)).astype(rq_ref.dtype)
    lo += RET_QK_W
    rk_ref[...] = (rotary(proj(lo, RET_QK_W)) * (RET_DK ** -0.5)).astype(rk_ref.dtype)
    lo += RET_QK_W
    rv_ref[...] = proj(lo, RET_V_W).astype(rv_ref.dtype)
    lo += RET_V_W
    rg_ref[...] = proj(lo, RET_V_W).astype(rg_ref.dtype)
    lo += RET_V_W
    nq_ref[...] = (proj(lo, NA_W) * NA_Q_SCALE).astype(nq_ref.dtype)
    lo += NA_W
    store_kv(nk_ref, k_prev_ref if cache_layer else None, proj(lo, NA_W))
    lo += NA_W
    store_kv(nv_ref, v_prev_ref if cache_layer else None, proj(lo, NA_W))
    gate_ref[...] = _sigmoid(_dot(hm, w_gate_ref[...])).astype(gate_ref.dtype)


def _premix(x, mod, mod_row, g, w_in, w_gate, rope_tabs=None, cache=None):
    b, l, _ = x.shape
    tm = min(TOKEN_TILE, l)
    tok = lambda w: pl.BlockSpec((None, tm, w), lambda i, j: (i, j, 0))
    in_specs = [
        tok(D_MODEL),
        pl.BlockSpec((1, 1, 6 * D_MODEL), lambda i, j: (mod_row(i), 0, 0)),
        _resident((1, D_MODEL)),
        _resident((D_MODEL, IN_WIDTH)),
        _resident((D_MODEL, 2 * D_MODEL)),
    ]
    args = [x, mod, g.reshape(1, D_MODEL), w_in, w_gate]
    if rope_tabs is not None:
        in_specs += [pl.BlockSpec((tm, RET_DK), lambda i, j: (j, 0))] * 2
        args += list(rope_tabs)
    widths = (RET_QK_W, RET_QK_W, RET_V_W, RET_V_W, NA_W, NA_W, NA_W, 2 * D_MODEL)
    out_specs = [tok(w) for w in widths]
    out_shape = [jax.ShapeDtypeStruct((b, l, w), BF16) for w in widths]
    cache_layer = None
    if cache is not None:
        cache_layer, seq, k_prev, v_prev = cache
        assert b == 1 and tm % seq == 0
        per_tile = tm // seq
        kv_spec = lambda n: pl.BlockSpec((per_tile, n, seq, NA_W), lambda i, j: (j, 0, 0, 0))
        if cache_layer:
            in_specs += [kv_spec(cache_layer)] * 2
            args += [k_prev, v_prev]
        kv_shape = jax.ShapeDtypeStruct((l // seq, cache_layer + 1, seq, NA_W), F32)
        out_specs[5:7] = [kv_spec(cache_layer + 1)] * 2
        out_shape[5:7] = [kv_shape] * 2
    return pl.pallas_call(
        functools.partial(_premix_kernel, rope=rope_tabs is not None, cache_layer=cache_layer),
        grid=(b, l // tm),
        in_specs=in_specs,
        out_specs=out_specs,
        out_shape=out_shape,
        compiler_params=_params("arbitrary", "arbitrary"),
        name="premix",
    )(*args)


def _retention_kernel(*refs, n_chunks, has_s0, state_layer):
    refs = list(refs)
    lg_ref, q_ref, k_ref, v_ref, rg_ref, gain_ref = refs[:6]
    pos = 6
    if has_s0:
        s0_ref = refs[pos]
        pos += 1
    if state_layer:
        sprev_ref = refs[pos]
        pos += 1
    y_ref = refs[pos]
    pos += 1
    if state_layer is not None:
        sout_ref = refs[pos]
        pos += 1
    s_scr = refs[pos]

    c = RET_CHUNK
    heads = gain_ref.shape[1] // RET_DV
    row = lax.broadcasted_iota(jnp.int32, (c, c), 0).astype(F32)
    col = lax.broadcasted_iota(jnp.int32, (c, c), 1).astype(F32)
    rel = row - col
    p = lax.broadcasted_iota(jnp.int32, (c, 1), 0).astype(F32)
    chunk_len = jnp.full((1, RET_DV), float(c), F32)
    unroll = min(n_chunks, RET_UNROLL)

    def chunk(ref, seq, i, hd, width):
        return ref[seq, pl.ds(pl.multiple_of(i * c, c), c), hd * width:(hd + 1) * width]

    def run(seq, hd, lgf, lgb):
        decay = (jnp.where(rel >= 0, jnp.exp(lgf * jnp.maximum(rel, 0.0)), 0.0)
                 + jnp.where(rel <= 0, jnp.exp(lgb * jnp.maximum(-rel, 0.0)), 0.0))
        xi_f = jnp.exp(lgf * (p + 1.0))
        xi_b = jnp.exp(lgb * (c - p))
        zeta_f = jnp.exp(lgf * (c - 1.0 - p))
        zeta_b = jnp.exp(lgb * p)
        g_f = jnp.exp(lgf * chunk_len)
        g_b = jnp.exp(lgb * chunk_len)
        gain = gain_ref[:, hd * RET_DV:(hd + 1) * RET_DV]
        slot = seq * heads + hd

        def outer_kv(i, zeta):
            kz = (chunk(k_ref, seq, i, hd, RET_DK).astype(F32) * zeta).T.astype(BF16)
            return _dot(kz, chunk(v_ref, seq, i, hd, RET_DV))

        if has_s0:
            s_f0 = s0_ref[seq, 0, hd]
            s_b0 = s0_ref[seq, 1, hd]
        else:
            s_f0 = jnp.zeros((RET_DK, RET_DV), F32)
            s_b0 = s_f0

        def scan(t, carry):
            s_f, s_b = carry
            i_b = n_chunks - 1 - t
            s_scr[slot, t, 0:RET_DK, :] = s_f.astype(BF16)
            s_scr[slot, i_b, RET_DK:2 * RET_DK, :] = s_b.astype(BF16)
            return (g_f * s_f + outer_kv(t, zeta_f), g_b * s_b + outer_kv(i_b, zeta_b))

        s_f, s_b = lax.fori_loop(0, n_chunks, scan, (s_f0, s_b0), unroll=unroll)
        if state_layer is not None:
            if state_layer:
                sout_ref[seq, 0:state_layer, :, hd] = sprev_ref[seq, :, :, hd]
            sout_ref[seq, state_layer, 0, hd] = s_f
            sout_ref[seq, state_layer, 1, hd] = s_b

        def emit(i, _):
            qi = chunk(q_ref, seq, i, hd, RET_DK)
            scores = (_dot_nt(qi, chunk(k_ref, seq, i, hd, RET_DK)) * decay).astype(BF16)
            qf = qi.astype(F32)
            qx = jnp.concatenate([(qf * xi_f).astype(BF16), (qf * xi_b).astype(BF16)], axis=1)
            y = _dot(scores, chunk(v_ref, seq, i, hd, RET_DV)) + _dot(qx, s_scr[slot, i])
            mu = jnp.mean(y, axis=-1, keepdims=True)
            d = y - mu
            var = jnp.mean(d * d, axis=-1, keepdims=True)
            yn = d * lax.rsqrt(var + EPS) * gain
            out = _silu(chunk(rg_ref, seq, i, hd, RET_DV).astype(F32)) * yn
            y_ref[seq, pl.ds(pl.multiple_of(i * c, c), c),
                  hd * RET_DV:(hd + 1) * RET_DV] = out.astype(y_ref.dtype)
            return 0

        lax.fori_loop(0, n_chunks, emit, 0, unroll=unroll)

    for hd in range(heads):
        h = pl.program_id(1) * heads + hd
        for seq in range(q_ref.shape[0]):
            run(seq, hd, lg_ref[0, h], lg_ref[1, h])


def _retention(log_g, rq, rk, rv, rg, gain, state=None, layer=0, new_state=None):
    b, l, _ = rq.shape
    n_chunks = l // RET_CHUNK
    seqs = max(1, SHORT_SEQ_TOKENS // l)
    assert b % seqs == 0
    hps = RET_HEADS_PER_STEP
    head = lambda w: pl.BlockSpec((seqs, l, hps * w), lambda i, h: (i, 0, h))
    in_specs = [pl.BlockSpec(memory_space=pltpu.SMEM), head(RET_DK), head(RET_DK),
                head(RET_DV), head(RET_DV),
                pl.BlockSpec((1, hps * RET_DV), lambda i, h: (0, h))]
    args = [log_g, rq, rk, rv, rg, gain.reshape(1, RET_V_W)]
    if state is not None:
        in_specs.append(pl.BlockSpec((seqs, None, 2, hps, RET_DK, RET_DV),
                                     lambda i, h: (i, layer, 0, h, 0, 0)))
        args.append(state)
    out_specs = [head(RET_DV)]
    out_shape = [jax.ShapeDtypeStruct((b, l, RET_V_W), BF16)]
    state_layer = None
    if new_state is not None:
        state_layer, earlier = new_state
        state_spec = lambda n: pl.BlockSpec((seqs, n, 2, hps, RET_DK, RET_DV),
                                            lambda i, h: (i, 0, 0, h, 0, 0))
        if state_layer:
            in_specs.append(state_spec(state_layer))
            args.append(earlier)
        out_specs.append(state_spec(state_layer + 1))
        out_shape.append(jax.ShapeDtypeStruct(
            (b, state_layer + 1, 2, RET_HEADS, RET_DK, RET_DV), F32))
    return pl.pallas_call(
        functools.partial(_retention_kernel, n_chunks=n_chunks, has_s0=state is not None,
                          state_layer=state_layer),
        grid=(b // seqs, RET_HEADS // hps),
        in_specs=in_specs,
        out_specs=out_specs,
        out_shape=out_shape,
        scratch_shapes=[pltpu.VMEM((seqs * hps, n_chunks, 2 * RET_DK, RET_DV), BF16)],
        compiler_params=_params("arbitrary", "arbitrary"),
        name="retention",
    )(*args)


def _fold_lanes(x, op):
    return functools.reduce(op, [x[:, c:c + LANES] for c in range(0, x.shape[1], LANES)])


def _pair_attention(q2, keys, values, biases):
    n_q = q2.shape[0]
    lane = lax.broadcasted_iota(jnp.int32, (1, LANES), 1)
    lo_mask = lane < NA_DH
    zero = jnp.zeros_like(q2)
    qq = jnp.concatenate([jnp.where(lo_mask, q2, zero), jnp.where(lo_mask, zero, q2)], axis=0)
    scores = []
    for kt, bt in zip(keys, biases):
        s = _dot_nt(qq, kt)
        scores.append(s if bt is None else s + bt)
    m = functools.reduce(jnp.maximum, [_fold_lanes(s, jnp.maximum) for s in scores])
    m = m.max(axis=-1, keepdims=True)
    denom = None
    acc = None
    for s, v in zip(scores, values):
        e = jnp.exp2(s - m)
        part = _fold_lanes(e, jnp.add)
        pv = _dot(e.astype(BF16), v)
        denom = part if denom is None else denom + part
        acc = pv if acc is None else acc + pv
    out = acc * (1.0 / denom.sum(axis=-1, keepdims=True))
    return jnp.where(lo_mask, out[:n_q], out[n_q:])


def _ctx_attn_kernel(q_ref, k_ref, v_ref, o_ref):
    for seq in range(q_ref.shape[0]):
        for hp in range(HEAD_PAIRS):
            sl = slice(hp * LANES, (hp + 1) * LANES)
            out = _pair_attention(q_ref[seq, :, sl], [k_ref[seq, :, sl].astype(BF16)],
                                  [v_ref[seq, :, sl].astype(BF16)], [None])
            o_ref[seq, :, sl] = out.astype(o_ref.dtype)


def _ctx_attention(nq, k_cache, v_cache, layer):
    b, l, _ = nq.shape
    seqs = max(1, SHORT_SEQ_TOKENS // l)
    assert b % seqs == 0
    spec = pl.BlockSpec((seqs, l, NA_W), lambda i: (i, 0, 0))
    kv_spec = pl.BlockSpec((seqs, None, l, NA_W), lambda i: (i, layer, 0, 0))
    return pl.pallas_call(
        _ctx_attn_kernel,
        grid=(b // seqs,),
        in_specs=[spec, kv_spec, kv_spec],
        out_specs=spec,
        out_shape=jax.ShapeDtypeStruct((b, l, NA_W), BF16),
        compiler_params=_params("arbitrary"),
        name="ctx_attention",
    )(nq, k_cache, v_cache)


def _na_window_start(j, rows):
    return int(np.clip(NA_Q_ROWS * j - NA_WIN_H // 2, 0, rows - NA_KEY_ROWS))


def _na_bias_rows(rpb):
    half = NA_WIN_W - 1
    gap = jnp.zeros(rpb.shape[:-1] + (LANES - rpb.shape[-1],), F32)
    return jnp.concatenate([rpb[..., half:], gap, rpb[..., :half]], axis=-1) * LOG2_E


def _na_block_index(rows):
    a = np.arange(NA_Q_ROWS)[:, None]
    kk = np.arange(NA_KEY_ROWS)[None, :]
    index = []
    for j in NA_PATTERNS:
        qr = NA_Q_ROWS * j + a
        kr = _na_window_start(j, rows) + kk
        rs = np.clip(qr - NA_WIN_H // 2, 0, rows - NA_WIN_H)
        row_valid = (kr >= rs) & (kr < rs + NA_WIN_H)
        index.append(np.where(row_valid, kr - qr + NA_WIN_H - 1, 2 * NA_WIN_H - 1))
    return np.stack(index)


def _na_kernel(q_ref, k_ref, v_ref, ck_ref, cv_ref, rows_ref, o_ref, tab_ref, blocks_ref, *,
               index):
    step = pl.program_id(1)
    n_blocks = pl.num_programs(1) * NA_BLOCKS_PER_STEP

    @pl.when((pl.program_id(0) == 0) & (step == 0))
    def _():
        heads, n_dr, _ = rows_ref.shape
        qc = lax.broadcasted_iota(jnp.int32, (GRID_W, LANES), 0)
        kc = lax.broadcasted_iota(jnp.int32, (GRID_W, LANES), 1)
        cs = jnp.clip(qc - NA_WIN_W // 2, 0, GRID_W - NA_WIN_W)
        col_valid = (kc >= cs) & (kc < cs + NA_WIN_W)
        for h in range(heads):
            for dr in range(n_dr):
                row = jnp.broadcast_to(rows_ref[h, dr:dr + 1, :], (GRID_W, LANES))
                skew = pltpu.roll(row, 0, 1, stride=1, stride_axis=0)
                blocks_ref[h, dr] = jnp.where(col_valid, skew, MASKED)
        blocks_ref[:, n_dr] = jnp.full((heads, GRID_W, LANES), MASKED, F32)
        for p, rows_p in enumerate(index):
            for a, row_a in enumerate(rows_p):
                for kk, blk in enumerate(row_a):
                    tab_ref[p, :, a * GRID_W:(a + 1) * GRID_W, kk * GRID_W:(kk + 1) * GRID_W] = (
                        blocks_ref[:, int(blk), :, 0:GRID_W])

    for sub in range(NA_BLOCKS_PER_STEP):
        j = step * NA_BLOCKS_PER_STEP + sub
        rows = slice(sub * NA_Q_BLOCK, (sub + 1) * NA_Q_BLOCK)
        pattern = jnp.minimum(j, 2) + jnp.maximum(j - (n_blocks - 3), 0)
        start = jnp.clip(j - 2, 0, n_blocks - NA_KEY_BLOCKS) * NA_Q_BLOCK
        window = pl.ds(pl.multiple_of(start, NA_Q_BLOCK), NA_KEY_BLOCKS * LANES)
        for hp in range(HEAD_PAIRS):
            sl = slice(hp * LANES, (hp + 1) * LANES)
            bias = tab_ref[pattern, 2 * hp:2 * hp + 2].reshape(2 * NA_Q_BLOCK, -1)
            out = _pair_attention(q_ref[rows, sl], [k_ref[window, sl], ck_ref[:, sl]],
                                  [v_ref[window, sl], cv_ref[:, sl]], [bias, None])
            o_ref[rows, sl] = out.astype(o_ref.dtype)


def _na_attention(nq, nk, nv, ck, cv, bias_rows, layer):
    b, l, _ = nq.shape
    n_blocks = l // NA_Q_BLOCK
    assert [min(j, 2) + max(j - (n_blocks - 3), 0) for j in NA_PATTERNS] == list(range(5))
    assert n_blocks % NA_BLOCKS_PER_STEP == 0
    tile = pl.BlockSpec((None, NA_BLOCKS_PER_STEP * NA_Q_BLOCK, NA_W), lambda i, j: (i, j, 0))
    whole = lambda a: pl.BlockSpec((None,) + a.shape[1:], lambda i, j: (i, 0, 0))
    table = (len(NA_PATTERNS), NA_HEADS, NA_Q_BLOCK, NA_KEY_ROWS * GRID_W)
    return pl.pallas_call(
        functools.partial(_na_kernel, index=_na_block_index(l // GRID_W)),
        grid=(b, n_blocks // NA_BLOCKS_PER_STEP),
        in_specs=[tile, whole(nk), whole(nv), whole(ck), whole(cv),
                  pl.BlockSpec((None,) + bias_rows.shape[1:], lambda i, j: (layer, 0, 0, 0),
                               pipeline_mode=pl.Buffered(1))],
        out_specs=tile,
        out_shape=jax.ShapeDtypeStruct((b, l, NA_W), BF16),
        scratch_shapes=[pltpu.VMEM(table, F32),
                        pltpu.VMEM((NA_HEADS, 2 * NA_WIN_H, GRID_W, LANES), F32)],
        compiler_params=_params("arbitrary", "arbitrary"),
        name="na_attention",
    )(nq, nk, nv, ck, cv, bias_rows)


def _tail_kernel(x_ref, mod_ref, yret_ref, yna_ref, gate_ref, n1_ref, n2_ref, n3_ref,
                 w_ret_ref, w_na_ref, w_o_ref, w_g_ref, w_u_ref, w_d_ref, o_ref):
    mod = mod_ref[0]
    g1 = mod[:, 2 * D_MODEL:3 * D_MODEL]
    sh2 = mod[:, 3 * D_MODEL:4 * D_MODEL]
    sc2 = mod[:, 4 * D_MODEL:5 * D_MODEL]
    g2 = mod[:, 5 * D_MODEL:6 * D_MODEL]
    g_ret = gate_ref[:, 0:D_MODEL].astype(F32)
    g_na = gate_ref[:, D_MODEL:2 * D_MODEL].astype(F32)
    branches = (g_ret * _dot(yret_ref[...], w_ret_ref[...])
                + g_na * _dot(yna_ref[...], w_na_ref[...]))
    mixed = _dot(branches.astype(BF16), w_o_ref[...])
    x = x_ref[...] + g1 * _rms(mixed, n1_ref[...])
    hf = (_rms(x, n2_ref[...]) * (1.0 + sc2) + sh2).astype(BF16)
    act = (_silu(_dot(hf, w_g_ref[...])) * _dot(hf, w_u_ref[...])).astype(BF16)
    o_ref[...] = x + g2 * _rms(_dot(act, w_d_ref[...]), n3_ref[...])


def _tail(x, mod, mod_row, y_ret, y_na, gates, n1, n2, n3, w_ret, w_na, w_o, w_g, w_u, w_d):
    b, l, _ = x.shape
    tm = min(TOKEN_TILE, l)
    tok = lambda w: pl.BlockSpec((None, tm, w), lambda i, j: (i, j, 0))
    vec = lambda a: a.reshape(1, D_MODEL)
    return pl.pallas_call(
        _tail_kernel,
        grid=(b, l // tm),
        in_specs=[
            tok(D_MODEL),
            pl.BlockSpec((1, 1, 6 * D_MODEL), lambda i, j: (mod_row(i), 0, 0)),
            tok(RET_V_W), tok(NA_W), tok(2 * D_MODEL),
            _resident((1, D_MODEL)), _resident((1, D_MODEL)), _resident((1, D_MODEL)),
            _resident(w_ret.shape), _resident(w_na.shape), _resident(w_o.shape),
            _resident(w_g.shape), _resident(w_u.shape), _resident(w_d.shape),
        ],
        out_specs=tok(D_MODEL),
        out_shape=jax.ShapeDtypeStruct((b, l, D_MODEL), F32),
        compiler_params=_params("arbitrary", "arbitrary"),
        name="tail",
    )(x, mod, y_ret, y_na, gates, vec(n1), vec(n2), vec(n3), w_ret, w_na, w_o, w_g, w_u, w_d)


def _rope_tables(l):
    t = jnp.arange(l)
    row = (t // GRID_W).astype(F32)
    col = (t % GRID_W).astype(F32)
    n_freq = RET_DK // 4
    inv = ROPE_BASE ** (-jnp.arange(n_freq, dtype=F32) / n_freq)
    ang = jnp.concatenate([row[:, None] * inv, col[:, None] * inv], axis=-1)
    cos = jnp.cos(ang)
    sin = jnp.sin(ang)
    return jnp.concatenate([cos, cos], axis=-1), jnp.concatenate([-sin, sin], axis=-1)


def kernel(x_prompt, x_sample, cache_na_k, cache_na_v, state_ret, c, c_ctx, w_ada, b_ada,
           norm_pre_mix, norm_post_mix, norm_pre_ffn, norm_post_ffn, w_in, ret_decay_logit,
           ret_gn_gain, na_rpb, w_ret_out, w_na_out, w_gate, w_o, w_ffn_gate, w_ffn_up,
           w_ffn_down):
    dec_b, dec_l, _ = x_sample.shape
    past = cache_na_k.shape[2]
    cvec = jnp.zeros((MOD_ROWS, D_MODEL), F32).at[0].set(c_ctx).at[1:1 + dec_b].set(c)
    mods = _adaln(cvec, w_ada, b_ada).reshape(DEPTH, MOD_ROWS, 1, 6 * D_MODEL)
    log_g = jax.nn.log_sigmoid(ret_decay_logit.astype(F32))
    rope_tabs = _rope_tables(dec_l)
    bf = lambda a: a.astype(BF16)
    layers = lambda w: [bf(w[l]) for l in range(DEPTH)]
    w_in_b, w_gate_b = layers(w_in), layers(w_gate)
    w_ret_b, w_na_b, w_o_b = layers(w_ret_out), layers(w_na_out), layers(w_o)
    w_g_b, w_u_b, w_d_b = layers(w_ffn_gate), layers(w_ffn_up), layers(w_ffn_down)
    ck = [bf(cache_na_k[:, l]).reshape(dec_b, past, NA_W) for l in range(DEPTH)]
    cv = [bf(cache_na_v[:, l]).reshape(dec_b, past, NA_W) for l in range(DEPTH)]
    bias_rows = _na_bias_rows(na_rpb.astype(F32))
    ctx_row = lambda i: 0
    dec_row = lambda i: i + 1

    def tail(x, l, row, y_ret, y_na, gates):
        return _tail(x, mods[l], row, y_ret, y_na, gates, norm_post_mix[l], norm_pre_ffn[l],
                     norm_post_ffn[l], w_ret_b[l], w_na_b[l], w_o_b[l], w_g_b[l], w_u_b[l],
                     w_d_b[l])

    ctx_b, ctx_l, _ = x_prompt.shape
    flat = lambda a: a.reshape(1, ctx_b * ctx_l, a.shape[-1])
    per_seq = lambda a: a.reshape(ctx_b, ctx_l, a.shape[-1])
    x = flat(x_prompt)
    new_k = new_v = new_s = None
    for l in range(DEPTH):
        rq, rk, rv, rg, nq, new_k, new_v, gates = _premix(
            x, mods[l], ctx_row, norm_pre_mix[l], w_in_b[l], w_gate_b[l],
            cache=(l, ctx_l, new_k, new_v))
        y_ret, new_s = _retention(log_g[l], per_seq(rq), per_seq(rk), per_seq(rv), per_seq(rg),
                                  ret_gn_gain[l], new_state=(l, new_s))
        y_na = _ctx_attention(per_seq(nq), new_k, new_v, l)
        x = tail(x, l, ctx_row, flat(y_ret), flat(y_na), gates)
    y_prompt = per_seq(x)
    new_k = new_k.reshape(ctx_b, DEPTH, ctx_l, NA_HEADS, NA_DH)
    new_v = new_v.reshape(ctx_b, DEPTH, ctx_l, NA_HEADS, NA_DH)

    x = x_sample
    for l in range(DEPTH):
        rq, rk, rv, rg, nq, nk, nv, gates = _premix(
            x, mods[l], dec_row, norm_pre_mix[l], w_in_b[l], w_gate_b[l], rope_tabs=rope_tabs)
        (y_ret,) = _retention(log_g[l], rq, rk, rv, rg, ret_gn_gain[l], state=state_ret, layer=l)
        y_na = _na_attention(nq, nk, nv, ck[l], cv[l], bias_rows, l)
        x = tail(x, l, dec_row, y_ret, y_na, gates)
    return (y_prompt, x, new_k, new_v, new_s)
```

```python
import functools
import math

import numpy as np
import jax
import jax.numpy as jnp
from jax import lax
from jax.experimental import pallas as pl
from jax.experimental.pallas import tpu as pltpu

F32 = jnp.float32
BF16 = jnp.bfloat16

D_MODEL = 1024
DEPTH = 2
GRID_W = 64
RET_HEADS = 4
RET_DK = 128
RET_DV = 256
RET_QK_W = RET_HEADS * RET_DK
RET_V_W = RET_HEADS * RET_DV
RET_CHUNK = 128
NA_HEADS = 8
NA_DH = 64
NA_W = NA_HEADS * NA_DH
NA_WIN_H = 8
NA_WIN_W = 16
D_FF = 2816
ROPE_BASE = 10000.0
EPS = 1e-6
IN_WIDTH = 2 * RET_QK_W + 2 * RET_V_W + 3 * NA_W

LANES = 128
HEAD_PAIRS = NA_W // LANES
MOD_ROWS = 16
NA_Q_ROWS = 2
NA_Q_BLOCK = NA_Q_ROWS * GRID_W
NA_KEY_BLOCKS = 5
NA_KEY_ROWS = NA_KEY_BLOCKS * LANES // GRID_W
NA_PATTERNS = (0, 1, 2, 14, 15)
NA_BLOCKS_PER_STEP = 4
MASKED = -1e30
LOG2_E = math.log2(math.e)
NA_Q_SCALE = NA_DH ** -0.5 * LOG2_E
VMEM_LIMIT = 56 * 1024 * 1024
TOKEN_TILE = 512
RET_HEADS_PER_STEP = 2
RET_UNROLL = 16
SHORT_SEQ_TOKENS = 2048


def _sigmoid(x):
    return 1.0 / (1.0 + jnp.exp(-x))


def _silu(x):
    return x * _sigmoid(x)


def _rms(x, g):
    return x * lax.rsqrt(jnp.mean(x * x, axis=-1, keepdims=True) + EPS) * g


def _dot(a, b):
    return jnp.dot(a, b, preferred_element_type=F32)


def _dot_nt(a, b):
    return lax.dot_general(a, b, (((1,), (1,)), ((), ())), preferred_element_type=F32)


def _resident(shape):
    zeros = (0,) * len(shape)
    return pl.BlockSpec(shape, lambda *_: zeros, pipeline_mode=pl.Buffered(1))


def _params(*sem):
    return pltpu.CompilerParams(dimension_semantics=sem, vmem_limit_bytes=VMEM_LIMIT)


def _adaln_kernel(c_ref, w_ref, b_ref, o_ref):
    s = _silu(c_ref[...]).astype(BF16)
    o_ref[0] = _dot(s, w_ref[0].astype(BF16)) + b_ref[0]


def _adaln(cvec, w_ada, b_ada):
    tn = 1536
    n = 6 * D_MODEL
    return pl.pallas_call(
        _adaln_kernel,
        grid=(DEPTH, n // tn),
        in_specs=[
            pl.BlockSpec((MOD_ROWS, D_MODEL), lambda l, j: (0, 0)),
            pl.BlockSpec((1, D_MODEL, tn), lambda l, j: (l, 0, j)),
            pl.BlockSpec((1, 1, tn), lambda l, j: (l, 0, j)),
        ],
        out_specs=pl.BlockSpec((1, MOD_ROWS, tn), lambda l, j: (l, 0, j)),
        out_shape=jax.ShapeDtypeStruct((DEPTH, MOD_ROWS, n), F32),
        compiler_params=_params("arbitrary", "arbitrary"),
        name="adaln",
    )(cvec, w_ada, b_ada.reshape(DEPTH, 1, n))


def _premix_kernel(*refs, rope, cache_layer):
    refs = list(refs)
    x_ref, mod_ref, g_ref, w_in_ref, w_gate_ref = refs[:5]
    pos = 5
    if rope:
        cos_ref, sin_ref = refs[pos:pos + 2]
        pos += 2
    if cache_layer:
        k_prev_ref, v_prev_ref = refs[pos:pos + 2]
        pos += 2
    rq_ref, rk_ref, rv_ref, rg_ref, nq_ref, nk_ref, nv_ref, gate_ref = refs[pos:]
    mod = mod_ref[0]
    sh1 = mod[:, 0:D_MODEL]
    sc1 = mod[:, D_MODEL:2 * D_MODEL]
    hm = (_rms(x_ref[...], g_ref[...]) * (1.0 + sc1) + sh1).astype(BF16)

    def proj(lo, width):
        return _dot(hm, w_in_ref[:, lo:lo + width])

    def rotary(t):
        if not rope:
            return t
        cos = cos_ref[...]
        sin = sin_ref[...]
        heads = []
        for h in range(RET_HEADS):
            blk = t[:, h * RET_DK:(h + 1) * RET_DK]
            heads.append(blk * cos + pltpu.roll(blk, RET_DK // 2, axis=1) * sin)
        return jnp.concatenate(heads, axis=1)

    def store_kv(ref, prev_ref, t):
        if cache_layer is None:
            ref[...] = t.astype(ref.dtype)
            return
        if cache_layer:
            ref[:, 0:cache_layer] = prev_ref[...]
        ref[:, cache_layer] = t.reshape(ref.shape[0], ref.shape[2], ref.shape[3])

    lo = 0
    rq_ref[...] = rotary(proj(lo, RET_QK_W)).astype(rq_ref.dtype)
    lo += RET_QK_W
    rk_ref[...] = (rotary(proj(lo, RET_QK_W)) * (RET_DK ** -0.5)).astype(rk_ref.dtype)
    lo += RET_QK_W
    rv_ref[...] = proj(lo, RET_V_W).astype(rv_ref.dtype)
    lo += RET_V_W
    rg_ref[...] = proj(lo, RET_V_W).astype(rg_ref.dtype)
    lo += RET_V_W
    nq_ref[...] = (proj(lo, NA_W) * NA_Q_SCALE).astype(nq_ref.dtype)
    lo += NA_W
    store_kv(nk_ref, k_prev_ref if cache_layer else None, proj(lo, NA_W))
    lo += NA_W
    store_kv(nv_ref, v_prev_ref if cache_layer else None, proj(lo, NA_W))
    gate_ref[...] = _sigmoid(_dot(hm, w_gate_ref[...])).astype(gate_ref.dtype)


def _premix(x, mod, mod_row, g, w_in, w_gate, rope_tabs=None, cache=None):
    b, l, _ = x.shape
    tm = min(TOKEN_TILE, l)
    tok = lambda w: pl.BlockSpec((None, tm, w), lambda i, j: (i, j, 0))
    in_specs = [
        tok(D_MODEL),
        pl.BlockSpec((1, 1, 6 * D_MODEL), lambda i, j: (mod_row(i), 0, 0)),
        _resident((1, D_MODEL)),
        _resident((D_MODEL, IN_WIDTH)),
        _resident((D_MODEL, 2 * D_MODEL)),
    ]
    args = [x, mod, g.reshape(1, D_MODEL), w_in, w_gate]
    if rope_tabs is not None:
        in_specs += [pl.BlockSpec((tm, RET_DK), lambda i, j: (j, 0))] * 2
        args += list(rope_tabs)
    widths = (RET_QK_W, RET_QK_W, RET_V_W, RET_V_W, NA_W, NA_W, NA_W, 2 * D_MODEL)
    out_specs = [tok(w) for w in widths]
    out_shape = [jax.ShapeDtypeStruct((b, l, w), BF16) for w in widths]
    cache_layer = None
    if cache is not None:
        cache_layer, seq, k_prev, v_prev = cache
        assert b == 1 and tm % seq == 0
        per_tile = tm // seq
        kv_spec = lambda n: pl.BlockSpec((per_tile, n, seq, NA_W), lambda i, j: (j, 0, 0, 0))
        if cache_layer:
            in_specs += [kv_spec(cache_layer)] * 2
            args += [k_prev, v_prev]
        kv_shape = jax.ShapeDtypeStruct((l // seq, cache_layer + 1, seq, NA_W), F32)
        out_specs[5:7] = [kv_spec(cache_layer + 1)] * 2
        out_shape[5:7] = [kv_shape] * 2
    return pl.pallas_call(
        functools.partial(_premix_kernel, rope=rope_tabs is not None, cache_layer=cache_layer),
        grid=(b, l // tm),
        in_specs=in_specs,
        out_specs=out_specs,
        out_shape=out_shape,
        compiler_params=_params("arbitrary", "arbitrary"),
        name="premix",
    )(*args)


def _retention_kernel(*refs, n_chunks, has_s0, state_layer):
    refs = list(refs)
    lg_ref, q_ref, k_ref, v_ref, gain_ref = refs[:5]
    pos = 5
    if has_s0:
        s0_ref = refs[pos]
        pos += 1
    if state_layer:
        sprev_ref = refs[pos]
        pos += 1
    y_ref = refs[pos]
    pos += 1
    if state_layer is not None:
        sout_ref = refs[pos]
        pos += 1
    s_scr = refs[pos]

    c = RET_CHUNK
    heads = gain_ref.shape[1] // RET_DV
    row = lax.broadcasted_iota(jnp.int32, (c, c), 0).astype(F32)
    col = lax.broadcasted_iota(jnp.int32, (c, c), 1).astype(F32)
    rel = row - col
    p = lax.broadcasted_iota(jnp.int32, (c, 1), 0).astype(F32)
    chunk_len = jnp.full((1, RET_DV), float(c), F32)
    unroll = min(n_chunks, RET_UNROLL)

    def chunk(ref, seq, i, hd, width):
        return ref[seq, pl.ds(pl.multiple_of(i * c, c), c), hd * width:(hd + 1) * width]

    def run(seq, hd, lgf, lgb):
        decay = (jnp.where(rel >= 0, jnp.exp(lgf * jnp.maximum(rel, 0.0)), 0.0)
                 + jnp.where(rel <= 0, jnp.exp(lgb * jnp.maximum(-rel, 0.0)), 0.0))
        xi_f = jnp.exp(lgf * (p + 1.0))
        xi_b = jnp.exp(lgb * (c - p))
        zeta_f = jnp.exp(lgf * (c - 1.0 - p))
        zeta_b = jnp.exp(lgb * p)
        g_f = jnp.exp(lgf * chunk_len)
        g_b = jnp.exp(lgb * chunk_len)
        gain = gain_ref[:, hd * RET_DV:(hd + 1) * RET_DV]
        slot = seq * heads + hd

        def outer_kv(i, zeta):
            kz = (chunk(k_ref, seq, i, hd, RET_DK).astype(F32) * zeta).T.astype(BF16)
            return _dot(kz, chunk(v_ref, seq, i, hd, RET_DV))

        if has_s0:
            s_f0 = s0_ref[seq, 0, hd]
            s_b0 = s0_ref[seq, 1, hd]
        else:
            s_f0 = jnp.zeros((RET_DK, RET_DV), F32)
            s_b0 = s_f0

        def scan(t, carry):
            s_f, s_b = carry
            i_b = n_chunks - 1 - t
            s_scr[slot, t, 0:RET_DK, :] = s_f.astype(BF16)
            s_scr[slot, i_b, RET_DK:2 * RET_DK, :] = s_b.astype(BF16)
            return (g_f * s_f + outer_kv(t, zeta_f), g_b * s_b + outer_kv(i_b, zeta_b))

        s_f, s_b = lax.fori_loop(0, n_chunks, scan, (s_f0, s_b0), unroll=unroll)
        if state_layer is not None:
            if state_layer:
                sout_ref[seq, 0:state_layer, :, hd] = sprev_ref[seq, :, :, hd]
            sout_ref[seq, state_layer, 0, hd] = s_f
            sout_ref[seq, state_layer, 1, hd] = s_b

        def emit(i, _):
            qi = chunk(q_ref, seq, i, hd, RET_DK)
            scores = (_dot_nt(qi, chunk(k_ref, seq, i, hd, RET_DK)) * decay).astype(BF16)
            qf = qi.astype(F32)
            qx = jnp.concatenate([(qf * xi_f).astype(BF16), (qf * xi_b).astype(BF16)], axis=1)
            y = _dot(scores, chunk(v_ref, seq, i, hd, RET_DV)) + _dot(qx, s_scr[slot, i])
            mu = jnp.mean(y, axis=-1, keepdims=True)
            d = y - mu
            var = jnp.mean(d * d, axis=-1, keepdims=True)
            yn = d * lax.rsqrt(var + EPS) * gain
            y_ref[seq, pl.ds(pl.multiple_of(i * c, c), c),
                  hd * RET_DV:(hd + 1) * RET_DV] = yn.astype(y_ref.dtype)
            return 0

        lax.fori_loop(0, n_chunks, emit, 0, unroll=unroll)

    for hd in range(heads):
        h = pl.program_id(1) * heads + hd
        for seq in range(q_ref.shape[0]):
            run(seq, hd, lg_ref[0, h], lg_ref[1, h])


def _retention(log_g, rq, rk, rv, gain, state=None, layer=0, new_state=None):
    b, l, _ = rq.shape
    n_chunks = l // RET_CHUNK
    seqs = max(1, SHORT_SEQ_TOKENS // l)
    assert b % seqs == 0
    hps = RET_HEADS_PER_STEP
    head = lambda w: pl.BlockSpec((seqs, l, hps * w), lambda i, h: (i, 0, h))
    in_specs = [pl.BlockSpec(memory_space=pltpu.SMEM), head(RET_DK), head(RET_DK),
                head(RET_DV), pl.BlockSpec((1, hps * RET_DV), lambda i, h: (0, h))]
    args = [log_g, rq, rk, rv, gain.reshape(1, RET_V_W)]
    if state is not None:
        in_specs.append(pl.BlockSpec((seqs, None, 2, hps, RET_DK, RET_DV),
                                     lambda i, h: (i, layer, 0, h, 0, 0)))
        args.append(state)
    out_specs = [head(RET_DV)]
    out_shape = [jax.ShapeDtypeStruct((b, l, RET_V_W), BF16)]
    state_layer = None
    if new_state is not None:
        state_layer, earlier = new_state
        state_spec = lambda n: pl.BlockSpec((seqs, n, 2, hps, RET_DK, RET_DV),
                                            lambda i, h: (i, 0, 0, h, 0, 0))
        if state_layer:
            in_specs.append(state_spec(state_layer))
            args.append(earlier)
        out_specs.append(state_spec(state_layer + 1))
        out_shape.append(jax.ShapeDtypeStruct(
            (b, state_layer + 1, 2, RET_HEADS, RET_DK, RET_DV), F32))
    return pl.pallas_call(
        functools.partial(_retention_kernel, n_chunks=n_chunks, has_s0=state is not None,
                          state_layer=state_layer),
        grid=(b // seqs, RET_HEADS // hps),
        in_specs=in_specs,
        out_specs=out_specs,
        out_shape=out_shape,
        scratch_shapes=[pltpu.VMEM((seqs * hps, n_chunks, 2 * RET_DK, RET_DV), BF16)],
        compiler_params=_params("arbitrary", "arbitrary"),
        name="retention",
    )(*args)


def _fold_lanes(x, op):
    return functools.reduce(op, [x[:, c:c + LANES] for c in range(0, x.shape[1], LANES)])


def _pair_attention(q2, keys, values, biases):
    n_q = q2.shape[0]
    lane = lax.broadcasted_iota(jnp.int32, (1, LANES), 1)
    lo_mask = lane < NA_DH
    zero = jnp.zeros_like(q2)
    qq = jnp.concatenate([jnp.where(lo_mask, q2, zero), jnp.where(lo_mask, zero, q2)], axis=0)
    scores = []
    for kt, bt in zip(keys, biases):
        s = _dot_nt(qq, kt)
        scores.append(s if bt is None else s + bt)
    m = functools.reduce(jnp.maximum, [_fold_lanes(s, jnp.maximum) for s in scores])
    m = m.max(axis=-1, keepdims=True)
    denom = None
    acc = None
    for s, v in zip(scores, values):
        e = jnp.exp2(s - m)
        part = _fold_lanes(e, jnp.add)
        pv = _dot(e.astype(BF16), v)
        denom = part if denom is None else denom + part
        acc = pv if acc is None else acc + pv
    out = acc * (1.0 / denom.sum(axis=-1, keepdims=True))
    return jnp.where(lo_mask, out[:n_q], out[n_q:])


def _ctx_attn_kernel(q_ref, k_ref, v_ref, o_ref):
    for seq in range(q_ref.shape[0]):
        for hp in range(HEAD_PAIRS):
            sl = slice(hp * LANES, (hp + 1) * LANES)
            out = _pair_attention(q_ref[seq, :, sl], [k_ref[seq, :, sl].astype(BF16)],
                                  [v_ref[seq, :, sl].astype(BF16)], [None])
            o_ref[seq, :, sl] = out.astype(o_ref.dtype)


def _ctx_attention(nq, k_cache, v_cache, layer):
    b, l, _ = nq.shape
    seqs = max(1, SHORT_SEQ_TOKENS // l)
    assert b % seqs == 0
    spec = pl.BlockSpec((seqs, l, NA_W), lambda i: (i, 0, 0))
    kv_spec = pl.BlockSpec((seqs, None, l, NA_W), lambda i: (i, layer, 0, 0))
    return pl.pallas_call(
        _ctx_attn_kernel,
        grid=(b // seqs,),
        in_specs=[spec, kv_spec, kv_spec],
        out_specs=spec,
        out_shape=jax.ShapeDtypeStruct((b, l, NA_W), BF16),
        compiler_params=_params("arbitrary"),
        name="ctx_attention",
    )(nq, k_cache, v_cache)


def _na_window_start(j, rows):
    return int(np.clip(NA_Q_ROWS * j - NA_WIN_H // 2, 0, rows - NA_KEY_ROWS))


def _na_bias_rows(rpb):
    half = NA_WIN_W - 1
    gap = jnp.zeros(rpb.shape[:-1] + (LANES - rpb.shape[-1],), F32)
    return jnp.concatenate([rpb[..., half:], gap, rpb[..., :half]], axis=-1) * LOG2_E


def _na_block_index(rows):
    a = np.arange(NA_Q_ROWS)[:, None]
    kk = np.arange(NA_KEY_ROWS)[None, :]
    index = []
    for j in NA_PATTERNS:
        qr = NA_Q_ROWS * j + a
        kr = _na_window_start(j, rows) + kk
        rs = np.clip(qr - NA_WIN_H // 2, 0, rows - NA_WIN_H)
        row_valid = (kr >= rs) & (kr < rs + NA_WIN_H)
        index.append(np.where(row_valid, kr - qr + NA_WIN_H - 1, 2 * NA_WIN_H - 1))
    return np.stack(index)


def _na_kernel(q_ref, k_ref, v_ref, ck_ref, cv_ref, rows_ref, o_ref, tab_ref, blocks_ref, *,
               index):
    step = pl.program_id(1)
    n_blocks = pl.num_programs(1) * NA_BLOCKS_PER_STEP

    @pl.when((pl.program_id(0) == 0) & (step == 0))
    def _():
        heads, n_dr, _ = rows_ref.shape
        qc = lax.broadcasted_iota(jnp.int32, (GRID_W, LANES), 0)
        kc = lax.broadcasted_iota(jnp.int32, (GRID_W, LANES), 1)
        cs = jnp.clip(qc - NA_WIN_W // 2, 0, GRID_W - NA_WIN_W)
        col_valid = (kc >= cs) & (kc < cs + NA_WIN_W)
        for h in range(heads):
            for dr in range(n_dr):
                row = jnp.broadcast_to(rows_ref[h, dr:dr + 1, :], (GRID_W, LANES))
                skew = pltpu.roll(row, 0, 1, stride=1, stride_axis=0)
                blocks_ref[h, dr] = jnp.where(col_valid, skew, MASKED)
        blocks_ref[:, n_dr] = jnp.full((heads, GRID_W, LANES), MASKED, F32)
        for p, rows_p in enumerate(index):
            for a, row_a in enumerate(rows_p):
                for kk, blk in enumerate(row_a):
                    tab_ref[p, :, a * GRID_W:(a + 1) * GRID_W, kk * GRID_W:(kk + 1) * GRID_W] = (
                        blocks_ref[:, int(blk), :, 0:GRID_W])

    for sub in range(NA_BLOCKS_PER_STEP):
        j = step * NA_BLOCKS_PER_STEP + sub
        rows = slice(sub * NA_Q_BLOCK, (sub + 1) * NA_Q_BLOCK)
        pattern = jnp.minimum(j, 2) + jnp.maximum(j - (n_blocks - 3), 0)
        start = jnp.clip(j - 2, 0, n_blocks - NA_KEY_BLOCKS) * NA_Q_BLOCK
        window = pl.ds(pl.multiple_of(start, NA_Q_BLOCK), NA_KEY_BLOCKS * LANES)
        for hp in range(HEAD_PAIRS):
            sl = slice(hp * LANES, (hp + 1) * LANES)
            bias = tab_ref[pattern, 2 * hp:2 * hp + 2].reshape(2 * NA_Q_BLOCK, -1)
            out = _pair_attention(q_ref[rows, sl], [k_ref[window, sl], ck_ref[:, sl]],
                                  [v_ref[window, sl], cv_ref[:, sl]], [bias, None])
            o_ref[rows, sl] = out.astype(o_ref.dtype)


def _na_attention(nq, nk, nv, ck, cv, bias_rows, layer):
    b, l, _ = nq.shape
    n_blocks = l // NA_Q_BLOCK
    assert [min(j, 2) + max(j - (n_blocks - 3), 0) for j in NA_PATTERNS] == list(range(5))
    assert n_blocks % NA_BLOCKS_PER_STEP == 0
    tile = pl.BlockSpec((None, NA_BLOCKS_PER_STEP * NA_Q_BLOCK, NA_W), lambda i, j: (i, j, 0))
    whole = lambda a: pl.BlockSpec((None,) + a.shape[1:], lambda i, j: (i, 0, 0))
    table = (len(NA_PATTERNS), NA_HEADS, NA_Q_BLOCK, NA_KEY_ROWS * GRID_W)
    return pl.pallas_call(
        functools.partial(_na_kernel, index=_na_block_index(l // GRID_W)),
        grid=(b, n_blocks // NA_BLOCKS_PER_STEP),
        in_specs=[tile, whole(nk), whole(nv), whole(ck), whole(cv),
                  pl.BlockSpec((None,) + bias_rows.shape[1:], lambda i, j: (layer, 0, 0, 0),
                               pipeline_mode=pl.Buffered(1))],
        out_specs=tile,
        out_shape=jax.ShapeDtypeStruct((b, l, NA_W), BF16),
        scratch_shapes=[pltpu.VMEM(table, F32),
                        pltpu.VMEM((NA_HEADS, 2 * NA_WIN_H, GRID_W, LANES), F32)],
        compiler_params=_params("arbitrary", "arbitrary"),
        name="na_attention",
    )(nq, nk, nv, ck, cv, bias_rows)


def _tail_kernel(x_ref, mod_ref, yret_ref, rg_ref, yna_ref, gate_ref, n1_ref, n2_ref, n3_ref,
                 w_ret_ref, w_na_ref, w_o_ref, w_g_ref, w_u_ref, w_d_ref, o_ref):
    mod = mod_ref[0]
    g1 = mod[:, 2 * D_MODEL:3 * D_MODEL]
    sh2 = mod[:, 3 * D_MODEL:4 * D_MODEL]
    sc2 = mod[:, 4 * D_MODEL:5 * D_MODEL]
    g2 = mod[:, 5 * D_MODEL:6 * D_MODEL]
    g_ret = gate_ref[:, 0:D_MODEL].astype(F32)
    g_na = gate_ref[:, D_MODEL:2 * D_MODEL].astype(F32)
    y_ret = (_silu(rg_ref[...].astype(F32)) * yret_ref[...].astype(F32)).astype(BF16)
    branches = (g_ret * _dot(y_ret, w_ret_ref[...])
                + g_na * _dot(yna_ref[...], w_na_ref[...]))
    mixed = _dot(branches.astype(BF16), w_o_ref[...])
    x = x_ref[...] + g1 * _rms(mixed, n1_ref[...])
    hf = (_rms(x, n2_ref[...]) * (1.0 + sc2) + sh2).astype(BF16)
    act = (_silu(_dot(hf, w_g_ref[...])) * _dot(hf, w_u_ref[...])).astype(BF16)
    o_ref[...] = x + g2 * _rms(_dot(act, w_d_ref[...]), n3_ref[...])


def _tail(x, mod, mod_row, y_ret, rg, y_na, gates, n1, n2, n3, w_ret, w_na, w_o, w_g, w_u, w_d):
    b, l, _ = x.shape
    tm = min(TOKEN_TILE, l)
    tok = lambda w: pl.BlockSpec((None, tm, w), lambda i, j: (i, j, 0))
    vec = lambda a: a.reshape(1, D_MODEL)
    return pl.pallas_call(
        _tail_kernel,
        grid=(b, l // tm),
        in_specs=[
            tok(D_MODEL),
            pl.BlockSpec((1, 1, 6 * D_MODEL), lambda i, j: (mod_row(i), 0, 0)),
            tok(RET_V_W), tok(RET_V_W), tok(NA_W), tok(2 * D_MODEL),
            _resident((1, D_MODEL)), _resident((1, D_MODEL)), _resident((1, D_MODEL)),
            _resident(w_ret.shape), _resident(w_na.shape), _resident(w_o.shape),
            _resident(w_g.shape), _resident(w_u.shape), _resident(w_d.shape),
        ],
        out_specs=tok(D_MODEL),
        out_shape=jax.ShapeDtypeStruct((b, l, D_MODEL), F32),
        compiler_params=_params("arbitrary", "arbitrary"),
        name="tail",
    )(x, mod, y_ret, rg, y_na, gates, vec(n1), vec(n2), vec(n3), w_ret, w_na, w_o, w_g, w_u, w_d)


def _rope_tables(l):
    t = jnp.arange(l)
    row = (t // GRID_W).astype(F32)
    col = (t % GRID_W).astype(F32)
    n_freq = RET_DK // 4
    inv = ROPE_BASE ** (-jnp.arange(n_freq, dtype=F32) / n_freq)
    ang = jnp.concatenate([row[:, None] * inv, col[:, None] * inv], axis=-1)
    cos = jnp.cos(ang)
    sin = jnp.sin(ang)
    return jnp.concatenate([cos, cos], axis=-1), jnp.concatenate([-sin, sin], axis=-1)


def kernel(x_prompt, x_sample, cache_na_k, cache_na_v, state_ret, c, c_ctx, w_ada, b_ada,
           norm_pre_mix, norm_post_mix, norm_pre_ffn, norm_post_ffn, w_in, ret_decay_logit,
           ret_gn_gain, na_rpb, w_ret_out, w_na_out, w_gate, w_o, w_ffn_gate, w_ffn_up,
           w_ffn_down):
    dec_b, dec_l, _ = x_sample.shape
    past = cache_na_k.shape[2]
    cvec = jnp.zeros((MOD_ROWS, D_MODEL), F32).at[0].set(c_ctx).at[1:1 + dec_b].set(c)
    mods = _adaln(cvec, w_ada, b_ada).reshape(DEPTH, MOD_ROWS, 1, 6 * D_MODEL)
    log_g = jax.nn.log_sigmoid(ret_decay_logit.astype(F32))
    rope_tabs = _rope_tables(dec_l)
    bf = lambda a: a.astype(BF16)
    layers = lambda w: [bf(w[l]) for l in range(DEPTH)]
    w_in_b, w_gate_b = layers(w_in), layers(w_gate)
    w_ret_b, w_na_b, w_o_b = layers(w_ret_out), layers(w_na_out), layers(w_o)
    w_g_b, w_u_b, w_d_b = layers(w_ffn_gate), layers(w_ffn_up), layers(w_ffn_down)
    ck = [bf(cache_na_k[:, l]).reshape(dec_b, past, NA_W) for l in range(DEPTH)]
    cv = [bf(cache_na_v[:, l]).reshape(dec_b, past, NA_W) for l in range(DEPTH)]
    bias_rows = _na_bias_rows(na_rpb.astype(F32))
    ctx_row = lambda i: 0
    dec_row = lambda i: i + 1

    def tail(x, l, row, y_ret, rg, y_na, gates):
        return _tail(x, mods[l], row, y_ret, rg, y_na, gates, norm_post_mix[l], norm_pre_ffn[l],
                     norm_post_ffn[l], w_ret_b[l], w_na_b[l], w_o_b[l], w_g_b[l], w_u_b[l],
                     w_d_b[l])

    ctx_b, ctx_l, _ = x_prompt.shape
    flat = lambda a: a.reshape(1, ctx_b * ctx_l, a.shape[-1])
    per_seq = lambda a: a.reshape(ctx_b, ctx_l, a.shape[-1])
    x = flat(x_prompt)
    new_k = new_v = new_s = None
    for l in range(DEPTH):
        rq, rk, rv, rg, nq, new_k, new_v, gates = _premix(
            x, mods[l], ctx_row, norm_pre_mix[l], w_in_b[l], w_gate_b[l],
            cache=(l, ctx_l, new_k, new_v))
        y_ret, new_s = _retention(log_g[l], per_seq(rq), per_seq(rk), per_seq(rv),
                                  ret_gn_gain[l], new_state=(l, new_s))
        y_na = _ctx_attention(per_seq(nq), new_k, new_v, l)
        x = tail(x, l, ctx_row, flat(y_ret), rg, flat(y_na), gates)
    y_prompt = per_seq(x)
    new_k = new_k.reshape(ctx_b, DEPTH, ctx_l, NA_HEADS, NA_DH)
    new_v = new_v.reshape(ctx_b, DEPTH, ctx_l, NA_HEADS, NA_DH)

    x = x_sample
    for l in range(DEPTH):
        rq, rk, rv, rg, nq, nk, nv, gates = _premix(
            x, mods[l], dec_row, norm_pre_mix[l], w_in_b[l], w_gate_b[l], rope_tabs=rope_tabs)
        (y_ret,) = _retention(log_g[l], rq, rk, rv, ret_gn_gain[l], state=state_ret, layer=l)
        y_na = _na_attention(nq, nk, nv, ck[l], cv[l], bias_rows, l)
        x = tail(x, l, dec_row, y_ret, rg, y_na, gates)
    return (y_prompt, x, new_k, new_v, new_s)
```

```python
import functools
import math

import numpy as np
import jax
import jax.numpy as jnp
from jax import lax
from jax.experimental import pallas as pl
from jax.experimental.pallas import tpu as pltpu

F32 = jnp.float32
BF16 = jnp.bfloat16

D_MODEL = 1024
DEPTH = 2
GRID_W = 64
RET_HEADS = 4
RET_DK = 128
RET_DV = 256
RET_QK_W = RET_HEADS * RET_DK
RET_V_W = RET_HEADS * RET_DV
RET_CHUNK = 128
NA_HEADS = 8
NA_DH = 64
NA_W = NA_HEADS * NA_DH
NA_WIN_H = 8
NA_WIN_W = 16
D_FF = 2816
ROPE_BASE = 10000.0
EPS = 1e-6
IN_WIDTH = 2 * RET_QK_W + 2 * RET_V_W + 3 * NA_W

LANES = 128
HEAD_PAIRS = NA_W // LANES
MOD_ROWS = 16
NA_Q_ROWS = 2
NA_Q_BLOCK = NA_Q_ROWS * GRID_W
NA_KEY_BLOCKS = 5
NA_KEY_ROWS = NA_KEY_BLOCKS * LANES // GRID_W
NA_PATTERNS = (0, 1, 2, 14, 15)
NA_BLOCKS_PER_STEP = 4
MASKED = -1e30
LOG2_E = math.log2(math.e)
NA_Q_SCALE = NA_DH ** -0.5 * LOG2_E
VMEM_LIMIT = 56 * 1024 * 1024
TOKEN_TILE = 512
RET_HEADS_PER_STEP = 2
RET_UNROLL = 16
SHORT_SEQ_TOKENS = 2048


def _sigmoid(x):
    return 1.0 / (1.0 + jnp.exp(-x))


def _silu(x):
    return x * _sigmoid(x)


def _rms(x, g):
    return x * lax.rsqrt(jnp.mean(x * x, axis=-1, keepdims=True) + EPS) * g


def _dot(a, b):
    return jnp.dot(a, b, preferred_element_type=F32)


def _dot_nt(a, b):
    return lax.dot_general(a, b, (((1,), (1,)), ((), ())), preferred_element_type=F32)


def _resident(shape):
    zeros = (0,) * len(shape)
    return pl.BlockSpec(shape, lambda *_: zeros, pipeline_mode=pl.Buffered(1))


def _params(*sem):
    return pltpu.CompilerParams(dimension_semantics=sem, vmem_limit_bytes=VMEM_LIMIT)


def _adaln_kernel(c_ref, w_ref, b_ref, o_ref):
    s = _silu(c_ref[...]).astype(BF16)
    o_ref[0] = _dot(s, w_ref[0].astype(BF16)) + b_ref[0]


def _adaln(cvec, w_ada, b_ada):
    tn = 1536
    n = 6 * D_MODEL
    return pl.pallas_call(
        _adaln_kernel,
        grid=(DEPTH, n // tn),
        in_specs=[
            pl.BlockSpec((MOD_ROWS, D_MODEL), lambda l, j: (0, 0)),
            pl.BlockSpec((1, D_MODEL, tn), lambda l, j: (l, 0, j)),
            pl.BlockSpec((1, 1, tn), lambda l, j: (l, 0, j)),
        ],
        out_specs=pl.BlockSpec((1, MOD_ROWS, tn), lambda l, j: (l, 0, j)),
        out_shape=jax.ShapeDtypeStruct((DEPTH, MOD_ROWS, n), F32),
        compiler_params=_params("arbitrary", "arbitrary"),
        name="adaln",
    )(cvec, w_ada, b_ada.reshape(DEPTH, 1, n))


def _premix_kernel(*refs, rope, cache_layer):
    refs = list(refs)
    x_ref, mod_ref, g_ref, w_in_ref, w_gate_ref = refs[:5]
    pos = 5
    if rope:
        cos_ref, sin_ref = refs[pos:pos + 2]
        pos += 2
    if cache_layer:
        k_prev_ref, v_prev_ref = refs[pos:pos + 2]
        pos += 2
    rq_ref, rk_ref, rv_ref, rg_ref, nq_ref, nk_ref, nv_ref, gate_ref = refs[pos:]
    mod = mod_ref[0]
    sh1 = mod[:, 0:D_MODEL]
    sc1 = mod[:, D_MODEL:2 * D_MODEL]
    hm = (_rms(x_ref[...], g_ref[...]) * (1.0 + sc1) + sh1).astype(BF16)

    def proj(lo, width):
        return _dot(hm, w_in_ref[:, lo:lo + width])

    def rotary(t):
        if not rope:
            return t
        cos = cos_ref[...]
        sin = sin_ref[...]
        heads = []
        for h in range(RET_HEADS):
            blk = t[:, h * RET_DK:(h + 1) * RET_DK]
            heads.append(blk * cos + pltpu.roll(blk, RET_DK // 2, axis=1) * sin)
        return jnp.concatenate(heads, axis=1)

    def store_kv(ref, prev_ref, t):
        if cache_layer is None:
            ref[...] = t.astype(ref.dtype)
            return
        if cache_layer:
            ref[:, 0:cache_layer] = prev_ref[...]
        ref[:, cache_layer] = t.reshape(ref.shape[0], ref.shape[2], ref.shape[3])

    lo = 0
    rq_ref[...] = rotary(proj(lo, RET_QK_W)).astype(rq_ref.dtype)
    lo += RET_QK_W
    rk_ref[...] = (rotary(proj(lo, RET_QK_W)) * (RET_DK ** -0.5)).astype(rk_ref.dtype)
    lo += RET_QK_W
    rv_ref[...] = proj(lo, RET_V_W).astype(rv_ref.dtype)
    lo += RET_V_W
    rg_ref[...] = _silu(proj(lo, RET_V_W)).astype(rg_ref.dtype)
    lo += RET_V_W
    nq_ref[...] = (proj(lo, NA_W) * NA_Q_SCALE).astype(nq_ref.dtype)
    lo += NA_W
    store_kv(nk_ref, k_prev_ref if cache_layer else None, proj(lo, NA_W))
    lo += NA_W
    store_kv(nv_ref, v_prev_ref if cache_layer else None, proj(lo, NA_W))
    gate_ref[...] = _sigmoid(_dot(hm, w_gate_ref[...])).astype(gate_ref.dtype)


def _premix(x, mod, mod_row, g, w_in, w_gate, rope_tabs=None, cache=None):
    b, l, _ = x.shape
    tm = min(TOKEN_TILE, l)
    tok = lambda w: pl.BlockSpec((None, tm, w), lambda i, j: (i, j, 0))
    in_specs = [
        tok(D_MODEL),
        pl.BlockSpec((1, 1, 6 * D_MODEL), lambda i, j: (mod_row(i), 0, 0)),
        _resident((1, D_MODEL)),
        _resident((D_MODEL, IN_WIDTH)),
        _resident((D_MODEL, 2 * D_MODEL)),
    ]
    args = [x, mod, g.reshape(1, D_MODEL), w_in, w_gate]
    if rope_tabs is not None:
        in_specs += [pl.BlockSpec((tm, RET_DK), lambda i, j: (j, 0))] * 2
        args += list(rope_tabs)
    widths = (RET_QK_W, RET_QK_W, RET_V_W, RET_V_W, NA_W, NA_W, NA_W, 2 * D_MODEL)
    out_specs = [tok(w) for w in widths]
    out_shape = [jax.ShapeDtypeStruct((b, l, w), BF16) for w in widths]
    cache_layer = None
    if cache is not None:
        cache_layer, seq, k_prev, v_prev = cache
        assert b == 1 and tm % seq == 0
        per_tile = tm // seq
        kv_spec = lambda n: pl.BlockSpec((per_tile, n, seq, NA_W), lambda i, j: (j, 0, 0, 0))
        if cache_layer:
            in_specs += [kv_spec(cache_layer)] * 2
            args += [k_prev, v_prev]
        kv_shape = jax.ShapeDtypeStruct((l // seq, cache_layer + 1, seq, NA_W), F32)
        out_specs[5:7] = [kv_spec(cache_layer + 1)] * 2
        out_shape[5:7] = [kv_shape] * 2
    return pl.pallas_call(
        functools.partial(_premix_kernel, rope=rope_tabs is not None, cache_layer=cache_layer),
        grid=(b, l // tm),
        in_specs=in_specs,
        out_specs=out_specs,
        out_shape=out_shape,
        compiler_params=_params("arbitrary", "arbitrary"),
        name="premix",
    )(*args)


def _retention_kernel(*refs, n_chunks, has_s0, state_layer):
    refs = list(refs)
    lg_ref, q_ref, k_ref, v_ref, rg_ref, gain_ref = refs[:6]
    pos = 6
    if has_s0:
        s0_ref = refs[pos]
        pos += 1
    if state_layer:
        sprev_ref = refs[pos]
        pos += 1
    y_ref = refs[pos]
    pos += 1
    if state_layer is not None:
        sout_ref = refs[pos]
        pos += 1
    s_scr = refs[pos]

    c = RET_CHUNK
    heads = gain_ref.shape[1] // RET_DV
    row = lax.broadcasted_iota(jnp.int32, (c, c), 0).astype(F32)
    col = lax.broadcasted_iota(jnp.int32, (c, c), 1).astype(F32)
    rel = row - col
    p = lax.broadcasted_iota(jnp.int32, (c, 1), 0).astype(F32)
    chunk_len = jnp.full((1, RET_DV), float(c), F32)
    unroll = min(n_chunks, RET_UNROLL)

    def chunk(ref, seq, i, hd, width):
        return ref[seq, pl.ds(pl.multiple_of(i * c, c), c), hd * width:(hd + 1) * width]

    def run(seq, hd, lgf, lgb):
        decay = (jnp.where(rel >= 0, jnp.exp(lgf * jnp.maximum(rel, 0.0)), 0.0)
                 + jnp.where(rel <= 0, jnp.exp(lgb * jnp.maximum(-rel, 0.0)), 0.0))
        xi_f = jnp.exp(lgf * (p + 1.0))
        xi_b = jnp.exp(lgb * (c - p))
        zeta_f = jnp.exp(lgf * (c - 1.0 - p))
        zeta_b = jnp.exp(lgb * p)
        g_f = jnp.exp(lgf * chunk_len)
        g_b = jnp.exp(lgb * chunk_len)
        gain = gain_ref[:, hd * RET_DV:(hd + 1) * RET_DV]
        slot = seq * heads + hd

        def outer_kv(i, zeta):
            kz = (chunk(k_ref, seq, i, hd, RET_DK).astype(F32) * zeta).T.astype(BF16)
            return _dot(kz, chunk(v_ref, seq, i, hd, RET_DV))

        if has_s0:
            s_f0 = s0_ref[seq, 0, hd]
            s_b0 = s0_ref[seq, 1, hd]
        else:
            s_f0 = jnp.zeros((RET_DK, RET_DV), F32)
            s_b0 = s_f0

        def scan(t, carry):
            s_f, s_b = carry
            i_b = n_chunks - 1 - t
            s_scr[slot, t, 0:RET_DK, :] = s_f.astype(BF16)
            s_scr[slot, i_b, RET_DK:2 * RET_DK, :] = s_b.astype(BF16)
            return (g_f * s_f + outer_kv(t, zeta_f), g_b * s_b + outer_kv(i_b, zeta_b))

        s_f, s_b = lax.fori_loop(0, n_chunks, scan, (s_f0, s_b0), unroll=unroll)
        if state_layer is not None:
            if state_layer:
                sout_ref[seq, 0:state_layer, :, hd] = sprev_ref[seq, :, :, hd]
            sout_ref[seq, state_layer, 0, hd] = s_f
            sout_ref[seq, state_layer, 1, hd] = s_b

        def emit(i, _):
            qi = chunk(q_ref, seq, i, hd, RET_DK)
            scores = (_dot_nt(qi, chunk(k_ref, seq, i, hd, RET_DK)) * decay).astype(BF16)
            qf = qi.astype(F32)
            qx = jnp.concatenate([(qf * xi_f).astype(BF16), (qf * xi_b).astype(BF16)], axis=1)
            y = _dot(scores, chunk(v_ref, seq, i, hd, RET_DV)) + _dot(qx, s_scr[slot, i])
            mu = jnp.mean(y, axis=-1, keepdims=True)
            d = y - mu
            var = jnp.mean(d * d, axis=-1, keepdims=True)
            yn = d * lax.rsqrt(var + EPS) * gain
            out = chunk(rg_ref, seq, i, hd, RET_DV).astype(F32) * yn
            y_ref[seq, pl.ds(pl.multiple_of(i * c, c), c),
                  hd * RET_DV:(hd + 1) * RET_DV] = out.astype(y_ref.dtype)
            return 0

        lax.fori_loop(0, n_chunks, emit, 0, unroll=unroll)

    for hd in range(heads):
        h = pl.program_id(1) * heads + hd
        for seq in range(q_ref.shape[0]):
            run(seq, hd, lg_ref[0, h], lg_ref[1, h])


def _retention(log_g, rq, rk, rv, rg, gain, state=None, layer=0, new_state=None):
    b, l, _ = rq.shape
    n_chunks = l // RET_CHUNK
    seqs = max(1, SHORT_SEQ_TOKENS // l)
    assert b % seqs == 0
    hps = RET_HEADS_PER_STEP
    head = lambda w: pl.BlockSpec((seqs, l, hps * w), lambda i, h: (i, 0, h))
    in_specs = [pl.BlockSpec(memory_space=pltpu.SMEM), head(RET_DK), head(RET_DK),
                head(RET_DV), head(RET_DV),
                pl.BlockSpec((1, hps * RET_DV), lambda i, h: (0, h))]
    args = [log_g, rq, rk, rv, rg, gain.reshape(1, RET_V_W)]
    if state is not None:
        in_specs.append(pl.BlockSpec((seqs, None, 2, hps, RET_DK, RET_DV),
                                     lambda i, h: (i, layer, 0, h, 0, 0)))
        args.append(state)
    out_specs = [head(RET_DV)]
    out_shape = [jax.ShapeDtypeStruct((b, l, RET_V_W), BF16)]
    state_layer = None
    if new_state is not None:
        state_layer, earlier = new_state
        state_spec = lambda n: pl.BlockSpec((seqs, n, 2, hps, RET_DK, RET_DV),
                                            lambda i, h: (i, 0, 0, h, 0, 0))
        if state_layer:
            in_specs.append(state_spec(state_layer))
            args.append(earlier)
        out_specs.append(state_spec(state_layer + 1))
        out_shape.append(jax.ShapeDtypeStruct(
            (b, state_layer + 1, 2, RET_HEADS, RET_DK, RET_DV), F32))
    return pl.pallas_call(
        functools.partial(_retention_kernel, n_chunks=n_chunks, has_s0=state is not None,
                          state_layer=state_layer),
        grid=(b // seqs, RET_HEADS // hps),
        in_specs=in_specs,
        out_specs=out_specs,
        out_shape=out_shape,
        scratch_shapes=[pltpu.VMEM((seqs * hps, n_chunks, 2 * RET_DK, RET_DV), BF16)],
        compiler_params=_params("arbitrary", "arbitrary"),
        name="retention",
    )(*args)


def _fold_lanes(x, op):
    return functools.reduce(op, [x[:, c:c + LANES] for c in range(0, x.shape[1], LANES)])


def _pair_attention(q2, keys, values, biases):
    n_q = q2.shape[0]
    lane = lax.broadcasted_iota(jnp.int32, (1, LANES), 1)
    lo_mask = lane < NA_DH
    zero = jnp.zeros_like(q2)
    qq = jnp.concatenate([jnp.where(lo_mask, q2, zero), jnp.where(lo_mask, zero, q2)], axis=0)
    scores = []
    for kt, bt in zip(keys, biases):
        s = _dot_nt(qq, kt)
        scores.append(s if bt is None else s + bt)
    m = functools.reduce(jnp.maximum, [_fold_lanes(s, jnp.maximum) for s in scores])
    m = m.max(axis=-1, keepdims=True)
    denom = None
    acc = None
    for s, v in zip(scores, values):
        e = jnp.exp2(s - m)
        part = _fold_lanes(e, jnp.add)
        pv = _dot(e.astype(BF16), v)
        denom = part if denom is None else denom + part
        acc = pv if acc is None else acc + pv
    out = acc * (1.0 / denom.sum(axis=-1, keepdims=True))
    return jnp.where(lo_mask, out[:n_q], out[n_q:])


def _ctx_attn_kernel(q_ref, k_ref, v_ref, o_ref):
    for seq in range(q_ref.shape[0]):
        for hp in range(HEAD_PAIRS):
            sl = slice(hp * LANES, (hp + 1) * LANES)
            out = _pair_attention(q_ref[seq, :, sl], [k_ref[seq, :, sl].astype(BF16)],
                                  [v_ref[seq, :, sl].astype(BF16)], [None])
            o_ref[seq, :, sl] = out.astype(o_ref.dtype)


def _ctx_attention(nq, k_cache, v_cache, layer):
    b, l, _ = nq.shape
    seqs = max(1, SHORT_SEQ_TOKENS // l)
    assert b % seqs == 0
    spec = pl.BlockSpec((seqs, l, NA_W), lambda i: (i, 0, 0))
    kv_spec = pl.BlockSpec((seqs, None, l, NA_W), lambda i: (i, layer, 0, 0))
    return pl.pallas_call(
        _ctx_attn_kernel,
        grid=(b // seqs,),
        in_specs=[spec, kv_spec, kv_spec],
        out_specs=spec,
        out_shape=jax.ShapeDtypeStruct((b, l, NA_W), BF16),
        compiler_params=_params("arbitrary"),
        name="ctx_attention",
    )(nq, k_cache, v_cache)


def _na_window_start(j, rows):
    return int(np.clip(NA_Q_ROWS * j - NA_WIN_H // 2, 0, rows - NA_KEY_ROWS))


def _na_bias_rows(rpb):
    half = NA_WIN_W - 1
    gap = jnp.zeros(rpb.shape[:-1] + (LANES - rpb.shape[-1],), F32)
    return jnp.concatenate([rpb[..., half:], gap, rpb[..., :half]], axis=-1) * LOG2_E


def _na_block_index(rows):
    a = np.arange(NA_Q_ROWS)[:, None]
    kk = np.arange(NA_KEY_ROWS)[None, :]
    index = []
    for j in NA_PATTERNS:
        qr = NA_Q_ROWS * j + a
        kr = _na_window_start(j, rows) + kk
        rs = np.clip(qr - NA_WIN_H // 2, 0, rows - NA_WIN_H)
        row_valid = (kr >= rs) & (kr < rs + NA_WIN_H)
        index.append(np.where(row_valid, kr - qr + NA_WIN_H - 1, 2 * NA_WIN_H - 1))
    return np.stack(index)


def _na_kernel(q_ref, k_ref, v_ref, ck_ref, cv_ref, rows_ref, o_ref, tab_ref, blocks_ref, *,
               index):
    step = pl.program_id(1)
    n_blocks = pl.num_programs(1) * NA_BLOCKS_PER_STEP

    @pl.when((pl.program_id(0) == 0) & (step == 0))
    def _():
        heads, n_dr, _ = rows_ref.shape
        qc = lax.broadcasted_iota(jnp.int32, (GRID_W, LANES), 0)
        kc = lax.broadcasted_iota(jnp.int32, (GRID_W, LANES), 1)
        cs = jnp.clip(qc - NA_WIN_W // 2, 0, GRID_W - NA_WIN_W)
        col_valid = (kc >= cs) & (kc < cs + NA_WIN_W)
        for h in range(heads):
            for dr in range(n_dr):
                row = jnp.broadcast_to(rows_ref[h, dr:dr + 1, :], (GRID_W, LANES))
                skew = pltpu.roll(row, 0, 1, stride=1, stride_axis=0)
                blocks_ref[h, dr] = jnp.where(col_valid, skew, MASKED)
        blocks_ref[:, n_dr] = jnp.full((heads, GRID_W, LANES), MASKED, F32)
        for p, rows_p in enumerate(index):
            for a, row_a in enumerate(rows_p):
                for kk, blk in enumerate(row_a):
                    tab_ref[p, :, a * GRID_W:(a + 1) * GRID_W, kk * GRID_W:(kk + 1) * GRID_W] = (
                        blocks_ref[:, int(blk), :, 0:GRID_W])

    for sub in range(NA_BLOCKS_PER_STEP):
        j = step * NA_BLOCKS_PER_STEP + sub
        rows = slice(sub * NA_Q_BLOCK, (sub + 1) * NA_Q_BLOCK)
        pattern = jnp.minimum(j, 2) + jnp.maximum(j - (n_blocks - 3), 0)
        start = jnp.clip(j - 2, 0, n_blocks - NA_KEY_BLOCKS) * NA_Q_BLOCK
        window = pl.ds(pl.multiple_of(start, NA_Q_BLOCK), NA_KEY_BLOCKS * LANES)
        for hp in range(HEAD_PAIRS):
            sl = slice(hp * LANES, (hp + 1) * LANES)
            bias = tab_ref[pattern, 2 * hp:2 * hp + 2].reshape(2 * NA_Q_BLOCK, -1)
            out = _pair_attention(q_ref[rows, sl], [k_ref[window, sl], ck_ref[:, sl]],
                                  [v_ref[window, sl], cv_ref[:, sl]], [bias, None])
            o_ref[rows, sl] = out.astype(o_ref.dtype)


def _na_attention(nq, nk, nv, ck, cv, bias_rows, layer):
    b, l, _ = nq.shape
    n_blocks = l // NA_Q_BLOCK
    assert [min(j, 2) + max(j - (n_blocks - 3), 0) for j in NA_PATTERNS] == list(range(5))
    assert n_blocks % NA_BLOCKS_PER_STEP == 0
    tile = pl.BlockSpec((None, NA_BLOCKS_PER_STEP * NA_Q_BLOCK, NA_W), lambda i, j: (i, j, 0))
    whole = lambda a: pl.BlockSpec((None,) + a.shape[1:], lambda i, j: (i, 0, 0))
    table = (len(NA_PATTERNS), NA_HEADS, NA_Q_BLOCK, NA_KEY_ROWS * GRID_W)
    return pl.pallas_call(
        functools.partial(_na_kernel, index=_na_block_index(l // GRID_W)),
        grid=(b, n_blocks // NA_BLOCKS_PER_STEP),
        in_specs=[tile, whole(nk), whole(nv), whole(ck), whole(cv),
                  pl.BlockSpec((None,) + bias_rows.shape[1:], lambda i, j: (layer, 0, 0, 0),
                               pipeline_mode=pl.Buffered(1))],
        out_specs=tile,
        out_shape=jax.ShapeDtypeStruct((b, l, NA_W), BF16),
        scratch_shapes=[pltpu.VMEM(table, F32),
                        pltpu.VMEM((NA_HEADS, 2 * NA_WIN_H, GRID_W, LANES), F32)],
        compiler_params=_params("arbitrary", "arbitrary"),
        name="na_attention",
    )(nq, nk, nv, ck, cv, bias_rows)


def _tail_kernel(x_ref, mod_ref, yret_ref, yna_ref, gate_ref, n1_ref, n2_ref, n3_ref,
                 w_ret_ref, w_na_ref, w_o_ref, w_g_ref, w_u_ref, w_d_ref, o_ref):
    mod = mod_ref[0]
    g1 = mod[:, 2 * D_MODEL:3 * D_MODEL]
    sh2 = mod[:, 3 * D_MODEL:4 * D_MODEL]
    sc2 = mod[:, 4 * D_MODEL:5 * D_MODEL]
    g2 = mod[:, 5 * D_MODEL:6 * D_MODEL]
    g_ret = gate_ref[:, 0:D_MODEL].astype(F32)
    g_na = gate_ref[:, D_MODEL:2 * D_MODEL].astype(F32)
    branches = (g_ret * _dot(yret_ref[...], w_ret_ref[...])
                + g_na * _dot(yna_ref[...], w_na_ref[...]))
    mixed = _dot(branches.astype(BF16), w_o_ref[...])
    x = x_ref[...] + g1 * _rms(mixed, n1_ref[...])
    hf = (_rms(x, n2_ref[...]) * (1.0 + sc2) + sh2).astype(BF16)
    act = (_silu(_dot(hf, w_g_ref[...])) * _dot(hf, w_u_ref[...])).astype(BF16)
    o_ref[...] = x + g2 * _rms(_dot(act, w_d_ref[...]), n3_ref[...])


def _tail(x, mod, mod_row, y_ret, y_na, gates, n1, n2, n3, w_ret, w_na, w_o, w_g, w_u, w_d):
    b, l, _ = x.shape
    tm = min(TOKEN_TILE, l)
    tok = lambda w: pl.BlockSpec((None, tm, w), lambda i, j: (i, j, 0))
    vec = lambda a: a.reshape(1, D_MODEL)
    return pl.pallas_call(
        _tail_kernel,
        grid=(b, l // tm),
        in_specs=[
            tok(D_MODEL),
            pl.BlockSpec((1, 1, 6 * D_MODEL), lambda i, j: (mod_row(i), 0, 0)),
            tok(RET_V_W), tok(NA_W), tok(2 * D_MODEL),
            _resident((1, D_MODEL)), _resident((1, D_MODEL)), _resident((1, D_MODEL)),
            _resident(w_ret.shape), _resident(w_na.shape), _resident(w_o.shape),
            _resident(w_g.shape), _resident(w_u.shape), _resident(w_d.shape),
        ],
        out_specs=tok(D_MODEL),
        out_shape=jax.ShapeDtypeStruct((b, l, D_MODEL), F32),
        compiler_params=_params("arbitrary", "arbitrary"),
        name="tail",
    )(x, mod, y_ret, y_na, gates, vec(n1), vec(n2), vec(n3), w_ret, w_na, w_o, w_g, w_u, w_d)


def _rope_tables(l):
    t = jnp.arange(l)
    row = (t // GRID_W).astype(F32)
    col = (t % GRID_W).astype(F32)
    n_freq = RET_DK // 4
    inv = ROPE_BASE ** (-jnp.arange(n_freq, dtype=F32) / n_freq)
    ang = jnp.concatenate([row[:, None] * inv, col[:, None] * inv], axis=-1)
    cos = jnp.cos(ang)
    sin = jnp.sin(ang)
    return jnp.concatenate([cos, cos], axis=-1), jnp.concatenate([-sin, sin], axis=-1)


def kernel(x_prompt, x_sample, cache_na_k, cache_na_v, state_ret, c, c_ctx, w_ada, b_ada,
           norm_pre_mix, norm_post_mix, norm_pre_ffn, norm_post_ffn, w_in, ret_decay_logit,
           ret_gn_gain, na_rpb, w_ret_out, w_na_out, w_gate, w_o, w_ffn_gate, w_ffn_up,
           w_ffn_down):
    dec_b, dec_l, _ = x_sample.shape
    past = cache_na_k.shape[2]
    cvec = jnp.zeros((MOD_ROWS, D_MODEL), F32).at[0].set(c_ctx).at[1:1 + dec_b].set(c)
    mods = _adaln(cvec, w_ada, b_ada).reshape(DEPTH, MOD_ROWS, 1, 6 * D_MODEL)
    log_g = jax.nn.log_sigmoid(ret_decay_logit.astype(F32))
    rope_tabs = _rope_tables(dec_l)
    bf = lambda a: a.astype(BF16)
    layers = lambda w: [bf(w[l]) for l in range(DEPTH)]
    w_in_b, w_gate_b = layers(w_in), layers(w_gate)
    w_ret_b, w_na_b, w_o_b = layers(w_ret_out), layers(w_na_out), layers(w_o)
    w_g_b, w_u_b, w_d_b = layers(w_ffn_gate), layers(w_ffn_up), layers(w_ffn_down)
    ck = [bf(cache_na_k[:, l]).reshape(dec_b, past, NA_W) for l in range(DEPTH)]
    cv = [bf(cache_na_v[:, l]).reshape(dec_b, past, NA_W) for l in range(DEPTH)]
    bias_rows = _na_bias_rows(na_rpb.astype(F32))
    ctx_row = lambda i: 0
    dec_row = lambda i: i + 1

    def tail(x, l, row, y_ret, y_na, gates):
        return _tail(x, mods[l], row, y_ret, y_na, gates, norm_post_mix[l], norm_pre_ffn[l],
                     norm_post_ffn[l], w_ret_b[l], w_na_b[l], w_o_b[l], w_g_b[l], w_u_b[l],
                     w_d_b[l])

    ctx_b, ctx_l, _ = x_prompt.shape
    flat = lambda a: a.reshape(1, ctx_b * ctx_l, a.shape[-1])
    per_seq = lambda a: a.reshape(ctx_b, ctx_l, a.shape[-1])
    x = flat(x_prompt)
    new_k = new_v = new_s = None
    for l in range(DEPTH):
        rq, rk, rv, rg, nq, new_k, new_v, gates = _premix(
            x, mods[l], ctx_row, norm_pre_mix[l], w_in_b[l], w_gate_b[l],
            cache=(l, ctx_l, new_k, new_v))
        y_ret, new_s = _retention(log_g[l], per_seq(rq), per_seq(rk), per_seq(rv), per_seq(rg),
                                  ret_gn_gain[l], new_state=(l, new_s))
        y_na = _ctx_attention(per_seq(nq), new_k, new_v, l)
        x = tail(x, l, ctx_row, flat(y_ret), flat(y_na), gates)
    y_prompt = per_seq(x)
    new_k = new_k.reshape(ctx_b, DEPTH, ctx_l, NA_HEADS, NA_DH)
    new_v = new_v.reshape(ctx_b, DEPTH, ctx_l, NA_HEADS, NA_DH)

    x = x_sample
    for l in range(DEPTH):
        rq, rk, rv, rg, nq, nk, nv, gates = _premix(
            x, mods[l], dec_row, norm_pre_mix[l], w_in_b[l], w_gate_b[l], rope_tabs=rope_tabs)
        (y_ret,) = _retention(log_g[l], rq, rk, rv, rg, ret_gn_gain[l], state=state_ret, layer=l)
        y_na = _na_attention(nq, nk, nv, ck[l], cv[l], bias_rows, l)
        x = tail(x, l, dec_row, y_ret, y_na, gates)
    return (y_prompt, x, new_k, new_v, new_s)
```

```python
import functools
import math

import numpy as np
import jax
import jax.numpy as jnp
from jax import lax
from jax.experimental import pallas as pl
from jax.experimental.pallas import tpu as pltpu

F32 = jnp.float32
BF16 = jnp.bfloat16

D_MODEL = 1024
DEPTH = 2
GRID_W = 64
RET_HEADS = 4
RET_DK = 128
RET_DV = 256
RET_QK_W = RET_HEADS * RET_DK
RET_V_W = RET_HEADS * RET_DV
RET_CHUNK = 128
NA_HEADS = 8
NA_DH = 64
NA_W = NA_HEADS * NA_DH
NA_WIN_H = 8
NA_WIN_W = 16
D_FF = 2816
ROPE_BASE = 10000.0
EPS = 1e-6
IN_WIDTH = 2 * RET_QK_W + 2 * RET_V_W + 3 * NA_W

LANES = 128
HEAD_PAIRS = NA_W // LANES
MOD_ROWS = 16
NA_Q_ROWS = 2
NA_Q_BLOCK = NA_Q_ROWS * GRID_W
NA_KEY_BLOCKS = 5
NA_KEY_ROWS = NA_KEY_BLOCKS * LANES // GRID_W
NA_PATTERNS = (0, 1, 2, 14, 15)
NA_BLOCKS_PER_STEP = 4
MASKED = -1e30
LOG2_E = math.log2(math.e)
NA_Q_SCALE = NA_DH ** -0.5 * LOG2_E
VMEM_LIMIT = 56 * 1024 * 1024
TOKEN_TILE = 512
RET_HEADS_PER_STEP = 2
RET_UNROLL = 16
SHORT_SEQ_TOKENS = 2048


def _sigmoid(x):
    return 1.0 / (1.0 + jnp.exp(-x))


def _silu(x):
    return x * _sigmoid(x)


def _rms(x, g):
    return x * lax.rsqrt(jnp.mean(x * x, axis=-1, keepdims=True) + EPS) * g


def _dot(a, b):
    return jnp.dot(a, b, preferred_element_type=F32)


def _dot_nt(a, b):
    return lax.dot_general(a, b, (((1,), (1,)), ((), ())), preferred_element_type=F32)


def _resident(shape):
    zeros = (0,) * len(shape)
    return pl.BlockSpec(shape, lambda *_: zeros, pipeline_mode=pl.Buffered(1))


def _params(*sem):
    return pltpu.CompilerParams(dimension_semantics=sem, vmem_limit_bytes=VMEM_LIMIT)


def _adaln_kernel(c_ref, w_ref, b_ref, o_ref):
    s = _silu(c_ref[...]).astype(BF16)
    o_ref[0] = _dot(s, w_ref[0].astype(BF16)) + b_ref[0]


def _adaln(cvec, w_ada, b_ada):
    tn = 1536
    n = 6 * D_MODEL
    return pl.pallas_call(
        _adaln_kernel,
        grid=(DEPTH, n // tn),
        in_specs=[
            pl.BlockSpec((MOD_ROWS, D_MODEL), lambda l, j: (0, 0)),
            pl.BlockSpec((1, D_MODEL, tn), lambda l, j: (l, 0, j)),
            pl.BlockSpec((1, 1, tn), lambda l, j: (l, 0, j)),
        ],
        out_specs=pl.BlockSpec((1, MOD_ROWS, tn), lambda l, j: (l, 0, j)),
        out_shape=jax.ShapeDtypeStruct((DEPTH, MOD_ROWS, n), F32),
        compiler_params=_params("arbitrary", "arbitrary"),
        name="adaln",
    )(cvec, w_ada, b_ada.reshape(DEPTH, 1, n))


def _premix_kernel(*refs, rope, cache_layer):
    refs = list(refs)
    x_ref, mod_ref, g_ref, w_in_ref, w_gate_ref = refs[:5]
    pos = 5
    if rope:
        cos_ref, sin_ref = refs[pos:pos + 2]
        pos += 2
    if cache_layer:
        k_prev_ref, v_prev_ref = refs[pos:pos + 2]
        pos += 2
    rq_ref, rk_ref, rv_ref, rg_ref, nq_ref, nk_ref, nv_ref, gate_ref = refs[pos:]
    mod = mod_ref[0]
    sh1 = mod[:, 0:D_MODEL]
    sc1 = mod[:, D_MODEL:2 * D_MODEL]
    hm = (_rms(x_ref[...], g_ref[...]) * (1.0 + sc1) + sh1).astype(BF16)

    def proj(lo, width):
        return _dot(hm, w_in_ref[:, lo:lo + width])

    def rotary(t):
        if not rope:
            return t
        cos = cos_ref[...]
        sin = sin_ref[...]
        heads = []
        for h in range(RET_HEADS):
            blk = t[:, h * RET_DK:(h + 1) * RET_DK]
            heads.append(blk * cos + pltpu.roll(blk, RET_DK // 2, axis=1) * sin)
        return jnp.concatenate(heads, axis=1)

    def store_kv(ref, prev_ref, t):
        if cache_layer is None:
            ref[...] = t.astype(ref.dtype)
            return
        if cache_layer:
            ref[:, 0:cache_layer] = prev_ref[...]
        ref[:, cache_layer] = t.reshape(ref.shape[0], ref.shape[2], ref.shape[3])

    lo = 0
    rq_ref[...] = rotary(proj(lo, RET_QK_W)).astype(rq_ref.dtype)
    lo += RET_QK_W
    rk_ref[...] = (rotary(proj(lo, RET_QK_W)) * (RET_DK ** -0.5)).astype(rk_ref.dtype)
    lo += RET_QK_W
    rv_ref[...] = proj(lo, RET_V_W).astype(rv_ref.dtype)
    lo += RET_V_W
    rg_ref[...] = proj(lo, RET_V_W).astype(rg_ref.dtype)
    lo += RET_V_W
    nq_ref[...] = (proj(lo, NA_W) * NA_Q_SCALE).astype(nq_ref.dtype)
    lo += NA_W
    store_kv(nk_ref, k_prev_ref if cache_layer else None, proj(lo, NA_W))
    lo += NA_W
    store_kv(nv_ref, v_prev_ref if cache_layer else None, proj(lo, NA_W))
    gate_ref[...] = _dot(hm, w_gate_ref[...]).astype(gate_ref.dtype)


def _premix(x, mod, mod_row, g, w_in, w_gate, rope_tabs=None, cache=None):
    b, l, _ = x.shape
    tm = min(TOKEN_TILE, l)
    tok = lambda w: pl.BlockSpec((None, tm, w), lambda i, j: (i, j, 0))
    in_specs = [
        tok(D_MODEL),
        pl.BlockSpec((1, 1, 6 * D_MODEL), lambda i, j: (mod_row(i), 0, 0)),
        _resident((1, D_MODEL)),
        _resident((D_MODEL, IN_WIDTH)),
        _resident((D_MODEL, 2 * D_MODEL)),
    ]
    args = [x, mod, g.reshape(1, D_MODEL), w_in, w_gate]
    if rope_tabs is not None:
        in_specs += [pl.BlockSpec((tm, RET_DK), lambda i, j: (j, 0))] * 2
        args += list(rope_tabs)
    widths = (RET_QK_W, RET_QK_W, RET_V_W, RET_V_W, NA_W, NA_W, NA_W, 2 * D_MODEL)
    out_specs = [tok(w) for w in widths]
    out_shape = [jax.ShapeDtypeStruct((b, l, w), BF16) for w in widths]
    cache_layer = None
    if cache is not None:
        cache_layer, seq, k_prev, v_prev = cache
        assert b == 1 and tm % seq == 0
        per_tile = tm // seq
        kv_spec = lambda n: pl.BlockSpec((per_tile, n, seq, NA_W), lambda i, j: (j, 0, 0, 0))
        if cache_layer:
            in_specs += [kv_spec(cache_layer)] * 2
            args += [k_prev, v_prev]
        kv_shape = jax.ShapeDtypeStruct((l // seq, cache_layer + 1, seq, NA_W), F32)
        out_specs[5:7] = [kv_spec(cache_layer + 1)] * 2
        out_shape[5:7] = [kv_shape] * 2
    return pl.pallas_call(
        functools.partial(_premix_kernel, rope=rope_tabs is not None, cache_layer=cache_layer),
        grid=(b, l // tm),
        in_specs=in_specs,
        out_specs=out_specs,
        out_shape=out_shape,
        compiler_params=_params("arbitrary", "arbitrary"),
        name="premix",
    )(*args)


def _retention_kernel(*refs, n_chunks, has_s0, state_layer):
    refs = list(refs)
    lg_ref, q_ref, k_ref, v_ref, rg_ref, gain_ref = refs[:6]
    pos = 6
    if has_s0:
        s0_ref = refs[pos]
        pos += 1
    if state_layer:
        sprev_ref = refs[pos]
        pos += 1
    y_ref = refs[pos]
    pos += 1
    if state_layer is not None:
        sout_ref = refs[pos]
        pos += 1
    s_scr = refs[pos]

    c = RET_CHUNK
    heads = gain_ref.shape[1] // RET_DV
    row = lax.broadcasted_iota(jnp.int32, (c, c), 0).astype(F32)
    col = lax.broadcasted_iota(jnp.int32, (c, c), 1).astype(F32)
    rel = row - col
    p = lax.broadcasted_iota(jnp.int32, (c, 1), 0).astype(F32)
    chunk_len = jnp.full((1, RET_DV), float(c), F32)
    unroll = min(n_chunks, RET_UNROLL)

    def chunk(ref, seq, i, hd, width):
        return ref[seq, pl.ds(pl.multiple_of(i * c, c), c), hd * width:(hd + 1) * width]

    def run(seq, hd, lgf, lgb):
        decay = (jnp.where(rel >= 0, jnp.exp(lgf * jnp.maximum(rel, 0.0)), 0.0)
                 + jnp.where(rel <= 0, jnp.exp(lgb * jnp.maximum(-rel, 0.0)), 0.0))
        xi_f = jnp.exp(lgf * (p + 1.0))
        xi_b = jnp.exp(lgb * (c - p))
        zeta_f = jnp.exp(lgf * (c - 1.0 - p))
        zeta_b = jnp.exp(lgb * p)
        g_f = jnp.exp(lgf * chunk_len)
        g_b = jnp.exp(lgb * chunk_len)
        gain = gain_ref[:, hd * RET_DV:(hd + 1) * RET_DV]
        slot = seq * heads + hd

        def outer_kv(i, zeta):
            kz = (chunk(k_ref, seq, i, hd, RET_DK).astype(F32) * zeta).T.astype(BF16)
            return _dot(kz, chunk(v_ref, seq, i, hd, RET_DV))

        if has_s0:
            s_f0 = s0_ref[seq, 0, hd]
            s_b0 = s0_ref[seq, 1, hd]
        else:
            s_f0 = jnp.zeros((RET_DK, RET_DV), F32)
            s_b0 = s_f0

        def scan(t, carry):
            s_f, s_b = carry
            i_b = n_chunks - 1 - t
            s_scr[slot, t, 0:RET_DK, :] = s_f.astype(BF16)
            s_scr[slot, i_b, RET_DK:2 * RET_DK, :] = s_b.astype(BF16)
            return (g_f * s_f + outer_kv(t, zeta_f), g_b * s_b + outer_kv(i_b, zeta_b))

        s_f, s_b = lax.fori_loop(0, n_chunks, scan, (s_f0, s_b0), unroll=unroll)
        if state_layer is not None:
            if state_layer:
                sout_ref[seq, 0:state_layer, :, hd] = sprev_ref[seq, :, :, hd]
            sout_ref[seq, state_layer, 0, hd] = s_f
            sout_ref[seq, state_layer, 1, hd] = s_b

        def emit(i, _):
            qi = chunk(q_ref, seq, i, hd, RET_DK)
            scores = (_dot_nt(qi, chunk(k_ref, seq, i, hd, RET_DK)) * decay).astype(BF16)
            qf = qi.astype(F32)
            qx = jnp.concatenate([(qf * xi_f).astype(BF16), (qf * xi_b).astype(BF16)], axis=1)
            y = _dot(scores, chunk(v_ref, seq, i, hd, RET_DV)) + _dot(qx, s_scr[slot, i])
            mu = jnp.mean(y, axis=-1, keepdims=True)
            d = y - mu
            var = jnp.mean(d * d, axis=-1, keepdims=True)
            yn = d * lax.rsqrt(var + EPS) * gain
            out = _silu(chunk(rg_ref, seq, i, hd, RET_DV).astype(F32)) * yn
            y_ref[seq, pl.ds(pl.multiple_of(i * c, c), c),
                  hd * RET_DV:(hd + 1) * RET_DV] = out.astype(y_ref.dtype)
            return 0

        lax.fori_loop(0, n_chunks, emit, 0, unroll=unroll)

    for hd in range(heads):
        h = pl.program_id(1) * heads + hd
        for seq in range(q_ref.shape[0]):
            run(seq, hd, lg_ref[0, h], lg_ref[1, h])


def _retention(log_g, rq, rk, rv, rg, gain, state=None, layer=0, new_state=None):
    b, l, _ = rq.shape
    n_chunks = l // RET_CHUNK
    seqs = max(1, SHORT_SEQ_TOKENS // l)
    assert b % seqs == 0
    hps = RET_HEADS_PER_STEP
    head = lambda w: pl.BlockSpec((seqs, l, hps * w), lambda i, h: (i, 0, h))
    in_specs = [pl.BlockSpec(memory_space=pltpu.SMEM), head(RET_DK), head(RET_DK),
                head(RET_DV), head(RET_DV),
                pl.BlockSpec((1, hps * RET_DV), lambda i, h: (0, h))]
    args = [log_g, rq, rk, rv, rg, gain.reshape(1, RET_V_W)]
    if state is not None:
        in_specs.append(pl.BlockSpec((seqs, None, 2, hps, RET_DK, RET_DV),
                                     lambda i, h: (i, layer, 0, h, 0, 0)))
        args.append(state)
    out_specs = [head(RET_DV)]
    out_shape = [jax.ShapeDtypeStruct((b, l, RET_V_W), BF16)]
    state_layer = None
    if new_state is not None:
        state_layer, earlier = new_state
        state_spec = lambda n: pl.BlockSpec((seqs, n, 2, hps, RET_DK, RET_DV),
                                            lambda i, h: (i, 0, 0, h, 0, 0))
        if state_layer:
            in_specs.append(state_spec(state_layer))
            args.append(earlier)
        out_specs.append(state_spec(state_layer + 1))
        out_shape.append(jax.ShapeDtypeStruct(
            (b, state_layer + 1, 2, RET_HEADS, RET_DK, RET_DV), F32))
    return pl.pallas_call(
        functools.partial(_retention_kernel, n_chunks=n_chunks, has_s0=state is not None,
                          state_layer=state_layer),
        grid=(b // seqs, RET_HEADS // hps),
        in_specs=in_specs,
        out_specs=out_specs,
        out_shape=out_shape,
        scratch_shapes=[pltpu.VMEM((seqs * hps, n_chunks, 2 * RET_DK, RET_DV), BF16)],
        compiler_params=_params("arbitrary", "arbitrary"),
        name="retention",
    )(*args)


def _fold_lanes(x, op):
    return functools.reduce(op, [x[:, c:c + LANES] for c in range(0, x.shape[1], LANES)])


def _pair_attention(q2, keys, values, biases):
    n_q = q2.shape[0]
    lane = lax.broadcasted_iota(jnp.int32, (1, LANES), 1)
    lo_mask = lane < NA_DH
    zero = jnp.zeros_like(q2)
    qq = jnp.concatenate([jnp.where(lo_mask, q2, zero), jnp.where(lo_mask, zero, q2)], axis=0)
    scores = []
    for kt, bt in zip(keys, biases):
        s = _dot_nt(qq, kt)
        scores.append(s if bt is None else s + bt)
    m = functools.reduce(jnp.maximum, [_fold_lanes(s, jnp.maximum) for s in scores])
    m = m.max(axis=-1, keepdims=True)
    denom = None
    acc = None
    for s, v in zip(scores, values):
        e = jnp.exp2(s - m)
        part = _fold_lanes(e, jnp.add)
        pv = _dot(e.astype(BF16), v)
        denom = part if denom is None else denom + part
        acc = pv if acc is None else acc + pv
    out = acc * (1.0 / denom.sum(axis=-1, keepdims=True))
    return jnp.where(lo_mask, out[:n_q], out[n_q:])


def _ctx_attn_kernel(q_ref, k_ref, v_ref, o_ref):
    for seq in range(q_ref.shape[0]):
        for hp in range(HEAD_PAIRS):
            sl = slice(hp * LANES, (hp + 1) * LANES)
            out = _pair_attention(q_ref[seq, :, sl], [k_ref[seq, :, sl].astype(BF16)],
                                  [v_ref[seq, :, sl].astype(BF16)], [None])
            o_ref[seq, :, sl] = out.astype(o_ref.dtype)


def _ctx_attention(nq, k_cache, v_cache, layer):
    b, l, _ = nq.shape
    seqs = max(1, SHORT_SEQ_TOKENS // l)
    assert b % seqs == 0
    spec = pl.BlockSpec((seqs, l, NA_W), lambda i: (i, 0, 0))
    kv_spec = pl.BlockSpec((seqs, None, l, NA_W), lambda i: (i, layer, 0, 0))
    return pl.pallas_call(
        _ctx_attn_kernel,
        grid=(b // seqs,),
        in_specs=[spec, kv_spec, kv_spec],
        out_specs=spec,
        out_shape=jax.ShapeDtypeStruct((b, l, NA_W), BF16),
        compiler_params=_params("arbitrary"),
        name="ctx_attention",
    )(nq, k_cache, v_cache)


def _na_window_start(j, rows):
    return int(np.clip(NA_Q_ROWS * j - NA_WIN_H // 2, 0, rows - NA_KEY_ROWS))


def _na_bias_rows(rpb):
    half = NA_WIN_W - 1
    gap = jnp.zeros(rpb.shape[:-1] + (LANES - rpb.shape[-1],), F32)
    return jnp.concatenate([rpb[..., half:], gap, rpb[..., :half]], axis=-1) * LOG2_E


def _na_block_index(rows):
    a = np.arange(NA_Q_ROWS)[:, None]
    kk = np.arange(NA_KEY_ROWS)[None, :]
    index = []
    for j in NA_PATTERNS:
        qr = NA_Q_ROWS * j + a
        kr = _na_window_start(j, rows) + kk
        rs = np.clip(qr - NA_WIN_H // 2, 0, rows - NA_WIN_H)
        row_valid = (kr >= rs) & (kr < rs + NA_WIN_H)
        index.append(np.where(row_valid, kr - qr + NA_WIN_H - 1, 2 * NA_WIN_H - 1))
    return np.stack(index)


def _na_kernel(q_ref, k_ref, v_ref, ck_ref, cv_ref, rows_ref, o_ref, tab_ref, blocks_ref, *,
               index):
    step = pl.program_id(1)
    n_blocks = pl.num_programs(1) * NA_BLOCKS_PER_STEP

    @pl.when((pl.program_id(0) == 0) & (step == 0))
    def _():
        heads, n_dr, _ = rows_ref.shape
        qc = lax.broadcasted_iota(jnp.int32, (GRID_W, LANES), 0)
        kc = lax.broadcasted_iota(jnp.int32, (GRID_W, LANES), 1)
        cs = jnp.clip(qc - NA_WIN_W // 2, 0, GRID_W - NA_WIN_W)
        col_valid = (kc >= cs) & (kc < cs + NA_WIN_W)
        for h in range(heads):
            for dr in range(n_dr):
                row = jnp.broadcast_to(rows_ref[h, dr:dr + 1, :], (GRID_W, LANES))
                skew = pltpu.roll(row, 0, 1, stride=1, stride_axis=0)
                blocks_ref[h, dr] = jnp.where(col_valid, skew, MASKED)
        blocks_ref[:, n_dr] = jnp.full((heads, GRID_W, LANES), MASKED, F32)
        for p, rows_p in enumerate(index):
            for a, row_a in enumerate(rows_p):
                for kk, blk in enumerate(row_a):
                    tab_ref[p, :, a * GRID_W:(a + 1) * GRID_W, kk * GRID_W:(kk + 1) * GRID_W] = (
                        blocks_ref[:, int(blk), :, 0:GRID_W])

    for sub in range(NA_BLOCKS_PER_STEP):
        j = step * NA_BLOCKS_PER_STEP + sub
        rows = slice(sub * NA_Q_BLOCK, (sub + 1) * NA_Q_BLOCK)
        pattern = jnp.minimum(j, 2) + jnp.maximum(j - (n_blocks - 3), 0)
        start = jnp.clip(j - 2, 0, n_blocks - NA_KEY_BLOCKS) * NA_Q_BLOCK
        window = pl.ds(pl.multiple_of(start, NA_Q_BLOCK), NA_KEY_BLOCKS * LANES)
        for hp in range(HEAD_PAIRS):
            sl = slice(hp * LANES, (hp + 1) * LANES)
            bias = tab_ref[pattern, 2 * hp:2 * hp + 2].reshape(2 * NA_Q_BLOCK, -1)
            out = _pair_attention(q_ref[rows, sl], [k_ref[window, sl], ck_ref[:, sl]],
                                  [v_ref[window, sl], cv_ref[:, sl]], [bias, None])
            o_ref[rows, sl] = out.astype(o_ref.dtype)


def _na_attention(nq, nk, nv, ck, cv, bias_rows, layer):
    b, l, _ = nq.shape
    n_blocks = l // NA_Q_BLOCK
    assert [min(j, 2) + max(j - (n_blocks - 3), 0) for j in NA_PATTERNS] == list(range(5))
    assert n_blocks % NA_BLOCKS_PER_STEP == 0
    tile = pl.BlockSpec((None, NA_BLOCKS_PER_STEP * NA_Q_BLOCK, NA_W), lambda i, j: (i, j, 0))
    whole = lambda a: pl.BlockSpec((None,) + a.shape[1:], lambda i, j: (i, 0, 0))
    table = (len(NA_PATTERNS), NA_HEADS, NA_Q_BLOCK, NA_KEY_ROWS * GRID_W)
    return pl.pallas_call(
        functools.partial(_na_kernel, index=_na_block_index(l // GRID_W)),
        grid=(b, n_blocks // NA_BLOCKS_PER_STEP),
        in_specs=[tile, whole(nk), whole(nv), whole(ck), whole(cv),
                  pl.BlockSpec((None,) + bias_rows.shape[1:], lambda i, j: (layer, 0, 0, 0),
                               pipeline_mode=pl.Buffered(1))],
        out_specs=tile,
        out_shape=jax.ShapeDtypeStruct((b, l, NA_W), BF16),
        scratch_shapes=[pltpu.VMEM(table, F32),
                        pltpu.VMEM((NA_HEADS, 2 * NA_WIN_H, GRID_W, LANES), F32)],
        compiler_params=_params("arbitrary", "arbitrary"),
        name="na_attention",
    )(nq, nk, nv, ck, cv, bias_rows)


def _tail_kernel(x_ref, mod_ref, yret_ref, yna_ref, gate_ref, n1_ref, n2_ref, n3_ref,
                 w_ret_ref, w_na_ref, w_o_ref, w_g_ref, w_u_ref, w_d_ref, o_ref):
    mod = mod_ref[0]
    g1 = mod[:, 2 * D_MODEL:3 * D_MODEL]
    sh2 = mod[:, 3 * D_MODEL:4 * D_MODEL]
    sc2 = mod[:, 4 * D_MODEL:5 * D_MODEL]
    g2 = mod[:, 5 * D_MODEL:6 * D_MODEL]
    g_ret = _sigmoid(gate_ref[:, 0:D_MODEL].astype(F32))
    g_na = _sigmoid(gate_ref[:, D_MODEL:2 * D_MODEL].astype(F32))
    branches = (g_ret * _dot(yret_ref[...], w_ret_ref[...])
                + g_na * _dot(yna_ref[...], w_na_ref[...]))
    mixed = _dot(branches.astype(BF16), w_o_ref[...])
    x = x_ref[...] + g1 * _rms(mixed, n1_ref[...])
    hf = (_rms(x, n2_ref[...]) * (1.0 + sc2) + sh2).astype(BF16)
    act = (_silu(_dot(hf, w_g_ref[...])) * _dot(hf, w_u_ref[...])).astype(BF16)
    o_ref[...] = x + g2 * _rms(_dot(act, w_d_ref[...]), n3_ref[...])


def _tail(x, mod, mod_row, y_ret, y_na, gates, n1, n2, n3, w_ret, w_na, w_o, w_g, w_u, w_d):
    b, l, _ = x.shape
    tm = min(TOKEN_TILE, l)
    tok = lambda w: pl.BlockSpec((None, tm, w), lambda i, j: (i, j, 0))
    vec = lambda a: a.reshape(1, D_MODEL)
    return pl.pallas_call(
        _tail_kernel,
        grid=(b, l // tm),
        in_specs=[
            tok(D_MODEL),
            pl.BlockSpec((1, 1, 6 * D_MODEL), lambda i, j: (mod_row(i), 0, 0)),
            tok(RET_V_W), tok(NA_W), tok(2 * D_MODEL),
            _resident((1, D_MODEL)), _resident((1, D_MODEL)), _resident((1, D_MODEL)),
            _resident(w_ret.shape), _resident(w_na.shape), _resident(w_o.shape),
            _resident(w_g.shape), _resident(w_u.shape), _resident(w_d.shape),
        ],
        out_specs=tok(D_MODEL),
        out_shape=jax.ShapeDtypeStruct((b, l, D_MODEL), F32),
        compiler_params=_params("arbitrary", "arbitrary"),
        name="tail",
    )(x, mod, y_ret, y_na, gates, vec(n1), vec(n2), vec(n3), w_ret, w_na, w_o, w_g, w_u, w_d)


def _rope_tables(l):
    t = jnp.arange(l)
    row = (t // GRID_W).astype(F32)
    col = (t % GRID_W).astype(F32)
    n_freq = RET_DK // 4
    inv = ROPE_BASE ** (-jnp.arange(n_freq, dtype=F32) / n_freq)
    ang = jnp.concatenate([row[:, None] * inv, col[:, None] * inv], axis=-1)
    cos = jnp.cos(ang)
    sin = jnp.sin(ang)
    return jnp.concatenate([cos, cos], axis=-1), jnp.concatenate([-sin, sin], axis=-1)


def kernel(x_prompt, x_sample, cache_na_k, cache_na_v, state_ret, c, c_ctx, w_ada, b_ada,
           norm_pre_mix, norm_post_mix, norm_pre_ffn, norm_post_ffn, w_in, ret_decay_logit,
           ret_gn_gain, na_rpb, w_ret_out, w_na_out, w_gate, w_o, w_ffn_gate, w_ffn_up,
           w_ffn_down):
    dec_b, dec_l, _ = x_sample.shape
    past = cache_na_k.shape[2]
    cvec = jnp.zeros((MOD_ROWS, D_MODEL), F32).at[0].set(c_ctx).at[1:1 + dec_b].set(c)
    mods = _adaln(cvec, w_ada, b_ada).reshape(DEPTH, MOD_ROWS, 1, 6 * D_MODEL)
    log_g = jax.nn.log_sigmoid(ret_decay_logit.astype(F32))
    rope_tabs = _rope_tables(dec_l)
    bf = lambda a: a.astype(BF16)
    layers = lambda w: [bf(w[l]) for l in range(DEPTH)]
    w_in_b, w_gate_b = layers(w_in), layers(w_gate)
    w_ret_b, w_na_b, w_o_b = layers(w_ret_out), layers(w_na_out), layers(w_o)
    w_g_b, w_u_b, w_d_b = layers(w_ffn_gate), layers(w_ffn_up), layers(w_ffn_down)
    ck = [bf(cache_na_k[:, l]).reshape(dec_b, past, NA_W) for l in range(DEPTH)]
    cv = [bf(cache_na_v[:, l]).reshape(dec_b, past, NA_W) for l in range(DEPTH)]
    bias_rows = _na_bias_rows(na_rpb.astype(F32))
    ctx_row = lambda i: 0
    dec_row = lambda i: i + 1

    def tail(x, l, row, y_ret, y_na, gates):
        return _tail(x, mods[l], row, y_ret, y_na, gates, norm_post_mix[l], norm_pre_ffn[l],
                     norm_post_ffn[l], w_ret_b[l], w_na_b[l], w_o_b[l], w_g_b[l], w_u_b[l],
                     w_d_b[l])

    ctx_b, ctx_l, _ = x_prompt.shape
    flat = lambda a: a.reshape(1, ctx_b * ctx_l, a.shape[-1])
    per_seq = lambda a: a.reshape(ctx_b, ctx_l, a.shape[-1])
    x = flat(x_prompt)
    new_k = new_v = new_s = None
    for l in range(DEPTH):
        rq, rk, rv, rg, nq, new_k, new_v, gates = _premix(
            x, mods[l], ctx_row, norm_pre_mix[l], w_in_b[l], w_gate_b[l],
            cache=(l, ctx_l, new_k, new_v))
        y_ret, new_s = _retention(log_g[l], per_seq(rq), per_seq(rk), per_seq(rv), per_seq(rg),
                                  ret_gn_gain[l], new_state=(l, new_s))
        y_na = _ctx_attention(per_seq(nq), new_k, new_v, l)
        x = tail(x, l, ctx_row, flat(y_ret), flat(y_na), gates)
    y_prompt = per_seq(x)
    new_k = new_k.reshape(ctx_b, DEPTH, ctx_l, NA_HEADS, NA_DH)
    new_v = new_v.reshape(ctx_b, DEPTH, ctx_l, NA_HEADS, NA_DH)

    x = x_sample
    for l in range(DEPTH):
        rq, rk, rv, rg, nq, nk, nv, gates = _premix(
            x, mods[l], dec_row, norm_pre_mix[l], w_in_b[l], w_gate_b[l], rope_tabs=rope_tabs)
        (y_ret,) = _retention(log_g[l], rq, rk, rv, rg, ret_gn_gain[l], state=state_ret, layer=l)
        y_na = _na_attention(nq, nk, nv, ck[l], cv[l], bias_rows, l)
        x = tail(x, l, dec_row, y_ret, y_na, gates)
    return (y_prompt, x, new_k, new_v, new_s)
```
